```python
import jax, jax.numpy as jnp
from jax import lax
import numpy as np

D_MODEL = 2048
BATCH = 1
SEQ = 16384
DEPTH = 4
DEC_BATCH = 8
DEC_SEQ = 32
PAST_LEN = 1024

CHUNK = 64
Q_BLOCK = 128
K_BLOCK = 128
SPAN = 1024
HEAD_DIM = 128
N_HEADS_A = 4
N_HEADS_B = 4
N_HEADS_C = 4
W_A = N_HEADS_A * HEAD_DIM
W_B = N_HEADS_B * HEAD_DIM
MLA_Q_RANK = 512
MLA_KV_RANK = 256
MLA_NOPE_DIM = 128
MLA_ROPE_DIM = 64
MLA_V_DIM = 128
MLA_QK_DIM = MLA_NOPE_DIM + MLA_ROPE_DIM
W_C = N_HEADS_C * MLA_V_DIM
MIX_WIDTH = W_A + W_B + W_C
ROPE_THETA = 10000.0
D_FF = ((8 * D_MODEL // 3 + 255) // 256) * 256
RMS_EPS = 1e-6
NEG_INF = -1e30
SPLIT_SIZES = (W_A, W_A, W_A, N_HEADS_A, W_B, W_B, W_B, MLA_Q_RANK, MLA_KV_RANK, MLA_ROPE_DIM)
N_IN = W_A * 3 + N_HEADS_A + W_B * 3 + MLA_Q_RANK + MLA_KV_RANK + MLA_ROPE_DIM

kernel_name = 'hymba_fox_stickbreak_mla_stream_step'


def rmsnorm(x, g):
    x32 = x.astype(jnp.float32)
    y = x32 * lax.rsqrt(jnp.mean(x32 * x32, axis=-1, keepdims=True) + RMS_EPS)
    return (y * g.astype(jnp.float32)).astype(x.dtype)


def split_columns(proj):
    parts, start = [], 0
    for size in SPLIT_SIZES:
        parts.append(proj[..., start:start + size])
        start += size
    return parts


def rope(r, pos):
    half = MLA_ROPE_DIM // 2
    inv_freq = ROPE_THETA ** (-(jnp.arange(half, dtype=jnp.float32) / half))
    ang = pos.astype(jnp.float32)[:, None] * inv_freq[None, :]
    cos = jnp.cos(ang)[:, None, :]
    sin = jnp.sin(ang)[:, None, :]
    r32 = r.astype(jnp.float32)
    r1, r2 = r32[..., :half], r32[..., half:]
    return jnp.concatenate([r1 * cos - r2 * sin, r2 * cos + r1 * sin], axis=-1).astype(r.dtype)


def rope_tail(x, pos):
    return jnp.concatenate([x[..., :MLA_NOPE_DIM], rope(x[..., MLA_NOPE_DIM:], pos)], axis=-1)


def masked_softmax_attend(s, mask, v):
    logits = jnp.where(mask, s, NEG_INF)
    e = jnp.exp(logits - jnp.max(logits, axis=-1, keepdims=True))
    denom = jnp.sum(e, axis=-1)
    out = jnp.einsum('bhqk,bkhd->bqhd', e.astype(v.dtype), v, preferred_element_type=jnp.float32)
    return (out / jnp.transpose(denom, (0, 2, 1))[..., None]).astype(v.dtype)


def fox_attend(q, fq, k, v, fk, qpos, kpos):
    s = jnp.einsum('bqhd,bkhd->bhqk', q, k, preferred_element_type=jnp.float32)
    decay = jnp.transpose(fq, (0, 2, 1))[..., :, None] - jnp.transpose(fk, (0, 2, 1))[..., None, :]
    mask = kpos[None, :] <= qpos[:, None]
    return masked_softmax_attend(s + decay, mask, v)


def rev_cumsum_keys(x):
    L = x.shape[-1]
    nk = -(-L // K_BLOCK)
    xp = jnp.pad(x, [(0, 0)] * (x.ndim - 1) + [(0, nk * K_BLOCK - L)])
    xb = xp.reshape(*x.shape[:-1], nk, K_BLOCK)
    idx = jnp.arange(K_BLOCK)
    tri = (idx[:, None] >= idx[None, :]).astype(x.dtype)
    intra = jnp.einsum('bhqnj,js->bhqns', xb, tri, precision=lax.Precision.HIGHEST)
    bidx = jnp.arange(nk)
    tri_b = (bidx[:, None] > bidx[None, :]).astype(x.dtype)
    later = jnp.einsum('bhqm,mn->bhqn', jnp.sum(xb, axis=-1), tri_b, precision=lax.Precision.HIGHEST)
    r = (intra + later[..., None]).reshape(*x.shape[:-1], nk * K_BLOCK)
    return r[..., :L]


def sb_attend(q, k, v, qpos, kpos):
    z = jnp.einsum('bqhd,bkhd->bhqk', q, k, preferred_element_type=jnp.float32)
    mask = kpos[None, :] < qpos[:, None]
    log_keep = -(jnp.maximum(z, 0.0) + jnp.log1p(jnp.exp(-jnp.abs(z))))
    r = rev_cumsum_keys(jnp.where(mask, log_keep, 0.0))
    a = jnp.where(mask, jnp.exp(z + r), 0.0)
    return jnp.einsum('bhqk,bkhd->bqhd', a.astype(v.dtype), v)


def mla_attend(q, k, v, qpos, kpos):
    s = jnp.einsum('bqhd,bkhd->bhqk', q, k, preferred_element_type=jnp.float32)
    mask = (kpos[None, :] // CHUNK) <= (qpos[:, None] // CHUNK)
    return masked_softmax_attend(s, mask, v)


def query_sweep(attend, q_args, kv_args, qpos, kpos, P):
    T = qpos.shape[0]
    if T < Q_BLOCK or T % Q_BLOCK != 0:
        return attend(*q_args, *kv_args, qpos, kpos)
    outs = []
    for a in range(0, T, SPAN):
        b = min(a + SPAN, T)
        n_keys = P + b
        kv_prefix = tuple(arr[:, :n_keys] for arr in kv_args)
        kp = kpos[:n_keys]
        nb = (b - a) // Q_BLOCK

        def to_blocks(arr):
            arr = arr[:, a:b]
            return jnp.moveaxis(arr.reshape(arr.shape[0], nb, Q_BLOCK, *arr.shape[2:]), 1, 0)

        blocks = tuple(to_blocks(arr) for arr in q_args) + (qpos[a:b].reshape(nb, Q_BLOCK),)
        out = lax.map(lambda args, kvp=kv_prefix, kpp=kp: attend(*args[:-1], *kvp, args[-1], kpp), blocks)
        out = jnp.moveaxis(out, 0, 1)
        outs.append(out.reshape(out.shape[0], b - a, *out.shape[3:]))
    return jnp.concatenate(outs, axis=1)


def trunk_layer(x, past, p):
    B, T, _ = x.shape
    P = 0 if past is None else past[0].shape[1]
    qpos = P + jnp.arange(T, dtype=jnp.int32)
    kpos = jnp.arange(P + T, dtype=jnp.int32)
    Tk = P + T

    h = rmsnorm(x, p['norm_mix'])
    proj = jnp.einsum('btd,dn->btn', h, p['w_in'])
    qa, ka, va, fa, qb, kb, vb, cq, ckv, krope = split_columns(proj)

    qa = rmsnorm(qa.reshape(B, T, N_HEADS_A, HEAD_DIM), p['fox_q_norm']) * (HEAD_DIM ** -0.5)
    ka = rmsnorm(ka.reshape(B, T, N_HEADS_A, HEAD_DIM), p['fox_k_norm'])
    va = va.reshape(B, T, N_HEADS_A, HEAD_DIM)
    logf = jax.nn.log_sigmoid(fa.astype(jnp.float32) + p['fox_f_bias'].astype(jnp.float32))
    qb = qb.reshape(B, T, N_HEADS_B, HEAD_DIM) * (HEAD_DIM ** -0.5)
    kb = kb.reshape(B, T, N_HEADS_B, HEAD_DIM)
    vb = vb.reshape(B, T, N_HEADS_B, HEAD_DIM)
    cq = rmsnorm(cq, p['mla_qa_norm'])
    q_c = jnp.einsum('btr,rn->btn', cq, p['mla_w_uq']).reshape(B, T, N_HEADS_C, MLA_QK_DIM)
    q_c = rope_tail(rmsnorm(q_c, p['mla_q_norm']), qpos) * (MLA_QK_DIM ** -0.5)
    ckv = rmsnorm(ckv, p['mla_kva_norm'])

    new_rows = (ka, va, logf, kb, vb, ckv, krope)
    if past is None:
        ka_all, va_all, logf_all, kb_all, vb_all, ckv_all, krope_all = new_rows
    else:
        ka_all, va_all, logf_all, kb_all, vb_all, ckv_all, krope_all = (
            jnp.concatenate([old.astype(new.dtype), new], axis=1) for old, new in zip(past, new_rows))

    f_all = jnp.cumsum(logf_all.astype(jnp.float32), axis=1)
    f_q = f_all[:, P:]

    kv_c = jnp.einsum('bkr,rn->bkn', ckv_all, p['mla_w_ukv']).reshape(B, Tk, N_HEADS_C, MLA_NOPE_DIM + MLA_V_DIM)
    k_nope, v_c = kv_c[..., :MLA_NOPE_DIM], kv_c[..., MLA_NOPE_DIM:]
    k_rope = jnp.broadcast_to(krope_all[:, :, None, :].astype(k_nope.dtype), (B, Tk, N_HEADS_C, MLA_ROPE_DIM))
    k_c = rope_tail(rmsnorm(jnp.concatenate([k_nope, k_rope], axis=-1), p['mla_k_norm']), kpos)

    out_a = query_sweep(fox_attend, (qa, f_q), (ka_all, va_all, f_all), qpos, kpos, P)
    out_b = query_sweep(sb_attend, (qb,), (kb_all, vb_all), qpos, kpos, P)
    out_c = query_sweep(mla_attend, (q_c,), (k_c, v_c), qpos, kpos, P)

    mix = jnp.concatenate([
        rmsnorm(out_a.reshape(B, T, W_A), p['out_norm_a']),
        rmsnorm(out_b.reshape(B, T, W_B), p['out_norm_b']),
        rmsnorm(out_c.reshape(B, T, W_C), p['out_norm_c'])], axis=-1)
    x = x + jnp.einsum('btm,md->btd', mix, p['w_out'])

    h2 = rmsnorm(x, p['norm_ffn'])
    gu = jnp.einsum('btd,df->btf', h2, p['w_gu'])
    act = jax.nn.silu(gu[..., :D_FF]) * gu[..., D_FF:]
    x = x + jnp.einsum('btf,fd->btd', act, p['w_down'])
    return x, new_rows


def setup_inputs(seed: int = 0) -> dict:
    key = jax.random.key(seed)
    ks = iter(jax.random.split(key, 32))

    def normal(shape, scale):
        return jax.random.normal(next(ks), shape, jnp.float32) * scale

    def gain(shape):
        return 1.0 + normal(shape, 0.02)

    L = DEPTH
    return {
        'x_prompt': normal((BATCH, SEQ, D_MODEL), 1.0),
        'x_sample': normal((DEC_BATCH, DEC_SEQ, D_MODEL), 1.0),
        'cache_fox_k': normal((L, DEC_BATCH, PAST_LEN, N_HEADS_A, HEAD_DIM), 1.0),
        'cache_fox_v': normal((L, DEC_BATCH, PAST_LEN, N_HEADS_A, HEAD_DIM), 1.0),
        'cache_fox_logf': jax.nn.log_sigmoid(3.0 + normal((L, DEC_BATCH, PAST_LEN, N_HEADS_A), 1.0)),
        'cache_sb_k': normal((L, DEC_BATCH, PAST_LEN, N_HEADS_B, HEAD_DIM), 1.0),
        'cache_sb_v': normal((L, DEC_BATCH, PAST_LEN, N_HEADS_B, HEAD_DIM), 1.0),
        'cache_mla_ckv': normal((L, DEC_BATCH, PAST_LEN, MLA_KV_RANK), 1.0),
        'cache_mla_krope': normal((L, DEC_BATCH, PAST_LEN, MLA_ROPE_DIM), 1.0),
        'norm_mix': gain((L, D_MODEL)),
        'w_in': normal((L, D_MODEL, N_IN), D_MODEL ** -0.5),
        'fox_f_bias': 3.0 + normal((L, N_HEADS_A), 0.1),
        'fox_q_norm': gain((L, HEAD_DIM)),
        'fox_k_norm': gain((L, HEAD_DIM)),
        'mla_qa_norm': gain((L, MLA_Q_RANK)),
        'mla_w_uq': normal((L, MLA_Q_RANK, N_HEADS_C * MLA_QK_DIM), MLA_Q_RANK ** -0.5),
        'mla_kva_norm': gain((L, MLA_KV_RANK)),
        'mla_w_ukv': normal((L, MLA_KV_RANK, N_HEADS_C * (MLA_NOPE_DIM + MLA_V_DIM)), MLA_KV_RANK ** -0.5),
        'mla_q_norm': gain((L, MLA_QK_DIM)),
        'mla_k_norm': gain((L, MLA_QK_DIM)),
        'out_norm_a': gain((L, W_A)),
        'out_norm_b': gain((L, W_B)),
        'out_norm_c': gain((L, W_C)),
        'w_out': normal((L, MIX_WIDTH, D_MODEL), MIX_WIDTH ** -0.5),
        'norm_ffn': gain((L, D_MODEL)),
        'w_gu': normal((L, D_MODEL, 2 * D_FF), D_MODEL ** -0.5),
        'w_down': normal((L, D_FF, D_MODEL), D_FF ** -0.5),
    }


def reference(x_prompt, x_sample, cache_fox_k, cache_fox_v, cache_fox_logf, cache_sb_k, cache_sb_v,
              cache_mla_ckv, cache_mla_krope, norm_mix, w_in, fox_f_bias, fox_q_norm, fox_k_norm,
              mla_qa_norm, mla_w_uq, mla_kva_norm, mla_w_ukv, mla_q_norm, mla_k_norm,
              out_norm_a, out_norm_b, out_norm_c, w_out, norm_ffn, w_gu, w_down):
    y_p, y_s = x_prompt, x_sample
    rows_p, rows_s = [], []
    for l in range(DEPTH):
        p = {
            'norm_mix': norm_mix[l], 'w_in': w_in[l], 'fox_f_bias': fox_f_bias[l],
            'fox_q_norm': fox_q_norm[l], 'fox_k_norm': fox_k_norm[l],
            'mla_qa_norm': mla_qa_norm[l], 'mla_w_uq': mla_w_uq[l], 'mla_kva_norm': mla_kva_norm[l],
            'mla_w_ukv': mla_w_ukv[l], 'mla_q_norm': mla_q_norm[l], 'mla_k_norm': mla_k_norm[l],
            'out_norm_a': out_norm_a[l], 'out_norm_b': out_norm_b[l], 'out_norm_c': out_norm_c[l],
            'w_out': w_out[l], 'norm_ffn': norm_ffn[l], 'w_gu': w_gu[l], 'w_down': w_down[l],
        }
        y_p, r_p = trunk_layer(y_p, None, p)
        past = (cache_fox_k[l], cache_fox_v[l], cache_fox_logf[l], cache_sb_k[l], cache_sb_v[l],
                cache_mla_ckv[l], cache_mla_krope[l])
        y_s, r_s = trunk_layer(y_s, past, p)
        rows_p.append(r_p)
        rows_s.append(r_s)
    fox_k_p, fox_v_p, fox_logf_p, sb_k_p, sb_v_p, mla_ckv_p, mla_krope_p = (
        jnp.stack([r[i] for r in rows_p], axis=0) for i in range(7))
    fox_k_s, fox_v_s, fox_logf_s, sb_k_s, sb_v_s, mla_ckv_s, mla_krope_s = (
        jnp.stack([r[i] for r in rows_s], axis=0) for i in range(7))
    return (y_p, y_s, fox_k_p, fox_v_p, fox_logf_p, sb_k_p, sb_v_p, mla_ckv_p, mla_krope_p,
            fox_k_s, fox_v_s, fox_logf_s, sb_k_s, sb_v_s, mla_ckv_s, mla_krope_s)
```

```python
import functools

import jax
import jax.numpy as jnp
from jax import lax
from jax.experimental import pallas as pl
from jax.experimental.pallas import tpu as pltpu

D_MODEL = 2048
DEPTH = 4
CHUNK = 64
HEAD_DIM = 128
N_HEADS = 4
W_GROUP = N_HEADS * HEAD_DIM
MLA_Q_RANK = 512
MLA_KV_RANK = 256
MLA_NOPE_DIM = 128
MLA_ROPE_DIM = 64
MLA_QK_DIM = MLA_NOPE_DIM + MLA_ROPE_DIM
MLA_HEAD_PAD = 256
ROPE_THETA = 10000.0
D_FF = 5632
RMS_EPS = 1e-6
NEG_INF = -1e30
N_IN_PAD = 4096
LANES = 128
TRI = 256
VMEM_LIMIT = 56 * 1024 * 1024

C_QA, C_KA, C_VA, C_QB, C_KB, C_VB, C_CQ, C_CKV, C_KR, C_FA = (
    0, 512, 1024, 1536, 2048, 2560, 3072, 3584, 3840, 3968)

BF16 = jnp.bfloat16
F32 = jnp.float32


def _params(*sem):
    return pltpu.CompilerParams(dimension_semantics=sem, vmem_limit_bytes=VMEM_LIMIT)


def _pick_tile(n, pref):
    t = min(n, pref)
    while n % t:
        t //= 2
    return t


def _rms(x, g):
    return x * lax.rsqrt(jnp.mean(x * x, axis=-1, keepdims=True) + RMS_EPS) * g


def _dot(a, b):
    return jnp.dot(a, b, preferred_element_type=F32)


def _dot_nt(a, b):
    return lax.dot_general(a, b, (((1,), (1,)), ((), ())), preferred_element_type=F32)


def _split2(x):
    hi = x.astype(BF16)
    lo = (x - hi.astype(F32)).astype(BF16)
    return hi, lo


def _split3(x):
    a = x.astype(BF16)
    r = x - a.astype(F32)
    b = r.astype(BF16)
    c = (r - b.astype(F32)).astype(BF16)
    return a, b, c


def _log_sigmoid(x):
    return -(jnp.maximum(-x, 0.0) + jnp.log1p(jnp.exp(-jnp.abs(x))))


def _rms_matmul_kernel(x_ref, g_ref, w_ref, o_ref, h_ref):
    @pl.when(pl.program_id(1) == 0)
    def _():
        h_ref[...] = _rms(x_ref[...], g_ref[...]).astype(BF16)

    o_ref[...] = _dot(h_ref[...], w_ref[...])


def rms_matmul(x, g, w):
    m, k = x.shape
    n = w.shape[1]
    tm = _pick_tile(m, 512)
    tn = _pick_tile(n, 1024)
    return pl.pallas_call(
        _rms_matmul_kernel,
        grid=(m // tm, n // tn),
        in_specs=[pl.BlockSpec((tm, k), lambda i, j: (i, 0)),
                  pl.BlockSpec((1, k), lambda i, j: (0, 0)),
                  pl.BlockSpec((k, tn), lambda i, j: (0, j))],
        out_specs=pl.BlockSpec((tm, tn), lambda i, j: (i, j)),
        out_shape=jax.ShapeDtypeStruct((m, n), F32),
        scratch_shapes=[pltpu.VMEM((tm, k), BF16)],
        compiler_params=_params("parallel", "arbitrary"),
        name="rms_matmul",
    )(x, g, w)


def _rope_chunk(y2, cos, sin):
    return y2 * cos + pltpu.roll(y2, 64, 1) * sin


def _rope_lane_mask():
    return lax.broadcasted_iota(jnp.int32, (1, LANES), 1) < MLA_ROPE_DIM


def _post_kernel(proj_ref, fqg_ref, fkg_ref, fb_ref, qag_ref, wuq_ref, qg_ref, kvg_ref, cos_ref, sin_ref,
                 qa_o, ka_o, kab_o, va_o, vab_o, lf_o, qb_o, kb_o, kbb_o, vb_o, vbb_o, qc_o, ckv_o, kr_o):
    scale = HEAD_DIM ** -0.5
    for h in range(N_HEADS):
        sl = slice(h * HEAD_DIM, (h + 1) * HEAD_DIM)
        qa = proj_ref[:, C_QA + h * HEAD_DIM:C_QA + (h + 1) * HEAD_DIM]
        qa_o[:, sl] = (_rms(qa, fqg_ref[...]) * scale).astype(BF16)
        ka = _rms(proj_ref[:, C_KA + h * HEAD_DIM:C_KA + (h + 1) * HEAD_DIM], fkg_ref[...])
        ka_o[:, sl] = ka
        kab_o[:, sl] = ka.astype(BF16)
    va = proj_ref[:, C_VA:C_VA + W_GROUP]
    va_o[...] = va
    vab_o[...] = va.astype(BF16)
    qb_o[...] = (proj_ref[:, C_QB:C_QB + W_GROUP] * scale).astype(BF16)
    kb = proj_ref[:, C_KB:C_KB + W_GROUP]
    kb_o[...] = kb
    kbb_o[...] = kb.astype(BF16)
    vb = proj_ref[:, C_VB:C_VB + W_GROUP]
    vb_o[...] = vb
    vbb_o[...] = vb.astype(BF16)

    lane = lax.broadcasted_iota(jnp.int32, (1, LANES), 1)
    lf = _log_sigmoid(proj_ref[:, C_FA:C_FA + LANES] + fb_ref[...])
    lf_o[...] = jnp.where(lane < N_HEADS, lf, 0.0)

    cqn = _rms(proj_ref[:, C_CQ:C_CQ + MLA_Q_RANK], qag_ref[...]).astype(BF16)
    qc = _dot(cqn, wuq_ref[...])
    rmask = _rope_lane_mask()
    cos = cos_ref[...]
    sin = sin_ref[...]
    for h in range(N_HEADS):
        c1 = qc[:, h * MLA_HEAD_PAD:h * MLA_HEAD_PAD + LANES]
        c2 = qc[:, h * MLA_HEAD_PAD + LANES:(h + 1) * MLA_HEAD_PAD]
        ss = (jnp.sum(c1 * c1, axis=-1, keepdims=True)
              + jnp.sum(jnp.where(rmask, c2 * c2, 0.0), axis=-1, keepdims=True))
        rs = lax.rsqrt(ss * (1.0 / MLA_QK_DIM) + RMS_EPS)
        y1 = c1 * rs * qg_ref[:, :LANES]
        y2 = c2 * rs * qg_ref[:, LANES:]
        qc_o[:, h * MLA_HEAD_PAD:h * MLA_HEAD_PAD + LANES] = y1.astype(BF16)
        qc_o[:, h * MLA_HEAD_PAD + LANES:(h + 1) * MLA_HEAD_PAD] = _rope_chunk(y2, cos, sin).astype(BF16)

    ckv_o[...] = _rms(proj_ref[:, C_CKV:C_CKV + MLA_KV_RANK], kvg_ref[...])
    kr_o[...] = proj_ref[:, C_KR:C_KR + LANES]


def post(proj, p, cos, sin):
    m = proj.shape[0]
    tm = _pick_tile(m, 512)
    row = lambda w: pl.BlockSpec((tm, w), lambda i: (i, 0))
    full = lambda a: pl.BlockSpec(a.shape, lambda i: (0,) * a.ndim)
    consts = (p["fox_q_norm"], p["fox_k_norm"], p["fox_f_bias"], p["mla_qa_norm"], p["mla_w_uq"],
              p["mla_q_gain"], p["mla_kva_norm"])
    outs = [(W_GROUP, BF16), (W_GROUP, F32), (W_GROUP, BF16), (W_GROUP, F32), (W_GROUP, BF16), (LANES, F32),
            (W_GROUP, BF16), (W_GROUP, F32), (W_GROUP, BF16), (W_GROUP, F32), (W_GROUP, BF16),
            (N_HEADS * MLA_HEAD_PAD, BF16), (MLA_KV_RANK, F32), (LANES, F32)]
    return pl.pallas_call(
        _post_kernel,
        grid=(m // tm,),
        in_specs=[row(N_IN_PAD)] + [full(a) for a in consts] + [row(LANES), row(LANES)],
        out_specs=[row(w) for w, _ in outs],
        out_shape=[jax.ShapeDtypeStruct((m, w), dt) for w, dt in outs],
        compiler_params=_params("parallel"),
        name="post",
    )(proj, *consts, cos, sin)


def _kexp_kernel(ckv_ref, kr_ref, w_ref, kg_ref, cos_ref, sin_ref, kc_o, vc_o):
    kv = _dot(ckv_ref[...].astype(BF16), w_ref[...])
    kr = kr_ref[...]
    krsq = jnp.sum(jnp.where(_rope_lane_mask(), kr * kr, 0.0), axis=-1, keepdims=True)
    cos = cos_ref[...]
    sin = sin_ref[...]
    for h in range(N_HEADS):
        n = kv[:, h * LANES:(h + 1) * LANES]
        ss = jnp.sum(n * n, axis=-1, keepdims=True) + krsq
        rs = lax.rsqrt(ss * (1.0 / MLA_QK_DIM) + RMS_EPS)
        kc_o[:, h * MLA_HEAD_PAD:h * MLA_HEAD_PAD + LANES] = (n * rs * kg_ref[:, :LANES]).astype(BF16)
        y2 = kr * rs * kg_ref[:, LANES:]
        kc_o[:, h * MLA_HEAD_PAD + LANES:(h + 1) * MLA_HEAD_PAD] = _rope_chunk(y2, cos, sin).astype(BF16)
    vc_o[...] = kv[:, W_GROUP:].astype(BF16)


def kexp(ckv, krdup, p, cos, sin):
    m = ckv.shape[0]
    tm = _pick_tile(m, 512)
    row = lambda w: pl.BlockSpec((tm, w), lambda i: (i, 0))
    full = lambda a: pl.BlockSpec(a.shape, lambda i: (0,) * a.ndim)
    return pl.pallas_call(
        _kexp_kernel,
        grid=(m // tm,),
        in_specs=[row(MLA_KV_RANK), row(LANES), full(p["mla_w_ukv"]), full(p["mla_k_gain"]),
                  row(LANES), row(LANES)],
        out_specs=[row(N_HEADS * MLA_HEAD_PAD), row(W_GROUP)],
        out_shape=[jax.ShapeDtypeStruct((m, N_HEADS * MLA_HEAD_PAD), BF16),
                   jax.ShapeDtypeStruct((m, W_GROUP), BF16)],
        compiler_params=_params("parallel"),
        name="kexp",
    )(ckv, krdup, p["mla_w_ukv"], p["mla_k_gain"], cos, sin)


def _cumsum_kernel(x_ref, o_ref, carry_ref):
    @pl.when(pl.program_id(1) == 0)
    def _():
        carry_ref[...] = jnp.zeros_like(carry_ref)

    t = x_ref.shape[1]
    xt = x_ref[0].T[:8, :]
    ri = lax.broadcasted_iota(jnp.int32, (t, t), 0)
    ci = lax.broadcasted_iota(jnp.int32, (t, t), 1)
    upper = jnp.where(ri <= ci, 1.0, 0.0).astype(BF16)
    a, b, c = _split3(xt)
    f = _dot(a, upper) + _dot(b, upper) + _dot(c, upper) + carry_ref[:, :1]
    o_ref[0] = f
    carry_ref[...] = jnp.broadcast_to(f[:, t - 1:t], carry_ref.shape)


def cumsum_rows(x):
    b, n, _ = x.shape
    t = n if n <= 1280 else _pick_tile(n, 512)
    return pl.pallas_call(
        _cumsum_kernel,
        grid=(b, n // t),
        in_specs=[pl.BlockSpec((1, t, LANES), lambda i, j: (i, j, 0))],
        out_specs=pl.BlockSpec((1, 8, t), lambda i, j: (i, 0, j)),
        out_shape=jax.ShapeDtypeStruct((b, 8, n), F32),
        scratch_shapes=[pltpu.VMEM((8, LANES), F32)],
        compiler_params=_params("parallel", "arbitrary"),
        name="cumsum_rows",
    )(x)


def _softmax_tile(s, v, m_prev, l_prev, acc_prev):
    m_new = jnp.maximum(m_prev, jnp.max(s, axis=-1, keepdims=True))
    alpha = jnp.exp(m_prev - m_new)
    pr = jnp.exp(s - m_new)
    l_new = alpha * l_prev + jnp.sum(pr, axis=-1, keepdims=True)
    acc_new = alpha * acc_prev + _dot(pr.astype(BF16), v)
    return m_new, l_new, acc_new


def _tri_lower():
    ri = lax.broadcasted_iota(jnp.int32, (TRI, TRI), 0)
    ci = lax.broadcasted_iota(jnp.int32, (TRI, TRI), 1)
    return jnp.where(ri >= ci, 1.0, 0.0).astype(BF16)


def _sb_tile(q, k, v, carry, mask):
    z = _dot_nt(q, k)
    lk = -(jnp.maximum(z, 0.0) + jnp.log1p(jnp.exp(-jnp.abs(z))))
    if mask is not None:
        lk = jnp.where(mask, lk, 0.0)
    tk = k.shape[0]
    blk = min(TRI, tk)
    tri = _tri_lower()[:blk, :blk]
    parts = [None] * (tk // blk)
    for c in reversed(range(tk // blk)):
        lkc = lk[:, c * blk:(c + 1) * blk]
        hi, lo = _split2(lkc)
        parts[c] = _dot(hi, tri) + _dot(lo, tri) + carry
        carry = carry + jnp.sum(lkc, axis=-1, keepdims=True)
    r = parts[0] if len(parts) == 1 else jnp.concatenate(parts, axis=-1)
    a = jnp.exp(z + r)
    if mask is not None:
        a = jnp.where(mask, a, 0.0)
    return _dot(a.astype(BF16), v), carry


def _tile_mask(kind, qpos, kpos):
    if kind == "fox":
        return kpos <= qpos
    if kind == "sb":
        return kpos < qpos
    shift = CHUNK.bit_length() - 1
    return lax.shift_right_logical(kpos, shift) <= lax.shift_right_logical(qpos, shift)


def _attn_prompt_kernel(kind, tq, *refs):
    if kind == "fox":
        q_ref, k_ref, v_ref, f_ref, o_ref, m_ref, l_ref, acc_ref = refs
    else:
        q_ref, k_ref, v_ref, o_ref, m_ref, l_ref, acc_ref = refs
        f_ref = None
    qb = pl.program_id(1)
    q = q_ref[...]
    qpos = lax.broadcasted_iota(jnp.int32, (tq, tq), 0)
    kpos = lax.broadcasted_iota(jnp.int32, (tq, tq), 1)
    mask = _tile_mask(kind, qpos, kpos)
    fref = f_ref[0, qb][:, :1] if f_ref is not None else None

    def tile(kb, masked):
        start = pl.multiple_of(kb * tq, tq)
        k = k_ref[pl.ds(start, tq), :]
        v = v_ref[pl.ds(start, tq), :]
        if kind == "sb":
            out, carry = _sb_tile(q, k, v, m_ref[...], mask if masked else None)
            m_ref[...] = carry
            acc_ref[...] += out
            return
        s = _dot_nt(q, k)
        if kind == "fox":
            s = s + (fref - f_ref[0, kb])
        if masked:
            s = jnp.where(mask, s, NEG_INF)
        m_new, l_new, acc_new = _softmax_tile(s, v, m_ref[...], l_ref[...], acc_ref[...])
        m_ref[...] = m_new
        l_ref[...] = l_new
        acc_ref[...] = acc_new

    m_ref[...] = jnp.full(m_ref.shape, 0.0 if kind == "sb" else NEG_INF, F32)
    l_ref[...] = jnp.zeros_like(l_ref)
    acc_ref[...] = jnp.zeros_like(acc_ref)
    tile(qb, True)

    def body(i, c):
        tile(qb - 1 - i, False)
        return c

    lax.fori_loop(0, qb, body, 0)
    if kind == "sb":
        o_ref[...] = acc_ref[...]
    else:
        o_ref[...] = acc_ref[...] / l_ref[...]


def attn_prompt(kind, q, k, v, f=None):
    t = q.shape[0]
    dq = q.shape[1] // N_HEADS
    tq = _pick_tile(t, 512)
    nq = t // tq
    in_specs = [pl.BlockSpec((tq, dq), lambda h, i: (i, h)),
                pl.BlockSpec((t, dq), lambda h, i: (0, h)),
                pl.BlockSpec((t, HEAD_DIM), lambda h, i: (0, h))]
    args = [q, k, v]
    if kind == "fox":
        in_specs.append(pl.BlockSpec((1, nq, 1, tq), lambda h, i: (h, 0, 0, 0)))
        args.append(f.reshape(N_HEADS, nq, 1, tq))
    return pl.pallas_call(
        functools.partial(_attn_prompt_kernel, kind, tq),
        grid=(N_HEADS, nq),
        in_specs=in_specs,
        out_specs=pl.BlockSpec((tq, HEAD_DIM), lambda h, i: (i, h)),
        out_shape=jax.ShapeDtypeStruct((t, W_GROUP), F32),
        scratch_shapes=[pltpu.VMEM((tq, 1), F32), pltpu.VMEM((tq, 1), F32), pltpu.VMEM((tq, HEAD_DIM), F32)],
        compiler_params=_params("parallel", "arbitrary"),
        name="attn_prompt_" + kind,
    )(*args)


def _attn_decode_kernel(kind, *refs):
    if kind == "fox":
        q_ref, kn_ref, vn_ref, kc_ref, vc_ref, fn_ref, fc_ref, o_ref = refs
    else:
        q_ref, kn_ref, vn_ref, kc_ref, vc_ref, o_ref = refs
    q = q_ref[...]
    tq = q.shape[0]
    past = kc_ref.shape[0]
    kn = kn_ref[...]
    vn = vn_ref[...]
    kc = kc_ref[...].astype(BF16)
    vc = vc_ref[...].astype(BF16)
    qpos = past + lax.broadcasted_iota(jnp.int32, (tq, tq), 0)
    kpos = past + lax.broadcasted_iota(jnp.int32, (tq, tq), 1)
    mask = _tile_mask(kind, qpos, kpos)
    if kind == "sb":
        out_n, carry = _sb_tile(q, kn, vn, jnp.zeros((tq, 1), F32), mask)
        out_c, _ = _sb_tile(q, kc, vc, carry, None)
        o_ref[...] = out_n + out_c
        return
    s_n = _dot_nt(q, kn)
    s_c = _dot_nt(q, kc)
    if kind == "fox":
        fref = fn_ref[:, :1]
        s_n = s_n + (fref - fn_ref[...])
        s_c = s_c + (fref - fc_ref[...])
    s_n = jnp.where(mask, s_n, NEG_INF)
    m0 = jnp.full((tq, 1), NEG_INF, F32)
    z0 = jnp.zeros((tq, 1), F32)
    m, l, acc = _softmax_tile(s_n, vn, m0, z0, jnp.zeros((tq, HEAD_DIM), F32))
    m, l, acc = _softmax_tile(s_c, vc, m, l, acc)
    o_ref[...] = acc / l


def attn_decode(kind, q, kn, vn, kc, vc, fn=None, fc=None):
    nb, past = kc.shape[0], kc.shape[1]
    tq = q.shape[0] // nb
    dq = q.shape[1] // N_HEADS
    new = lambda d: pl.BlockSpec((tq, d), lambda b, h: (b, h))
    old = lambda d: pl.BlockSpec((None, past, d), lambda b, h: (b, 0, h))
    in_specs = [new(dq), new(dq), new(HEAD_DIM), old(dq), old(HEAD_DIM)]
    args = [q, kn, vn, kc, vc]
    if kind == "fox":
        in_specs += [pl.BlockSpec((None, None, 1, tq), lambda b, h: (b, h, 0, 0)),
                     pl.BlockSpec((None, None, 1, past), lambda b, h: (b, h, 0, 0))]
        args += [fn, fc]
    return pl.pallas_call(
        functools.partial(_attn_decode_kernel, kind),
        grid=(nb, N_HEADS),
        in_specs=in_specs,
        out_specs=new(HEAD_DIM),
        out_shape=jax.ShapeDtypeStruct((nb * tq, W_GROUP), F32),
        compiler_params=_params("parallel", "parallel"),
        name="attn_decode_" + kind,
    )(*args)


def _out_proj_kernel(oa_ref, ob_ref, oc_ref, ga_ref, gb_ref, gc_ref, w_ref, x_ref, o_ref):
    acc = x_ref[...]
    for g, (o, gn) in enumerate(((oa_ref, ga_ref), (ob_ref, gb_ref), (oc_ref, gc_ref))):
        y = _rms(o[...], gn[...]).astype(BF16)
        acc = acc + _dot(y, w_ref[g * W_GROUP:(g + 1) * W_GROUP, :])
    o_ref[...] = acc


def out_proj(oa, ob, oc, p, x):
    m = x.shape[0]
    tm = _pick_tile(m, 512)
    row = lambda w: pl.BlockSpec((tm, w), lambda i: (i, 0))
    full = lambda a: pl.BlockSpec(a.shape, lambda i: (0,) * a.ndim)
    consts = (p["out_norm_a"], p["out_norm_b"], p["out_norm_c"], p["w_out"])
    return pl.pallas_call(
        _out_proj_kernel,
        grid=(m // tm,),
        in_specs=[row(W_GROUP)] * 3 + [full(a) for a in consts] + [row(D_MODEL)],
        out_specs=row(D_MODEL),
        out_shape=jax.ShapeDtypeStruct((m, D_MODEL), F32),
        compiler_params=_params("parallel"),
        name="out_proj",
    )(oa, ob, oc, *consts, x)


def _ffn_kernel(x_ref, g_ref, wg_ref, wu_ref, wd_ref, o_ref, h_ref):
    @pl.when(pl.program_id(1) == 0)
    def _():
        x = x_ref[...]
        h_ref[...] = _rms(x, g_ref[...]).astype(BF16)
        o_ref[...] = x

    h = h_ref[...]
    gate = _dot(h, wg_ref[...])
    up = _dot(h, wu_ref[...])
    act = (gate * jax.nn.sigmoid(gate) * up).astype(BF16)
    o_ref[...] += _dot(act, wd_ref[...])


def ffn(x, g, w_gu, w_down):
    m = x.shape[0]
    tm = _pick_tile(m, 512)
    tf = 512
    nf = D_FF // tf
    return pl.pallas_call(
        _ffn_kernel,
        grid=(m // tm, nf),
        in_specs=[pl.BlockSpec((tm, D_MODEL), lambda i, j: (i, 0)),
                  pl.BlockSpec((1, D_MODEL), lambda i, j: (0, 0)),
                  pl.BlockSpec((D_MODEL, tf), lambda i, j: (0, j)),
                  pl.BlockSpec((D_MODEL, tf), lambda i, j: (0, j + nf)),
                  pl.BlockSpec((tf, D_MODEL), lambda i, j: (j, 0))],
        out_specs=pl.BlockSpec((tm, D_MODEL), lambda i, j: (i, 0)),
        out_shape=jax.ShapeDtypeStruct((m, D_MODEL), F32),
        scratch_shapes=[pltpu.VMEM((tm, D_MODEL), BF16)],
        compiler_params=_params("parallel", "arbitrary"),
        name="ffn",
    )(x, g, w_gu, w_gu, w_down)


def _dup_rope(r):
    half = MLA_ROPE_DIM // 2
    return jnp.concatenate([r, r[..., half:], r[..., :half]], axis=-1)


def _prep_layer(l, fox_f_bias, fox_q_norm, fox_k_norm, mla_qa_norm, mla_w_uq, mla_kva_norm, mla_w_ukv,
                mla_q_norm, mla_k_norm, out_norm_a, out_norm_b, out_norm_c, w_in, w_out, w_gu, w_down,
                norm_mix, norm_ffn):
    w = w_in[l]
    off = [0]
    for s in (512, 512, 512, 4, 512, 512, 512, 512, 256, 64):
        off.append(off[-1] + s)
    qa, ka, va, fa, qb, kb, vb, cq, ckv, kr = (w[:, off[i]:off[i + 1]] for i in range(10))
    w_in_p = jnp.concatenate(
        [qa, ka, va, qb, kb, vb, cq, ckv, _dup_rope(kr), fa, jnp.zeros((D_MODEL, LANES - N_HEADS), F32)],
        axis=1).astype(BF16)

    uq = mla_w_uq[l].reshape(MLA_Q_RANK, N_HEADS, MLA_QK_DIM)
    uq = jnp.concatenate([uq[..., :MLA_NOPE_DIM], _dup_rope(uq[..., MLA_NOPE_DIM:])], axis=-1)
    ukv = mla_w_ukv[l].reshape(MLA_KV_RANK, N_HEADS, 2 * LANES)
    ukv = jnp.concatenate([ukv[..., :LANES].reshape(MLA_KV_RANK, W_GROUP),
                           ukv[..., LANES:].reshape(MLA_KV_RANK, W_GROUP)], axis=1)

    def gain256(g):
        return jnp.concatenate([g[:MLA_NOPE_DIM], _dup_rope(g[MLA_NOPE_DIM:])])[None, :]

    row = lambda a: a[l][None, :]
    return {
        "norm_mix": row(norm_mix), "w_in": w_in_p,
        "fox_q_norm": row(fox_q_norm), "fox_k_norm": row(fox_k_norm),
        "fox_f_bias": jnp.pad(fox_f_bias[l], (0, LANES - N_HEADS))[None, :],
        "mla_qa_norm": row(mla_qa_norm),
        "mla_w_uq": uq.reshape(MLA_Q_RANK, N_HEADS * MLA_HEAD_PAD).astype(BF16),
        "mla_q_gain": gain256(mla_q_norm[l]) * (MLA_QK_DIM ** -0.5),
        "mla_kva_norm": row(mla_kva_norm),
        "mla_w_ukv": ukv.astype(BF16),
        "mla_k_gain": gain256(mla_k_norm[l]),
        "out_norm_a": row(out_norm_a), "out_norm_b": row(out_norm_b), "out_norm_c": row(out_norm_c),
        "w_out": w_out[l].astype(BF16),
        "norm_ffn": row(norm_ffn), "w_gu": w_gu[l].astype(BF16), "w_down": w_down[l].astype(BF16),
    }


def _rope_tables(pos):
    half = MLA_ROPE_DIM // 2
    inv_freq = ROPE_THETA ** (-(jnp.arange(half, dtype=F32) / half))
    ang = pos.astype(F32)[:, None] * inv_freq[None, :]
    cos, sin = jnp.cos(ang), jnp.sin(ang)
    zero = jnp.zeros_like(cos)
    return (jnp.concatenate([cos, cos, zero, zero], axis=1),
            jnp.concatenate([-sin, sin, zero, zero], axis=1))


def _layer(x, past, p, tabs):
    nb, t, _ = x.shape
    m = nb * t
    x2 = x.reshape(m, D_MODEL)
    proj = rms_matmul(x2, p["norm_mix"], p["w_in"])
    (qa, ka, kab, va, vab, lf, qb, kb, kbb, vb, vbb, qc, ckv, krd) = post(proj, p, *tabs["q"])
    kc_new, vc_new = kexp(ckv, krd, p, *tabs["q"])
    if past is None:
        f = cumsum_rows(lf.reshape(nb, t, LANES))
        oa = attn_prompt("fox", qa, kab, vab, f[0, :N_HEADS])
        ob = attn_prompt("sb", qb, kbb, vbb)
        oc = attn_prompt("mla", qc, kc_new, vc_new)
    else:
        c_fk, c_fv, c_lf, c_sk, c_sv, c_ckv, c_kr = past
        pl_ = c_fk.shape[1]
        lf_all = jnp.concatenate(
            [jnp.pad(c_lf, ((0, 0), (0, 0), (0, LANES - N_HEADS))), lf.reshape(nb, t, LANES)], axis=1)
        n_pad = -(-(pl_ + t) // LANES) * LANES
        lf_all = jnp.pad(lf_all, ((0, 0), (0, n_pad - pl_ - t), (0, 0)))
        f = cumsum_rows(lf_all)[:, :N_HEADS, None, :]
        kc_old, vc_old = kexp(c_ckv.reshape(nb * pl_, MLA_KV_RANK),
                              _dup_rope(c_kr).reshape(nb * pl_, LANES), p, *tabs["kc"])
        flat = lambda a: a.reshape(nb, pl_, -1)
        oa = attn_decode("fox", qa, kab, vab, flat(c_fk), flat(c_fv), f[..., pl_:pl_ + t], f[..., :pl_])
        ob = attn_decode("sb", qb, kbb, vbb, flat(c_sk), flat(c_sv))
        oc = attn_decode("mla", qc, kc_new, vc_new, flat(kc_old), flat(vc_old))
    x2 = out_proj(oa, ob, oc, p, x2)
    x2 = ffn(x2, p["norm_ffn"], p["w_gu"], p["w_down"])
    heads = lambda a: a.reshape(nb, t, N_HEADS, HEAD_DIM)
    rows = (heads(ka), heads(va), lf[:, :N_HEADS].reshape(nb, t, N_HEADS), heads(kb), heads(vb),
            ckv.reshape(nb, t, MLA_KV_RANK), krd[:, :MLA_ROPE_DIM].reshape(nb, t, MLA_ROPE_DIM))
    return x2.reshape(nb, t, D_MODEL), rows


def kernel(x_prompt, x_sample, cache_fox_k, cache_fox_v, cache_fox_logf, cache_sb_k, cache_sb_v, cache_mla_ckv, cache_mla_krope, norm_mix, w_in, fox_f_bias, fox_q_norm, fox_k_norm, mla_qa_norm, mla_w_uq, mla_kva_norm, mla_w_ukv, mla_q_norm, mla_k_norm, out_norm_a, out_norm_b, out_norm_c, w_out, norm_ffn, w_gu, w_down):
    depth = w_in.shape[0]
    t_p = x_prompt.shape[1]
    nb_s, t_s = x_sample.shape[0], x_sample.shape[1]
    past_len = cache_fox_k.shape[2]
    tabs_p = {"q": _rope_tables(jnp.arange(t_p, dtype=jnp.int32))}
    tabs_s = {"q": _rope_tables(jnp.tile(past_len + jnp.arange(t_s, dtype=jnp.int32), nb_s)),
              "kc": _rope_tables(jnp.tile(jnp.arange(past_len, dtype=jnp.int32), nb_s))}
    y_p, y_s = x_prompt, x_sample
    rows_p, rows_s = [], []
    for l in range(depth):
        p = _prep_layer(l, fox_f_bias, fox_q_norm, fox_k_norm, mla_qa_norm, mla_w_uq, mla_kva_norm, mla_w_ukv,
                        mla_q_norm, mla_k_norm, out_norm_a, out_norm_b, out_norm_c, w_in, w_out, w_gu, w_down,
                        norm_mix, norm_ffn)
        y_p, r_p = _layer(y_p, None, p, tabs_p)
        past = (cache_fox_k[l], cache_fox_v[l], cache_fox_logf[l], cache_sb_k[l], cache_sb_v[l],
                cache_mla_ckv[l], cache_mla_krope[l])
        y_s, r_s = _layer(y_s, past, p, tabs_s)
        rows_p.append(r_p)
        rows_s.append(r_s)
    outs_p = tuple(jnp.stack([r[i] for r in rows_p], axis=0) for i in range(7))
    outs_s = tuple(jnp.stack([r[i] for r in rows_s], axis=0) for i in range(7))
    return (y_p, y_s) + outs_p + outs_s
```

```python
import functools
import math

import jax
import jax.numpy as jnp
from jax import lax
from jax.experimental import pallas as pl
from jax.experimental.pallas import tpu as pltpu

D_MODEL = 2048
CHUNK = 64
HEAD_DIM = 128
N_HEADS = 4
W_GROUP = N_HEADS * HEAD_DIM
MLA_Q_RANK = 512
MLA_KV_RANK = 256
MLA_NOPE_DIM = 128
MLA_ROPE_DIM = 64
MLA_QK_DIM = MLA_NOPE_DIM + MLA_ROPE_DIM
MLA_HEAD_PAD = 256
ROPE_THETA = 10000.0
D_FF = 5632
RMS_EPS = 1e-6
NEG_INF = -1e30
LOG2E = math.log2(math.e)
N_IN_PAD = 4096
LANES = 128
TRI = 256
ATTN_TILE = 512
ROW_GROUP = 128
VMEM_LIMIT = 56 * 1024 * 1024

C_QA, C_KA, C_VA, C_QB, C_KB, C_VB, C_CQ, C_CKV, C_KR, C_FA = (
    0, 512, 1024, 1536, 2048, 2560, 3072, 3584, 3840, 3968)

BF16 = jnp.bfloat16
F32 = jnp.float32


def _params(*sem):
    return pltpu.CompilerParams(dimension_semantics=sem, vmem_limit_bytes=VMEM_LIMIT)


def _pick_tile(n, pref):
    t = min(n, pref)
    while n % t:
        t //= 2
    return t


def _rms(x, g):
    return x * lax.rsqrt(jnp.mean(x * x, axis=-1, keepdims=True) + RMS_EPS) * g


def _dot(a, b):
    return jnp.dot(a, b, preferred_element_type=F32)


def _dot_nt(a, b):
    return lax.dot_general(a, b, (((1,), (1,)), ((), ())), preferred_element_type=F32)


def _split2(x):
    hi = x.astype(BF16)
    lo = (x - hi.astype(F32)).astype(BF16)
    return hi, lo


def _neg_abs(x):
    bits = lax.bitcast_convert_type(x, jnp.uint32) | jnp.uint32(0x80000000)
    return lax.bitcast_convert_type(bits, F32)


def _split3(x):
    a = x.astype(BF16)
    r = x - a.astype(F32)
    b = r.astype(BF16)
    c = (r - b.astype(F32)).astype(BF16)
    return a, b, c


def _log_sigmoid(x):
    return -(jnp.maximum(-x, 0.0) + jnp.log1p(jnp.exp(-jnp.abs(x))))


def _log2_keep(z2):
    return -(jnp.maximum(z2, 0.0) + jnp.log2(1.0 + jnp.exp2(_neg_abs(z2))))


def _rms_matmul_kernel(x_ref, g_ref, w_ref, o_ref, h_ref):
    @pl.when(pl.program_id(1) == 0)
    def _():
        h_ref[...] = _rms(x_ref[...], g_ref[...]).astype(BF16)

    o_ref[...] = _dot(h_ref[...], w_ref[...])


def rms_matmul(x, g, w):
    m, k = x.shape
    n = w.shape[1]
    tm = _pick_tile(m, 1024)
    tn = _pick_tile(n, 1024)
    return pl.pallas_call(
        _rms_matmul_kernel,
        grid=(m // tm, n // tn),
        in_specs=[pl.BlockSpec((tm, k), lambda i, j: (i, 0)),
                  pl.BlockSpec((1, k), lambda i, j: (0, 0)),
                  pl.BlockSpec((k, tn), lambda i, j: (0, j))],
        out_specs=pl.BlockSpec((tm, tn), lambda i, j: (i, j)),
        out_shape=jax.ShapeDtypeStruct((m, n), F32),
        scratch_shapes=[pltpu.VMEM((tm, k), BF16)],
        compiler_params=_params("parallel", "arbitrary"),
        name="rms_matmul",
    )(x, g, w)


def _rope_chunk(y2, cos, sin):
    return y2 * cos + pltpu.roll(y2, 64, 1) * sin


def _rope_lane_mask():
    return lax.broadcasted_iota(jnp.int32, (1, LANES), 1) < MLA_ROPE_DIM


def _post_kernel(with_kt, proj_ref, fqg_ref, fkg_ref, fb_ref, qag_ref, wuq_ref, qg_ref, kvg_ref, cos_ref, sin_ref,
                 qa_o, ka_o, va_o, vab_o, lf_o, qb_o, kb_o, vb_o, vbb_o, qc_o, ckv_o, kr_o, *kt_o):
    scale = HEAD_DIM ** -0.5 * LOG2E
    for h in range(N_HEADS):
        sl = slice(h * HEAD_DIM, (h + 1) * HEAD_DIM)
        qa = proj_ref[:, C_QA + h * HEAD_DIM:C_QA + (h + 1) * HEAD_DIM]
        qa_o[:, sl] = (_rms(qa, fqg_ref[...]) * scale).astype(BF16)
        ka = _rms(proj_ref[:, C_KA + h * HEAD_DIM:C_KA + (h + 1) * HEAD_DIM], fkg_ref[...])
        ka_o[:, sl] = ka
        kb = proj_ref[:, C_KB + h * HEAD_DIM:C_KB + (h + 1) * HEAD_DIM]
        kb_o[:, sl] = kb
        if with_kt:
            kt_o[0][h, 0] = ka.T.astype(BF16)
            kt_o[1][h, 0] = kb.T.astype(BF16)
    va = proj_ref[:, C_VA:C_VA + W_GROUP]
    va_o[...] = va
    vab_o[...] = va.astype(BF16)
    qb_o[...] = (proj_ref[:, C_QB:C_QB + W_GROUP] * scale).astype(BF16)
    vb = proj_ref[:, C_VB:C_VB + W_GROUP]
    vb_o[...] = vb
    vbb_o[...] = vb.astype(BF16)

    lane = lax.broadcasted_iota(jnp.int32, (1, LANES), 1)
    lf = _log_sigmoid(proj_ref[:, C_FA:C_FA + LANES] + fb_ref[...])
    lf_o[...] = jnp.where(lane < N_HEADS, lf, 0.0)

    cqn = _rms(proj_ref[:, C_CQ:C_CQ + MLA_Q_RANK], qag_ref[...]).astype(BF16)
    qc = _dot(cqn, wuq_ref[...])
    rmask = _rope_lane_mask()
    cos = cos_ref[...]
    sin = sin_ref[...]
    for h in range(N_HEADS):
        c1 = qc[:, h * MLA_HEAD_PAD:h * MLA_HEAD_PAD + LANES]
        c2 = qc[:, h * MLA_HEAD_PAD + LANES:(h + 1) * MLA_HEAD_PAD]
        ss = (jnp.sum(c1 * c1, axis=-1, keepdims=True)
              + jnp.sum(jnp.where(rmask, c2 * c2, 0.0), axis=-1, keepdims=True))
        rs = lax.rsqrt(ss * (1.0 / MLA_QK_DIM) + RMS_EPS)
        y1 = c1 * rs * qg_ref[:, :LANES]
        y2 = c2 * rs * qg_ref[:, LANES:]
        qc_o[:, h * MLA_HEAD_PAD:h * MLA_HEAD_PAD + LANES] = y1.astype(BF16)
        qc_o[:, h * MLA_HEAD_PAD + LANES:(h + 1) * MLA_HEAD_PAD] = _rope_chunk(y2, cos, sin).astype(BF16)

    ckv_o[...] = _rms(proj_ref[:, C_CKV:C_CKV + MLA_KV_RANK], kvg_ref[...])
    kr_o[...] = proj_ref[:, C_KR:C_KR + LANES]


def post(proj, p, cos, sin, with_kt):
    m = proj.shape[0]
    tm = _pick_tile(m, ATTN_TILE)
    row = lambda w: pl.BlockSpec((tm, w), lambda i: (i, 0))
    full = lambda a: pl.BlockSpec(a.shape, lambda i: (0,) * a.ndim)
    consts = (p["fox_q_norm"], p["fox_k_norm"], p["fox_f_bias"], p["mla_qa_norm"], p["mla_w_uq"],
              p["mla_q_gain"], p["mla_kva_norm"])
    outs = [(W_GROUP, BF16), (W_GROUP, F32), (W_GROUP, F32), (W_GROUP, BF16), (LANES, F32),
            (W_GROUP, BF16), (W_GROUP, F32), (W_GROUP, F32), (W_GROUP, BF16),
            (N_HEADS * MLA_HEAD_PAD, BF16), (MLA_KV_RANK, F32), (LANES, F32)]
    out_specs = [row(w) for w, _ in outs]
    out_shape = [jax.ShapeDtypeStruct((m, w), dt) for w, dt in outs]
    if with_kt:
        for _ in range(2):
            out_specs.append(pl.BlockSpec((N_HEADS, 1, HEAD_DIM, tm), lambda i: (0, i, 0, 0)))
            out_shape.append(jax.ShapeDtypeStruct((N_HEADS, m // tm, HEAD_DIM, tm), BF16))
    return pl.pallas_call(
        functools.partial(_post_kernel, with_kt),
        grid=(m // tm,),
        in_specs=[row(N_IN_PAD)] + [full(a) for a in consts] + [row(LANES), row(LANES)],
        out_specs=out_specs,
        out_shape=out_shape,
        compiler_params=_params("parallel"),
        name="post",
    )(proj, *consts, cos, sin)


def _kexp_kernel(transposed, ckv_ref, kr_ref, w_ref, kg_ref, cos_ref, sin_ref, kc_o, vc_o):
    kv = _dot(ckv_ref[...].astype(BF16), w_ref[...])
    kr = kr_ref[...]
    krsq = jnp.sum(jnp.where(_rope_lane_mask(), kr * kr, 0.0), axis=-1, keepdims=True)
    cos = cos_ref[...]
    sin = sin_ref[...]
    for h in range(N_HEADS):
        n = kv[:, h * LANES:(h + 1) * LANES]
        ss = jnp.sum(n * n, axis=-1, keepdims=True) + krsq
        rs = lax.rsqrt(ss * (1.0 / MLA_QK_DIM) + RMS_EPS)
        k1 = n * rs * kg_ref[:, :LANES]
        k2 = _rope_chunk(kr * rs * kg_ref[:, LANES:], cos, sin)
        if transposed:
            kc_o[h, 0, :LANES, :] = k1.T.astype(BF16)
            kc_o[h, 0, LANES:, :] = k2.T.astype(BF16)
        else:
            kc_o[:, h * MLA_HEAD_PAD:h * MLA_HEAD_PAD + LANES] = k1.astype(BF16)
            kc_o[:, h * MLA_HEAD_PAD + LANES:(h + 1) * MLA_HEAD_PAD] = k2.astype(BF16)
    vc_o[...] = kv[:, W_GROUP:].astype(BF16)


def kexp(ckv, krdup, p, cos, sin, transposed, layer=None):
    m = krdup.shape[0]
    tm = _pick_tile(m, ATTN_TILE)
    row = lambda w: pl.BlockSpec((tm, w), lambda i: (i, 0))
    full = lambda a: pl.BlockSpec(a.shape, lambda i: (0,) * a.ndim)
    ckv_spec = row(MLA_KV_RANK) if layer is None else pl.BlockSpec(
        (None, tm, MLA_KV_RANK), lambda i: (layer, i, 0))
    if transposed:
        kc_spec = pl.BlockSpec((N_HEADS, 1, MLA_HEAD_PAD, tm), lambda i: (0, i, 0, 0))
        kc_shape = jax.ShapeDtypeStruct((N_HEADS, m // tm, MLA_HEAD_PAD, tm), BF16)
    else:
        kc_spec = row(N_HEADS * MLA_HEAD_PAD)
        kc_shape = jax.ShapeDtypeStruct((m, N_HEADS * MLA_HEAD_PAD), BF16)
    return pl.pallas_call(
        functools.partial(_kexp_kernel, transposed),
        grid=(m // tm,),
        in_specs=[ckv_spec, row(LANES), full(p["mla_w_ukv"]), full(p["mla_k_gain"]), row(LANES), row(LANES)],
        out_specs=[kc_spec, row(W_GROUP)],
        out_shape=[kc_shape, jax.ShapeDtypeStruct((m, W_GROUP), BF16)],
        compiler_params=_params("parallel"),
        name="kexp",
    )(ckv, krdup, p["mla_w_ukv"], p["mla_k_gain"], cos, sin)


def _cumsum_kernel(x_ref, o_ref, carry_ref):
    @pl.when(pl.program_id(1) == 0)
    def _():
        carry_ref[...] = jnp.zeros_like(carry_ref)

    t = x_ref.shape[1]
    xt = x_ref[0].T[:8, :]
    ri = lax.broadcasted_iota(jnp.int32, (t, t), 0)
    ci = lax.broadcasted_iota(jnp.int32, (t, t), 1)
    upper = jnp.where(ri <= ci, 1.0, 0.0).astype(BF16)
    a, b, c = _split3(xt)
    f = _dot(a, upper) + _dot(b, upper) + _dot(c, upper) + carry_ref[:, :1]
    o_ref[0] = f
    carry_ref[...] = jnp.broadcast_to(f[:, t - 1:t], carry_ref.shape)


def cumsum_rows(x):
    b, n, _ = x.shape
    t = n if n <= 1280 else _pick_tile(n, 512)
    return pl.pallas_call(
        _cumsum_kernel,
        grid=(b, n // t),
        in_specs=[pl.BlockSpec((1, t, LANES), lambda i, j: (i, j, 0))],
        out_specs=pl.BlockSpec((1, 8, t), lambda i, j: (i, 0, j)),
        out_shape=jax.ShapeDtypeStruct((b, 8, n), F32),
        scratch_shapes=[pltpu.VMEM((8, LANES), F32)],
        compiler_params=_params("parallel", "arbitrary"),
        name="cumsum_rows",
    )(x)


def _tile_mask(kind, qpos, kpos):
    if kind == "fox":
        return kpos <= qpos
    if kind == "sb":
        return kpos < qpos
    shift = CHUNK.bit_length() - 1
    return lax.shift_right_logical(kpos, shift) <= lax.shift_right_logical(qpos, shift)


def _tri_lower(n):
    ri = lax.broadcasted_iota(jnp.int32, (n, n), 0)
    ci = lax.broadcasted_iota(jnp.int32, (n, n), 1)
    return jnp.where(ri >= ci, 1.0, 0.0).astype(BF16)


def _sb_weights(z, carry, mask, tri):
    lk = _log2_keep(z)
    if mask is not None:
        lk = jnp.where(mask, lk, 0.0)
    blk = tri.shape[0]
    nblk = z.shape[1] // blk
    parts = [None] * nblk
    for c in reversed(range(nblk)):
        lkc = lk[:, c * blk:(c + 1) * blk]
        hi, lo = _split2(lkc)
        intra = _dot(hi, tri) + _dot(lo, tri)
        if carry.shape[1] == 1 or carry.shape[1] == blk:
            parts[c] = intra + carry
        else:
            parts[c] = intra + jnp.concatenate([carry] * (blk // carry.shape[1]), axis=1)
        carry = carry + jnp.sum(lkc, axis=-1, keepdims=True)
    r = parts[0] if nblk == 1 else jnp.concatenate(parts, axis=-1)
    a = jnp.exp2(z + r)
    if mask is not None:
        a = jnp.where(mask, a, 0.0)
    return a, carry


def _attn_prompt_kernel(kind, tq, *refs):
    f_ref = al_ref = None
    if kind == "fox":
        q_ref, kt_ref, v_ref, f_ref, o_ref, m_ref, acc_ref, s_ref, p_ref, al_ref = refs
    elif kind == "mla":
        q_ref, kt_ref, v_ref, o_ref, m_ref, acc_ref, s_ref, p_ref, al_ref = refs
    else:
        q_ref, kt_ref, v_ref, o_ref, m_ref, acc_ref, s_ref, p_ref = refs
    qb = pl.program_id(1)
    rg = min(TRI if kind == "sb" else ROW_GROUP, tq)
    n_rg = tq // rg
    q = q_ref[...]
    ones = jnp.ones((tq, LANES), BF16)
    tri2 = jnp.concatenate([_tri_lower(rg)] * 2, axis=0) if kind == "sb" else None
    fref = f_ref[0, qb][:, :1] if kind == "fox" else None

    def key_block(j):
        return jnp.clip(qb - j, 0, qb)

    def stage_a(j, slot):
        s_ref[slot] = _dot(q, kt_ref[0, key_block(j)])

    def stage_b(j, slot, diagonal):
        brow = (fref - f_ref[0, key_block(j)]) * LOG2E if kind == "fox" else None
        ms, als, prs = [], [], []
        for r in range(n_rg):
            rows = slice(r * rg, (r + 1) * rg)
            kw = (r + 1) * rg if diagonal else tq
            s = s_ref[slot, rows, :kw]
            mask = None
            if diagonal:
                qpos = r * rg + lax.broadcasted_iota(jnp.int32, (rg, kw), 0)
                kpos = lax.broadcasted_iota(jnp.int32, (rg, kw), 1)
                mask = _tile_mask(kind, qpos, kpos)
            m_prev = m_ref[rows, :]
            if kind == "sb":
                plk = jnp.maximum(s, 0.0) + jnp.log2(1.0 + jnp.exp2(_neg_abs(s)))
                if mask is not None:
                    plk = jnp.where(mask, plk, 0.0)
                carry = m_prev
                parts = [None] * (kw // rg)
                for c in reversed(range(kw // rg)):
                    pc = plk[:, c * rg:(c + 1) * rg]
                    hi, lo = _split2(pc)
                    later = _dot(jnp.concatenate([hi, lo], axis=1), tri2)
                    parts[c] = s[:, c * rg:(c + 1) * rg] - later - jnp.concatenate([carry] * (rg // LANES), 1)
                    carry = carry + jnp.sum(pc, axis=-1, keepdims=True)
                a = jnp.exp2(parts[0] if len(parts) == 1 else jnp.concatenate(parts, axis=1))
                if mask is not None:
                    a = jnp.where(mask, a, 0.0)
                pr = a.astype(BF16)
                ms.append(carry)
            else:
                if kind == "fox":
                    s = s + brow[:, :kw]
                if mask is not None:
                    s = jnp.where(mask, s, NEG_INF)
                chunks = [s[:, c * LANES:(c + 1) * LANES] for c in range(kw // LANES)]
                m_cur = jnp.max(functools.reduce(jnp.maximum, chunks), axis=-1, keepdims=True)
                m_new = jnp.maximum(m_prev, m_cur)
                als.append(jnp.exp2(m_prev - m_new))
                pr = jnp.concatenate([jnp.exp2(c - m_new).astype(BF16) for c in chunks], axis=1)
                ms.append(m_new)
            if kw < tq:
                pr = jnp.concatenate([pr, jnp.zeros((rg, tq - kw), BF16)], axis=1)
            prs.append(pr)
        m_ref[...] = jnp.concatenate(ms, axis=0)
        p_ref[slot] = jnp.concatenate(prs, axis=0)
        if kind != "sb":
            al_ref[slot] = jnp.concatenate(als, axis=0)

    def stage_c(j, slot):
        start = pl.multiple_of(key_block(j) * tq, tq)
        v = v_ref[pl.ds(start, tq), :]
        if kind == "sb":
            acc_ref[...] += _dot(p_ref[slot], v)
        else:
            al = al_ref[slot]
            pv = _dot(p_ref[slot], jnp.concatenate([v, ones], axis=1))
            acc_ref[...] = jnp.concatenate([al, al], axis=1) * acc_ref[...] + pv

    def step(j, slot):
        stage_b(j, slot, False)
        stage_a(j + 1, 1 - slot)
        stage_c(j - 1, 1 - slot)

    m_ref[...] = jnp.full(m_ref.shape, 0.0 if kind == "sb" else NEG_INF, F32)
    acc_ref[...] = jnp.zeros_like(acc_ref)
    stage_a(0, 0)
    stage_b(0, 0, True)
    stage_a(1, 1)

    def body(t, c):
        step(1 + 2 * t, 1)
        step(2 + 2 * t, 0)
        return c

    lax.fori_loop(0, qb // 2, body, 0)

    @pl.when(qb % 2 == 1)
    def _():
        step(qb, 1)
        stage_c(qb, 1)

    @pl.when(qb % 2 == 0)
    def _():
        stage_c(qb, 0)

    if kind == "sb":
        o_ref[...] = acc_ref[...]
    else:
        o_ref[...] = acc_ref[:, :HEAD_DIM] / acc_ref[:, HEAD_DIM:]


def attn_prompt(kind, q, kt, v, f=None):
    t = q.shape[0]
    dq = q.shape[1] // N_HEADS
    tq = kt.shape[3]
    nq = t // tq
    in_specs = [pl.BlockSpec((tq, dq), lambda h, i: (i, h)),
                pl.BlockSpec((1, nq, dq, tq), lambda h, i: (h, 0, 0, 0)),
                pl.BlockSpec((t, HEAD_DIM), lambda h, i: (0, h))]
    args = [q, kt, v]
    if kind == "fox":
        in_specs.append(pl.BlockSpec((1, nq, 1, tq), lambda h, i: (h, 0, 0, 0)))
        args.append(f.reshape(N_HEADS, nq, 1, tq))
    acc_w = HEAD_DIM if kind == "sb" else 2 * HEAD_DIM
    scratch = [pltpu.VMEM((tq, LANES), F32), pltpu.VMEM((tq, acc_w), F32),
               pltpu.VMEM((2, tq, tq), F32), pltpu.VMEM((2, tq, tq), BF16)]
    if kind != "sb":
        scratch.append(pltpu.VMEM((2, tq, LANES), F32))
    return pl.pallas_call(
        functools.partial(_attn_prompt_kernel, kind, tq),
        grid=(N_HEADS, nq),
        in_specs=in_specs,
        out_specs=pl.BlockSpec((tq, HEAD_DIM), lambda h, i: (i, h)),
        out_shape=jax.ShapeDtypeStruct((t, W_GROUP), F32),
        scratch_shapes=scratch,
        compiler_params=_params("parallel", "arbitrary"),
        name="attn_prompt_" + kind,
    )(*args)


def _softmax_tile(s, v, m_prev, l_prev, acc_prev):
    m_new = jnp.maximum(m_prev, jnp.max(s, axis=-1, keepdims=True))
    alpha = jnp.exp2(m_prev - m_new)
    pr = jnp.exp2(s - m_new)
    l_new = alpha * l_prev + jnp.sum(pr, axis=-1, keepdims=True)
    acc_new = alpha * acc_prev + _dot(pr.astype(BF16), v)
    return m_new, l_new, acc_new


def _attn_decode_kernel(kind, *refs):
    if kind == "fox":
        q_ref, kn_ref, vn_ref, kc_ref, vc_ref, fn_ref, fc_ref, o_ref = refs
    else:
        q_ref, kn_ref, vn_ref, kc_ref, vc_ref, o_ref = refs
    q = q_ref[...]
    tq = q.shape[0]
    past = kc_ref.shape[0]
    kn = kn_ref[...].astype(BF16)
    vn = vn_ref[...]
    kc = kc_ref[...].astype(BF16)
    vc = vc_ref[...].astype(BF16)
    qpos = past + lax.broadcasted_iota(jnp.int32, (tq, tq), 0)
    kpos = past + lax.broadcasted_iota(jnp.int32, (tq, tq), 1)
    mask = _tile_mask(kind, qpos, kpos)
    s_n = _dot_nt(q, kn)
    s_c = _dot_nt(q, kc)
    if kind == "sb":
        a_n, carry = _sb_weights(s_n, jnp.zeros((tq, 1), F32), mask, _tri_lower(tq))
        a_c, _ = _sb_weights(s_c, carry, None, _tri_lower(min(TRI, past)))
        o_ref[...] = _dot(a_n.astype(BF16), vn) + _dot(a_c.astype(BF16), vc)
        return
    if kind == "fox":
        fref = fn_ref[:, :1]
        s_n = s_n + (fref - fn_ref[...]) * LOG2E
        s_c = s_c + (fref - fc_ref[...]) * LOG2E
    s_n = jnp.where(mask, s_n, NEG_INF)
    m0 = jnp.full((tq, 1), NEG_INF, F32)
    z0 = jnp.zeros((tq, 1), F32)
    m, l, acc = _softmax_tile(s_n, vn, m0, z0, jnp.zeros((tq, HEAD_DIM), F32))
    m, l, acc = _softmax_tile(s_c, vc, m, l, acc)
    o_ref[...] = acc / l


def attn_decode(kind, q, kn, vn, kc, vc, layer=None, fn=None, fc=None):
    nb, past = kc.shape[-3], kc.shape[-2]
    tq = q.shape[0] // nb
    dq = q.shape[1] // N_HEADS
    new = lambda d: pl.BlockSpec((tq, d), lambda b, h: (b, h))
    if layer is None:
        old = lambda d: pl.BlockSpec((None, past, d), lambda b, h: (b, 0, h))
    else:
        old = lambda d: pl.BlockSpec((None, None, past, d), lambda b, h: (layer, b, 0, h))
    in_specs = [new(dq), new(dq), new(HEAD_DIM), old(dq), old(HEAD_DIM)]
    args = [q, kn, vn, kc, vc]
    if kind == "fox":
        in_specs += [pl.BlockSpec((None, None, 1, tq), lambda b, h: (b, h, 0, 0)),
                     pl.BlockSpec((None, None, 1, past), lambda b, h: (b, h, 0, 0))]
        args += [fn, fc]
    return pl.pallas_call(
        functools.partial(_attn_decode_kernel, kind),
        grid=(nb, N_HEADS),
        in_specs=in_specs,
        out_specs=new(HEAD_DIM),
        out_shape=jax.ShapeDtypeStruct((nb * tq, W_GROUP), F32),
        compiler_params=_params("parallel", "parallel"),
        name="attn_decode_" + kind,
    )(*args)


def _out_proj_kernel(oa_ref, ob_ref, oc_ref, ga_ref, gb_ref, gc_ref, w_ref, x_ref, o_ref):
    acc = x_ref[...]
    for g, (o, gn) in enumerate(((oa_ref, ga_ref), (ob_ref, gb_ref), (oc_ref, gc_ref))):
        y = _rms(o[...], gn[...]).astype(BF16)
        acc = acc + _dot(y, w_ref[g * W_GROUP:(g + 1) * W_GROUP, :])
    o_ref[...] = acc


def out_proj(oa, ob, oc, p, x):
    m = x.shape[0]
    tm = _pick_tile(m, 512)
    row = lambda w: pl.BlockSpec((tm, w), lambda i: (i, 0))
    full = lambda a: pl.BlockSpec(a.shape, lambda i: (0,) * a.ndim)
    consts = (p["out_norm_a"], p["out_norm_b"], p["out_norm_c"], p["w_out"])
    return pl.pallas_call(
        _out_proj_kernel,
        grid=(m // tm,),
        in_specs=[row(W_GROUP)] * 3 + [full(a) for a in consts] + [row(D_MODEL)],
        out_specs=row(D_MODEL),
        out_shape=jax.ShapeDtypeStruct((m, D_MODEL), F32),
        compiler_params=_params("parallel"),
        name="out_proj",
    )(oa, ob, oc, *consts, x)


def _ffn_kernel(x_ref, g_ref, wg_ref, wu_ref, wd_ref, o_ref, h_ref):
    @pl.when(pl.program_id(1) == 0)
    def _():
        x = x_ref[...]
        h_ref[...] = _rms(x, g_ref[...]).astype(BF16)
        o_ref[...] = x

    h = h_ref[...]
    gate = _dot(h, wg_ref[...])
    up = _dot(h, wu_ref[...])
    act = (gate * jax.nn.sigmoid(gate) * up).astype(BF16)
    o_ref[...] += _dot(act, wd_ref[...])


def ffn(x, g, w_gu, w_down):
    m = x.shape[0]
    tm = _pick_tile(m, 512)
    tf = 512
    nf = D_FF // tf
    return pl.pallas_call(
        _ffn_kernel,
        grid=(m // tm, nf),
        in_specs=[pl.BlockSpec((tm, D_MODEL), lambda i, j: (i, 0)),
                  pl.BlockSpec((1, D_MODEL), lambda i, j: (0, 0)),
                  pl.BlockSpec((D_MODEL, tf), lambda i, j: (0, j)),
                  pl.BlockSpec((D_MODEL, tf), lambda i, j: (0, j + nf)),
                  pl.BlockSpec((tf, D_MODEL), lambda i, j: (j, 0))],
        out_specs=pl.BlockSpec((tm, D_MODEL), lambda i, j: (i, 0)),
        out_shape=jax.ShapeDtypeStruct((m, D_MODEL), F32),
        scratch_shapes=[pltpu.VMEM((tm, D_MODEL), BF16)],
        compiler_params=_params("parallel", "arbitrary"),
        name="ffn",
    )(x, g, w_gu, w_gu, w_down)


def _dup_rope(r):
    half = MLA_ROPE_DIM // 2
    return jnp.concatenate([r, r[..., half:], r[..., :half]], axis=-1)


def _prep_layer(l, fox_f_bias, fox_q_norm, fox_k_norm, mla_qa_norm, mla_w_uq, mla_kva_norm, mla_w_ukv,
                mla_q_norm, mla_k_norm, out_norm_a, out_norm_b, out_norm_c, w_in, w_out, w_gu, w_down,
                norm_mix, norm_ffn):
    w = w_in[l]
    off = [0]
    for s in (512, 512, 512, 4, 512, 512, 512, 512, 256, 64):
        off.append(off[-1] + s)
    qa, ka, va, fa, qb, kb, vb, cq, ckv, kr = (w[:, off[i]:off[i + 1]] for i in range(10))
    w_in_p = jnp.concatenate(
        [qa, ka, va, qb, kb, vb, cq, ckv, _dup_rope(kr), fa, jnp.zeros((D_MODEL, LANES - N_HEADS), F32)],
        axis=1).astype(BF16)

    uq = mla_w_uq[l].reshape(MLA_Q_RANK, N_HEADS, MLA_QK_DIM)
    uq = jnp.concatenate([uq[..., :MLA_NOPE_DIM], _dup_rope(uq[..., MLA_NOPE_DIM:])], axis=-1)
    ukv = mla_w_ukv[l].reshape(MLA_KV_RANK, N_HEADS, 2 * LANES)
    ukv = jnp.concatenate([ukv[..., :LANES].reshape(MLA_KV_RANK, W_GROUP),
                           ukv[..., LANES:].reshape(MLA_KV_RANK, W_GROUP)], axis=1)

    def gain256(g):
        return jnp.concatenate([g[:MLA_NOPE_DIM], _dup_rope(g[MLA_NOPE_DIM:])])[None, :]

    row = lambda a: a[l][None, :]
    return {
        "norm_mix": row(norm_mix), "w_in": w_in_p,
        "fox_q_norm": row(fox_q_norm), "fox_k_norm": row(fox_k_norm),
        "fox_f_bias": jnp.pad(fox_f_bias[l], (0, LANES - N_HEADS))[None, :],
        "mla_qa_norm": row(mla_qa_norm),
        "mla_w_uq": uq.reshape(MLA_Q_RANK, N_HEADS * MLA_HEAD_PAD).astype(BF16),
        "mla_q_gain": gain256(mla_q_norm[l]) * (MLA_QK_DIM ** -0.5 * LOG2E),
        "mla_kva_norm": row(mla_kva_norm),
        "mla_w_ukv": ukv.astype(BF16),
        "mla_k_gain": gain256(mla_k_norm[l]),
        "out_norm_a": row(out_norm_a), "out_norm_b": row(out_norm_b), "out_norm_c": row(out_norm_c),
        "w_out": w_out[l].astype(BF16),
        "norm_ffn": row(norm_ffn), "w_gu": w_gu[l].astype(BF16), "w_down": w_down[l].astype(BF16),
    }


def _rope_tables(pos):
    half = MLA_ROPE_DIM // 2
    inv_freq = ROPE_THETA ** (-(jnp.arange(half, dtype=F32) / half))
    ang = pos.astype(F32)[:, None] * inv_freq[None, :]
    cos, sin = jnp.cos(ang), jnp.sin(ang)
    zero = jnp.zeros_like(cos)
    return (jnp.concatenate([cos, cos, zero, zero], axis=1),
            jnp.concatenate([-sin, sin, zero, zero], axis=1))


def _layer(x, caches, l, p, tabs):
    nb, t, _ = x.shape
    m = nb * t
    x2 = x.reshape(m, D_MODEL)
    proj = rms_matmul(x2, p["norm_mix"], p["w_in"])
    prompt = caches is None
    outs = post(proj, p, *tabs["q"], with_kt=prompt)
    qa, ka, va, vab, lf, qb, kb, vb, vbb, qc, ckv, krd = outs[:12]
    if prompt:
        kat, kbt = outs[12:]
        kct, vc_new = kexp(ckv, krd, p, *tabs["q"], transposed=True)
        f = cumsum_rows(lf.reshape(nb, t, LANES))
        oa = attn_prompt("fox", qa, kat, vab, f[0, :N_HEADS])
        ob = attn_prompt("sb", qb, kbt, vbb)
        oc = attn_prompt("mla", qc, kct, vc_new)
    else:
        c_fk, c_fv, c_lf, c_sk, c_sv, c_ckv, c_kr = caches
        depth, _, pl_ = c_fk.shape[:3]
        kc_new, vc_new = kexp(ckv, krd, p, *tabs["q"], transposed=False)
        lf_all = jnp.concatenate(
            [jnp.pad(c_lf[l], ((0, 0), (0, 0), (0, LANES - N_HEADS))), lf.reshape(nb, t, LANES)], axis=1)
        n_pad = -(-(pl_ + t) // LANES) * LANES
        lf_all = jnp.pad(lf_all, ((0, 0), (0, n_pad - pl_ - t), (0, 0)))
        f = cumsum_rows(lf_all)[:, :N_HEADS, None, :]
        kc_old, vc_old = kexp(c_ckv.reshape(depth, nb * pl_, MLA_KV_RANK),
                              _dup_rope(c_kr[l]).reshape(nb * pl_, LANES), p, *tabs["kc"],
                              transposed=False, layer=l)
        flat = lambda a: a.reshape(depth, nb, pl_, W_GROUP)
        oa = attn_decode("fox", qa, ka, vab, flat(c_fk), flat(c_fv), layer=l,
                         fn=f[..., pl_:pl_ + t], fc=f[..., :pl_])
        ob = attn_decode("sb", qb, kb, vbb, flat(c_sk), flat(c_sv), layer=l)
        oc = attn_decode("mla", qc, kc_new, vc_new, kc_old.reshape(nb, pl_, -1), vc_old.reshape(nb, pl_, -1))
    x2 = out_proj(oa, ob, oc, p, x2)
    x2 = ffn(x2, p["norm_ffn"], p["w_gu"], p["w_down"])
    heads = lambda a: a.reshape(nb, t, N_HEADS, HEAD_DIM)
    rows = (heads(ka), heads(va), lf[:, :N_HEADS].reshape(nb, t, N_HEADS), heads(kb), heads(vb),
            ckv.reshape(nb, t, MLA_KV_RANK), krd[:, :MLA_ROPE_DIM].reshape(nb, t, MLA_ROPE_DIM))
    return x2.reshape(nb, t, D_MODEL), rows


def kernel(x_prompt, x_sample, cache_fox_k, cache_fox_v, cache_fox_logf, cache_sb_k, cache_sb_v, cache_mla_ckv, cache_mla_krope, norm_mix, w_in, fox_f_bias, fox_q_norm, fox_k_norm, mla_qa_norm, mla_w_uq, mla_kva_norm, mla_w_ukv, mla_q_norm, mla_k_norm, out_norm_a, out_norm_b, out_norm_c, w_out, norm_ffn, w_gu, w_down):
    depth = w_in.shape[0]
    t_p = x_prompt.shape[1]
    nb_s, t_s = x_sample.shape[0], x_sample.shape[1]
    past_len = cache_fox_k.shape[2]
    tabs_p = {"q": _rope_tables(jnp.arange(t_p, dtype=jnp.int32))}
    tabs_s = {"q": _rope_tables(jnp.tile(past_len + jnp.arange(t_s, dtype=jnp.int32), nb_s)),
              "kc": _rope_tables(jnp.tile(jnp.arange(past_len, dtype=jnp.int32), nb_s))}
    caches = (cache_fox_k, cache_fox_v, cache_fox_logf, cache_sb_k, cache_sb_v, cache_mla_ckv, cache_mla_krope)
    y_p, y_s = x_prompt, x_sample
    rows_p, rows_s = [], []
    for l in range(depth):
        p = _prep_layer(l, fox_f_bias, fox_q_norm, fox_k_norm, mla_qa_norm, mla_w_uq, mla_kva_norm, mla_w_ukv,
                        mla_q_norm, mla_k_norm, out_norm_a, out_norm_b, out_norm_c, w_in, w_out, w_gu, w_down,
                        norm_mix, norm_ffn)
        y_p, r_p = _layer(y_p, None, l, p, tabs_p)
        y_s, r_s = _layer(y_s, caches, l, p, tabs_s)
        rows_p.append(r_p)
        rows_s.append(r_s)
    outs_p = tuple(jnp.stack([r[i] for r in rows_p], axis=0) for i in range(7))
    outs_s = tuple(jnp.stack([r[i] for r in rows_s], axis=0) for i in range(7))
    return (y_p, y_s) + outs_p + outs_s
```

```python
import functools
import math

import jax
import jax.numpy as jnp
from jax import lax
from jax.experimental import pallas as pl
from jax.experimental.pallas import tpu as pltpu

D_MODEL = 2048
CHUNK = 64
HEAD_DIM = 128
N_HEADS = 4
W_GROUP = N_HEADS * HEAD_DIM
MLA_Q_RANK = 512
MLA_KV_RANK = 256
MLA_NOPE_DIM = 128
MLA_ROPE_DIM = 64
MLA_QK_DIM = MLA_NOPE_DIM + MLA_ROPE_DIM
MLA_HEAD_PAD = 256
ROPE_THETA = 10000.0
D_FF = 5632
RMS_EPS = 1e-6
NEG_INF = -1e30
LOG2E = math.log2(math.e)
N_IN_PAD = 4096
LANES = 128
TRI = 256
ATTN_TILE = 512
ROW_GROUP = 128
VMEM_LIMIT = 56 * 1024 * 1024

C_QA, C_KA, C_VA, C_QB, C_KB, C_VB, C_CQ, C_CKV, C_KR, C_FA = (
    0, 512, 1024, 1536, 2048, 2560, 3072, 3584, 3840, 3968)

BF16 = jnp.bfloat16
F32 = jnp.float32


def _params(*sem):
    return pltpu.CompilerParams(dimension_semantics=sem, vmem_limit_bytes=VMEM_LIMIT)


def _pick_tile(n, pref):
    t = min(n, pref)
    while n % t:
        t //= 2
    return t


def _rms(x, g):
    return x * lax.rsqrt(jnp.mean(x * x, axis=-1, keepdims=True) + RMS_EPS) * g


def _dot(a, b):
    return jnp.dot(a, b, preferred_element_type=F32)


def _dot_nt(a, b):
    return lax.dot_general(a, b, (((1,), (1,)), ((), ())), preferred_element_type=F32)


def _split2(x):
    hi = x.astype(BF16)
    lo = (x - hi.astype(F32)).astype(BF16)
    return hi, lo


def _neg_abs(x):
    bits = lax.bitcast_convert_type(x, jnp.uint32) | jnp.uint32(0x80000000)
    return lax.bitcast_convert_type(bits, F32)


def _split3(x):
    a = x.astype(BF16)
    r = x - a.astype(F32)
    b = r.astype(BF16)
    c = (r - b.astype(F32)).astype(BF16)
    return a, b, c


def _log_sigmoid(x):
    return -(jnp.maximum(-x, 0.0) + jnp.log1p(jnp.exp(-jnp.abs(x))))


def _log2_keep(z2):
    return -(jnp.maximum(z2, 0.0) + jnp.log2(1.0 + jnp.exp2(_neg_abs(z2))))


def _w_in_prep_kernel(w_ref, o_ref):
    rows = w_ref.shape[0]
    fa0 = C_QB
    o_ref[:, :C_QB] = w_ref[:, :fa0].astype(BF16)
    o_ref[:, C_QB:C_KR] = w_ref[:, fa0 + N_HEADS:C_KR + N_HEADS].astype(BF16)
    kr = w_ref[:, C_KR + N_HEADS:C_KR + N_HEADS + MLA_ROPE_DIM]
    half = MLA_ROPE_DIM // 2
    o_ref[:, C_KR:C_FA] = jnp.concatenate([kr, kr[:, half:], kr[:, :half]], axis=1).astype(BF16)
    fa = w_ref[:, fa0:fa0 + N_HEADS]
    o_ref[:, C_FA:] = jnp.concatenate([fa, jnp.zeros((rows, LANES - N_HEADS), F32)], axis=1).astype(BF16)


def w_in_prep(w_in):
    depth, d, n = w_in.shape
    tr = 256
    return pl.pallas_call(
        _w_in_prep_kernel,
        grid=(depth, d // tr),
        in_specs=[pl.BlockSpec((None, tr, n), lambda l, i: (l, i, 0))],
        out_specs=pl.BlockSpec((None, tr, N_IN_PAD), lambda l, i: (l, i, 0)),
        out_shape=jax.ShapeDtypeStruct((depth, d, N_IN_PAD), BF16),
        compiler_params=_params("parallel", "parallel"),
        name="w_in_prep",
    )(w_in)


def _rms_matmul_kernel(x_ref, g_ref, w_ref, o_ref, h_ref):
    @pl.when(pl.program_id(1) == 0)
    def _():
        h_ref[...] = _rms(x_ref[...], g_ref[...]).astype(BF16)

    o_ref[...] = _dot(h_ref[...], w_ref[...])


def rms_matmul(x, g, w, l):
    m, k = x.shape
    n = w.shape[2]
    tm = _pick_tile(m, 1024)
    tn = _pick_tile(n, 1024)
    return pl.pallas_call(
        _rms_matmul_kernel,
        grid=(m // tm, n // tn),
        in_specs=[pl.BlockSpec((tm, k), lambda i, j: (i, 0)),
                  pl.BlockSpec((1, k), lambda i, j: (0, 0)),
                  pl.BlockSpec((None, k, tn), lambda i, j: (l, 0, j))],
        out_specs=pl.BlockSpec((tm, tn), lambda i, j: (i, j)),
        out_shape=jax.ShapeDtypeStruct((m, n), F32),
        scratch_shapes=[pltpu.VMEM((tm, k), BF16)],
        compiler_params=_params("parallel", "arbitrary"),
        name="rms_matmul",
    )(x, g, w)


def _rope_chunk(y2, cos, sin):
    return y2 * cos + pltpu.roll(y2, 64, 1) * sin


def _rope_lane_mask():
    return lax.broadcasted_iota(jnp.int32, (1, LANES), 1) < MLA_ROPE_DIM


def _post_kernel(with_kt, n_alias, proj_ref, fqg_ref, fkg_ref, fb_ref, qag_ref, wuq_ref, qg_ref, kvg_ref, cos_ref,
                 sin_ref, *refs):
    refs = refs[n_alias:]
    ka_o, va_o, kb_o, vb_o, ckv_o, qa_o, vab_o, lf_o, qb_o, vbb_o, qc_o, kr_o, k2a_o, k2b_o = refs
    scale = HEAD_DIM ** -0.5 * LOG2E
    va = proj_ref[:, C_VA:C_VA + W_GROUP]
    vb = proj_ref[:, C_VB:C_VB + W_GROUP]
    for h in range(N_HEADS):
        sl = slice(h * HEAD_DIM, (h + 1) * HEAD_DIM)
        qa = proj_ref[:, C_QA + h * HEAD_DIM:C_QA + (h + 1) * HEAD_DIM]
        qa_o[:, sl] = (_rms(qa, fqg_ref[...]) * scale).astype(BF16)
        ka = _rms(proj_ref[:, C_KA + h * HEAD_DIM:C_KA + (h + 1) * HEAD_DIM], fkg_ref[...])
        kb = proj_ref[:, C_KB + h * HEAD_DIM:C_KB + (h + 1) * HEAD_DIM]
        ka_o[:, h, :] = ka
        kb_o[:, h, :] = kb
        va_o[:, h, :] = va[:, sl]
        vb_o[:, h, :] = vb[:, sl]
        if with_kt:
            k2a_o[h, 0] = ka.T.astype(BF16)
            k2b_o[h, 0] = kb.T.astype(BF16)
        else:
            k2a_o[:, sl] = ka.astype(BF16)
            k2b_o[:, sl] = kb.astype(BF16)
    vab_o[...] = va.astype(BF16)
    qb_o[...] = (proj_ref[:, C_QB:C_QB + W_GROUP] * scale).astype(BF16)
    vbb_o[...] = vb.astype(BF16)

    lane = lax.broadcasted_iota(jnp.int32, (1, LANES), 1)
    lf = _log_sigmoid(proj_ref[:, C_FA:C_FA + LANES] + fb_ref[...])
    lf_o[...] = jnp.where(lane < N_HEADS, lf, 0.0)

    cqn = _rms(proj_ref[:, C_CQ:C_CQ + MLA_Q_RANK], qag_ref[...]).astype(BF16)
    qc = _dot(cqn, wuq_ref[...])
    rmask = _rope_lane_mask()
    cos = cos_ref[...]
    sin = sin_ref[...]
    for h in range(N_HEADS):
        c1 = qc[:, h * MLA_HEAD_PAD:h * MLA_HEAD_PAD + LANES]
        c2 = qc[:, h * MLA_HEAD_PAD + LANES:(h + 1) * MLA_HEAD_PAD]
        ss = (jnp.sum(c1 * c1, axis=-1, keepdims=True)
              + jnp.sum(jnp.where(rmask, c2 * c2, 0.0), axis=-1, keepdims=True))
        rs = lax.rsqrt(ss * (1.0 / MLA_QK_DIM) + RMS_EPS)
        y1 = c1 * rs * qg_ref[:, :LANES]
        y2 = c2 * rs * qg_ref[:, LANES:]
        qc_o[:, h * MLA_HEAD_PAD:h * MLA_HEAD_PAD + LANES] = y1.astype(BF16)
        qc_o[:, h * MLA_HEAD_PAD + LANES:(h + 1) * MLA_HEAD_PAD] = _rope_chunk(y2, cos, sin).astype(BF16)

    ckv_o[...] = _rms(proj_ref[:, C_CKV:C_CKV + MLA_KV_RANK], kvg_ref[...])
    kr_o[...] = proj_ref[:, C_KR:C_KR + LANES]


def post(proj, p, cos, sin, with_kt, l, depth, stacks):
    m = proj.shape[0]
    tm = _pick_tile(m, ATTN_TILE)
    row = lambda w: pl.BlockSpec((tm, w), lambda i: (i, 0))
    full = lambda a: pl.BlockSpec(a.shape, lambda i: (0,) * a.ndim)
    consts = (p["fox_q_norm"], p["fox_k_norm"], p["fox_f_bias"], p["mla_qa_norm"], p["mla_w_uq"],
              p["mla_q_gain"], p["mla_kva_norm"])
    heads_spec = pl.BlockSpec((None, tm, N_HEADS, HEAD_DIM), lambda i: (l, i, 0, 0))
    heads_shape = jax.ShapeDtypeStruct((depth, m, N_HEADS, HEAD_DIM), F32)
    out_specs = [heads_spec] * 4 + [pl.BlockSpec((None, tm, MLA_KV_RANK), lambda i: (l, i, 0))]
    out_shape = [heads_shape] * 4 + [jax.ShapeDtypeStruct((depth, m, MLA_KV_RANK), F32)]
    outs = [(W_GROUP, BF16), (W_GROUP, BF16), (LANES, F32), (W_GROUP, BF16), (W_GROUP, BF16),
            (N_HEADS * MLA_HEAD_PAD, BF16), (LANES, F32)]
    out_specs += [row(w) for w, _ in outs]
    out_shape += [jax.ShapeDtypeStruct((m, w), dt) for w, dt in outs]
    for _ in range(2):
        if with_kt:
            out_specs.append(pl.BlockSpec((N_HEADS, 1, HEAD_DIM, tm), lambda i: (0, i, 0, 0)))
            out_shape.append(jax.ShapeDtypeStruct((N_HEADS, m // tm, HEAD_DIM, tm), BF16))
        else:
            out_specs.append(row(W_GROUP))
            out_shape.append(jax.ShapeDtypeStruct((m, W_GROUP), BF16))
    n_in = 1 + len(consts) + 2
    stacks = () if stacks is None else tuple(stacks)
    return pl.pallas_call(
        functools.partial(_post_kernel, with_kt, len(stacks)),
        grid=(m // tm,),
        in_specs=([row(N_IN_PAD)] + [full(a) for a in consts] + [row(LANES), row(LANES)]
                  + [pl.BlockSpec(memory_space=pl.ANY)] * len(stacks)),
        out_specs=out_specs,
        out_shape=out_shape,
        input_output_aliases={n_in + k: k for k in range(len(stacks))},
        compiler_params=_params("parallel"),
        name="post",
    )(proj, *consts, cos, sin, *stacks)


def _kexp_kernel(transposed, ckv_ref, kr_ref, w_ref, kg_ref, cos_ref, sin_ref, kc_o, vc_o):
    kv = _dot(ckv_ref[...].astype(BF16), w_ref[...])
    kr = kr_ref[...]
    krsq = jnp.sum(jnp.where(_rope_lane_mask(), kr * kr, 0.0), axis=-1, keepdims=True)
    cos = cos_ref[...]
    sin = sin_ref[...]
    for h in range(N_HEADS):
        n = kv[:, h * LANES:(h + 1) * LANES]
        ss = jnp.sum(n * n, axis=-1, keepdims=True) + krsq
        rs = lax.rsqrt(ss * (1.0 / MLA_QK_DIM) + RMS_EPS)
        k1 = n * rs * kg_ref[:, :LANES]
        k2 = _rope_chunk(kr * rs * kg_ref[:, LANES:], cos, sin)
        if transposed:
            kc_o[h, 0, :LANES, :] = k1.T.astype(BF16)
            kc_o[h, 0, LANES:, :] = k2.T.astype(BF16)
        else:
            kc_o[:, h * MLA_HEAD_PAD:h * MLA_HEAD_PAD + LANES] = k1.astype(BF16)
            kc_o[:, h * MLA_HEAD_PAD + LANES:(h + 1) * MLA_HEAD_PAD] = k2.astype(BF16)
    vc_o[...] = kv[:, W_GROUP:].astype(BF16)


def kexp(ckv, krdup, p, cos, sin, transposed, layer=None):
    m = krdup.shape[0]
    tm = _pick_tile(m, ATTN_TILE)
    row = lambda w: pl.BlockSpec((tm, w), lambda i: (i, 0))
    full = lambda a: pl.BlockSpec(a.shape, lambda i: (0,) * a.ndim)
    ckv_spec = row(MLA_KV_RANK) if layer is None else pl.BlockSpec(
        (None, tm, MLA_KV_RANK), lambda i: (layer, i, 0))
    if transposed:
        kc_spec = pl.BlockSpec((N_HEADS, 1, MLA_HEAD_PAD, tm), lambda i: (0, i, 0, 0))
        kc_shape = jax.ShapeDtypeStruct((N_HEADS, m // tm, MLA_HEAD_PAD, tm), BF16)
    else:
        kc_spec = row(N_HEADS * MLA_HEAD_PAD)
        kc_shape = jax.ShapeDtypeStruct((m, N_HEADS * MLA_HEAD_PAD), BF16)
    return pl.pallas_call(
        functools.partial(_kexp_kernel, transposed),
        grid=(m // tm,),
        in_specs=[ckv_spec, row(LANES), full(p["mla_w_ukv"]), full(p["mla_k_gain"]), row(LANES), row(LANES)],
        out_specs=[kc_spec, row(W_GROUP)],
        out_shape=[kc_shape, jax.ShapeDtypeStruct((m, W_GROUP), BF16)],
        compiler_params=_params("parallel"),
        name="kexp",
    )(ckv, krdup, p["mla_w_ukv"], p["mla_k_gain"], cos, sin)


def _cumsum_kernel(x_ref, o_ref, carry_ref):
    @pl.when(pl.program_id(1) == 0)
    def _():
        carry_ref[...] = jnp.zeros_like(carry_ref)

    t = x_ref.shape[1]
    xt = x_ref[0].T[:8, :]
    ri = lax.broadcasted_iota(jnp.int32, (t, t), 0)
    ci = lax.broadcasted_iota(jnp.int32, (t, t), 1)
    upper = jnp.where(ri <= ci, 1.0, 0.0).astype(BF16)
    a, b, c = _split3(xt)
    f = _dot(a, upper) + _dot(b, upper) + _dot(c, upper) + carry_ref[:, :1]
    o_ref[0] = f
    carry_ref[...] = jnp.broadcast_to(f[:, t - 1:t], carry_ref.shape)


def cumsum_rows(x):
    b, n, _ = x.shape
    t = n if n <= 1280 else _pick_tile(n, 512)
    return pl.pallas_call(
        _cumsum_kernel,
        grid=(b, n // t),
        in_specs=[pl.BlockSpec((1, t, LANES), lambda i, j: (i, j, 0))],
        out_specs=pl.BlockSpec((1, 8, t), lambda i, j: (i, 0, j)),
        out_shape=jax.ShapeDtypeStruct((b, 8, n), F32),
        scratch_shapes=[pltpu.VMEM((8, LANES), F32)],
        compiler_params=_params("parallel", "arbitrary"),
        name="cumsum_rows",
    )(x)


def _tile_mask(kind, qpos, kpos):
    if kind == "fox":
        return kpos <= qpos
    if kind == "sb":
        return kpos < qpos
    shift = CHUNK.bit_length() - 1
    return lax.shift_right_logical(kpos, shift) <= lax.shift_right_logical(qpos, shift)


def _tri_lower(n):
    ri = lax.broadcasted_iota(jnp.int32, (n, n), 0)
    ci = lax.broadcasted_iota(jnp.int32, (n, n), 1)
    return jnp.where(ri >= ci, 1.0, 0.0).astype(BF16)


def _sb_weights(z, carry, mask, tri):
    lk = _log2_keep(z)
    if mask is not None:
        lk = jnp.where(mask, lk, 0.0)
    blk = tri.shape[0]
    nblk = z.shape[1] // blk
    parts = [None] * nblk
    for c in reversed(range(nblk)):
        lkc = lk[:, c * blk:(c + 1) * blk]
        hi, lo = _split2(lkc)
        intra = _dot(hi, tri) + _dot(lo, tri)
        if carry.shape[1] == 1 or carry.shape[1] == blk:
            parts[c] = intra + carry
        else:
            parts[c] = intra + jnp.concatenate([carry] * (blk // carry.shape[1]), axis=1)
        carry = carry + jnp.sum(lkc, axis=-1, keepdims=True)
    r = parts[0] if nblk == 1 else jnp.concatenate(parts, axis=-1)
    a = jnp.exp2(z + r)
    if mask is not None:
        a = jnp.where(mask, a, 0.0)
    return a, carry


def _attn_prompt_kernel(kind, tq, *refs):
    f_ref = al_ref = None
    if kind == "fox":
        q_ref, kt_ref, v_ref, f_ref, o_ref, m_ref, acc_ref, s_ref, p_ref, al_ref = refs
    elif kind == "mla":
        q_ref, kt_ref, v_ref, o_ref, m_ref, acc_ref, s_ref, p_ref, al_ref = refs
    else:
        q_ref, kt_ref, v_ref, o_ref, m_ref, acc_ref, s_ref, p_ref = refs
    qb = pl.program_id(1)
    rg = min(TRI if kind == "sb" else ROW_GROUP, tq)
    n_rg = tq // rg
    q = q_ref[...]
    ones = jnp.ones((tq, LANES), BF16)
    tri2 = jnp.concatenate([_tri_lower(rg)] * 2, axis=0) if kind == "sb" else None
    fref = f_ref[0, qb][:, :1] if kind == "fox" else None

    def key_block(j):
        return jnp.clip(qb - j, 0, qb)

    def stage_a(j, slot):
        s_ref[slot] = _dot(q, kt_ref[0, key_block(j)])

    def stage_b(j, slot, diagonal):
        brow = (fref - f_ref[0, key_block(j)]) * LOG2E if kind == "fox" else None
        ms, als, prs = [], [], []
        for r in range(n_rg):
            rows = slice(r * rg, (r + 1) * rg)
            kw = (r + 1) * rg if diagonal else tq
            s = s_ref[slot, rows, :kw]
            mask = None
            if diagonal:
                qpos = r * rg + lax.broadcasted_iota(jnp.int32, (rg, kw), 0)
                kpos = lax.broadcasted_iota(jnp.int32, (rg, kw), 1)
                mask = _tile_mask(kind, qpos, kpos)
            m_prev = m_ref[rows, :]
            if kind == "sb":
                plk = jnp.maximum(s, 0.0) + jnp.log2(1.0 + jnp.exp2(_neg_abs(s)))
                if mask is not None:
                    plk = jnp.where(mask, plk, 0.0)
                carry = m_prev
                parts = [None] * (kw // rg)
                for c in reversed(range(kw // rg)):
                    pc = plk[:, c * rg:(c + 1) * rg]
                    hi, lo = _split2(pc)
                    later = _dot(jnp.concatenate([hi, lo], axis=1), tri2)
                    parts[c] = s[:, c * rg:(c + 1) * rg] - later - jnp.concatenate([carry] * (rg // LANES), 1)
                    carry = carry + jnp.sum(pc, axis=-1, keepdims=True)
                a = jnp.exp2(parts[0] if len(parts) == 1 else jnp.concatenate(parts, axis=1))
                if mask is not None:
                    a = jnp.where(mask, a, 0.0)
                pr = a.astype(BF16)
                ms.append(carry)
            else:
                if kind == "fox":
                    s = s + brow[:, :kw]
                if mask is not None:
                    s = jnp.where(mask, s, NEG_INF)
                chunks = [s[:, c * LANES:(c + 1) * LANES] for c in range(kw // LANES)]
                m_cur = jnp.max(functools.reduce(jnp.maximum, chunks), axis=-1, keepdims=True)
                m_new = jnp.maximum(m_prev, m_cur)
                als.append(jnp.exp2(m_prev - m_new))
                pr = jnp.concatenate([jnp.exp2(c - m_new).astype(BF16) for c in chunks], axis=1)
                ms.append(m_new)
            if kw < tq:
                pr = jnp.concatenate([pr, jnp.zeros((rg, tq - kw), BF16)], axis=1)
            prs.append(pr)
        m_ref[...] = jnp.concatenate(ms, axis=0)
        p_ref[slot] = jnp.concatenate(prs, axis=0)
        if kind != "sb":
            al_ref[slot] = jnp.concatenate(als, axis=0)

    def stage_c(j, slot):
        start = pl.multiple_of(key_block(j) * tq, tq)
        v = v_ref[pl.ds(start, tq), :]
        if kind == "sb":
            acc_ref[...] += _dot(p_ref[slot], v)
        else:
            al = al_ref[slot]
            pv = _dot(p_ref[slot], jnp.concatenate([v, ones], axis=1))
            acc_ref[...] = jnp.concatenate([al, al], axis=1) * acc_ref[...] + pv

    def step(j, slot):
        stage_b(j, slot, False)
        stage_a(j + 1, 1 - slot)
        stage_c(j - 1, 1 - slot)

    m_ref[...] = jnp.full(m_ref.shape, 0.0 if kind == "sb" else NEG_INF, F32)
    acc_ref[...] = jnp.zeros_like(acc_ref)
    stage_a(0, 0)
    stage_b(0, 0, True)
    stage_a(1, 1)

    def body(t, c):
        step(1 + 2 * t, 1)
        step(2 + 2 * t, 0)
        return c

    lax.fori_loop(0, qb // 2, body, 0)

    @pl.when(qb % 2 == 1)
    def _():
        step(qb, 1)
        stage_c(qb, 1)

    @pl.when(qb % 2 == 0)
    def _():
        stage_c(qb, 0)

    if kind == "sb":
        o_ref[...] = acc_ref[...]
    else:
        o_ref[...] = acc_ref[:, :HEAD_DIM] / acc_ref[:, HEAD_DIM:]


def attn_prompt(kind, q, kt, v, f=None):
    t = q.shape[0]
    dq = q.shape[1] // N_HEADS
    tq = kt.shape[3]
    nq = t // tq
    in_specs = [pl.BlockSpec((tq, dq), lambda h, i: (i, h)),
                pl.BlockSpec((1, nq, dq, tq), lambda h, i: (h, 0, 0, 0)),
                pl.BlockSpec((t, HEAD_DIM), lambda h, i: (0, h))]
    args = [q, kt, v]
    if kind == "fox":
        in_specs.append(pl.BlockSpec((1, nq, 1, tq), lambda h, i: (h, 0, 0, 0)))
        args.append(f.reshape(N_HEADS, nq, 1, tq))
    acc_w = HEAD_DIM if kind == "sb" else 2 * HEAD_DIM
    scratch = [pltpu.VMEM((tq, LANES), F32), pltpu.VMEM((tq, acc_w), F32),
               pltpu.VMEM((2, tq, tq), F32), pltpu.VMEM((2, tq, tq), BF16)]
    if kind != "sb":
        scratch.append(pltpu.VMEM((2, tq, LANES), F32))
    return pl.pallas_call(
        functools.partial(_attn_prompt_kernel, kind, tq),
        grid=(N_HEADS, nq),
        in_specs=in_specs,
        out_specs=pl.BlockSpec((tq, HEAD_DIM), lambda h, i: (i, h)),
        out_shape=jax.ShapeDtypeStruct((t, W_GROUP), F32),
        scratch_shapes=scratch,
        compiler_params=_params("parallel", "arbitrary"),
        name="attn_prompt_" + kind,
    )(*args)


def _softmax_tile(s, v, m_prev, l_prev, acc_prev):
    m_new = jnp.maximum(m_prev, jnp.max(s, axis=-1, keepdims=True))
    alpha = jnp.exp2(m_prev - m_new)
    pr = jnp.exp2(s - m_new)
    l_new = alpha * l_prev + jnp.sum(pr, axis=-1, keepdims=True)
    acc_new = alpha * acc_prev + _dot(pr.astype(BF16), v)
    return m_new, l_new, acc_new


def _attn_decode_kernel(kind, *refs):
    if kind == "fox":
        q_ref, kn_ref, vn_ref, kc_ref, vc_ref, fn_ref, fc_ref, o_ref = refs
    else:
        q_ref, kn_ref, vn_ref, kc_ref, vc_ref, o_ref = refs
    tq = q_ref.shape[0]
    dq = q_ref.shape[1] // N_HEADS
    past = kc_ref.shape[0]
    qpos = past + lax.broadcasted_iota(jnp.int32, (tq, tq), 0)
    kpos = past + lax.broadcasted_iota(jnp.int32, (tq, tq), 1)
    mask = _tile_mask(kind, qpos, kpos)
    for h in range(N_HEADS):
        q = q_ref[:, h * dq:(h + 1) * dq]
        kn = kn_ref[:, h * dq:(h + 1) * dq]
        vn = vn_ref[:, h * HEAD_DIM:(h + 1) * HEAD_DIM]
        if len(kc_ref.shape) == 3:
            kc = kc_ref[:, h, :].astype(BF16)
            vc = vc_ref[:, h, :].astype(BF16)
        else:
            kc = kc_ref[:, h * dq:(h + 1) * dq]
            vc = vc_ref[:, h * HEAD_DIM:(h + 1) * HEAD_DIM]
        s_n = _dot_nt(q, kn)
        s_c = _dot_nt(q, kc)
        if kind == "sb":
            a_n, carry = _sb_weights(s_n, jnp.zeros((tq, 1), F32), mask, _tri_lower(tq))
            a_c, _ = _sb_weights(s_c, carry, None, _tri_lower(min(TRI, past)))
            out = _dot(a_n.astype(BF16), vn) + _dot(a_c.astype(BF16), vc)
        else:
            if kind == "fox":
                fref = fn_ref[h][:, :1]
                s_n = s_n + (fref - fn_ref[h]) * LOG2E
                s_c = s_c + (fref - fc_ref[h]) * LOG2E
            s_n = jnp.where(mask, s_n, NEG_INF)
            m0 = jnp.full((tq, 1), NEG_INF, F32)
            z0 = jnp.zeros((tq, 1), F32)
            m, l, acc = _softmax_tile(s_n, vn, m0, z0, jnp.zeros((tq, HEAD_DIM), F32))
            m, l, acc = _softmax_tile(s_c, vc, m, l, acc)
            out = acc / l
        o_ref[:, h * HEAD_DIM:(h + 1) * HEAD_DIM] = out


def attn_decode(kind, q, kn, vn, kc, vc, layer=None, fn=None, fc=None):
    native = layer is not None
    nb, past = (kc.shape[1], kc.shape[2]) if native else (kc.shape[0], kc.shape[1])
    tq = q.shape[0] // nb
    new = lambda a: pl.BlockSpec((tq, a.shape[1]), lambda b: (b, 0))
    if native:
        old = lambda a: pl.BlockSpec((None, None, past, N_HEADS, HEAD_DIM), lambda b: (layer, b, 0, 0, 0))
    else:
        old = lambda a: pl.BlockSpec((None, past, a.shape[2]), lambda b: (b, 0, 0))
    in_specs = [new(q), new(kn), new(vn), old(kc), old(vc)]
    args = [q, kn, vn, kc, vc]
    if kind == "fox":
        in_specs += [pl.BlockSpec((None, N_HEADS, 1, tq), lambda b: (b, 0, 0, 0)),
                     pl.BlockSpec((None, N_HEADS, 1, past), lambda b: (b, 0, 0, 0))]
        args += [fn, fc]
    return pl.pallas_call(
        functools.partial(_attn_decode_kernel, kind),
        grid=(nb,),
        in_specs=in_specs,
        out_specs=pl.BlockSpec((tq, W_GROUP), lambda b: (b, 0)),
        out_shape=jax.ShapeDtypeStruct((nb * tq, W_GROUP), F32),
        compiler_params=_params("parallel"),
        name="attn_decode_" + kind,
    )(*args)


def _out_proj_kernel(oa_ref, ob_ref, oc_ref, ga_ref, gb_ref, gc_ref, w_ref, x_ref, o_ref):
    acc = x_ref[...]
    for g, (o, gn) in enumerate(((oa_ref, ga_ref), (ob_ref, gb_ref), (oc_ref, gc_ref))):
        y = _rms(o[...], gn[...]).astype(BF16)
        acc = acc + _dot(y, w_ref[g * W_GROUP:(g + 1) * W_GROUP, :])
    o_ref[...] = acc


def out_proj(oa, ob, oc, p, x, l):
    m = x.shape[0]
    tm = _pick_tile(m, 512)
    row = lambda w: pl.BlockSpec((tm, w), lambda i: (i, 0))
    full = lambda a: pl.BlockSpec(a.shape, lambda i: (0,) * a.ndim)
    consts = (p["out_norm_a"], p["out_norm_b"], p["out_norm_c"], p["w_out"])
    w_spec = pl.BlockSpec((None,) + p["w_out"].shape[1:], lambda i: (l, 0, 0))
    return pl.pallas_call(
        _out_proj_kernel,
        grid=(m // tm,),
        in_specs=[row(W_GROUP)] * 3 + [full(a) for a in consts[:3]] + [w_spec, row(D_MODEL)],
        out_specs=row(D_MODEL),
        out_shape=jax.ShapeDtypeStruct((m, D_MODEL), F32),
        compiler_params=_params("parallel"),
        name="out_proj",
    )(oa, ob, oc, *consts, x)


def _ffn_kernel(x_ref, g_ref, wg_ref, wu_ref, wd_ref, o_ref, h_ref):
    @pl.when(pl.program_id(1) == 0)
    def _():
        x = x_ref[...]
        h_ref[...] = _rms(x, g_ref[...]).astype(BF16)
        o_ref[...] = x

    h = h_ref[...]
    gate = _dot(h, wg_ref[...])
    up = _dot(h, wu_ref[...])
    act = (gate * jax.nn.sigmoid(gate) * up).astype(BF16)
    o_ref[...] += _dot(act, wd_ref[...])


def ffn(x, g, w_gu, w_down, l):
    m = x.shape[0]
    tm = _pick_tile(m, 512)
    tf = 512
    nf = D_FF // tf
    return pl.pallas_call(
        _ffn_kernel,
        grid=(m // tm, nf),
        in_specs=[pl.BlockSpec((tm, D_MODEL), lambda i, j: (i, 0)),
                  pl.BlockSpec((1, D_MODEL), lambda i, j: (0, 0)),
                  pl.BlockSpec((None, D_MODEL, tf), lambda i, j: (l, 0, j)),
                  pl.BlockSpec((None, D_MODEL, tf), lambda i, j: (l, 0, j + nf)),
                  pl.BlockSpec((None, tf, D_MODEL), lambda i, j: (l, j, 0))],
        out_specs=pl.BlockSpec((tm, D_MODEL), lambda i, j: (i, 0)),
        out_shape=jax.ShapeDtypeStruct((m, D_MODEL), F32),
        scratch_shapes=[pltpu.VMEM((tm, D_MODEL), BF16)],
        compiler_params=_params("parallel", "arbitrary"),
        name="ffn",
    )(x, g, w_gu, w_gu, w_down)


def _dup_rope(r):
    half = MLA_ROPE_DIM // 2
    return jnp.concatenate([r, r[..., half:], r[..., :half]], axis=-1)


def _prep_layer(l, fox_f_bias, fox_q_norm, fox_k_norm, mla_qa_norm, mla_w_uq, mla_kva_norm, mla_w_ukv,
                mla_q_norm, mla_k_norm, out_norm_a, out_norm_b, out_norm_c, w_in, w_out, w_gu, w_down,
                norm_mix, norm_ffn):
    uq = mla_w_uq[l].reshape(MLA_Q_RANK, N_HEADS, MLA_QK_DIM)
    uq = jnp.concatenate([uq[..., :MLA_NOPE_DIM], _dup_rope(uq[..., MLA_NOPE_DIM:])], axis=-1)
    ukv = mla_w_ukv[l].reshape(MLA_KV_RANK, N_HEADS, 2 * LANES)
    ukv = jnp.concatenate([ukv[..., :LANES].reshape(MLA_KV_RANK, W_GROUP),
                           ukv[..., LANES:].reshape(MLA_KV_RANK, W_GROUP)], axis=1)

    def gain256(g):
        return jnp.concatenate([g[:MLA_NOPE_DIM], _dup_rope(g[MLA_NOPE_DIM:])])[None, :]

    row = lambda a: a[l][None, :]
    return {
        "norm_mix": row(norm_mix), "w_in": w_in,
        "fox_q_norm": row(fox_q_norm), "fox_k_norm": row(fox_k_norm),
        "fox_f_bias": jnp.pad(fox_f_bias[l], (0, LANES - N_HEADS))[None, :],
        "mla_qa_norm": row(mla_qa_norm),
        "mla_w_uq": uq.reshape(MLA_Q_RANK, N_HEADS * MLA_HEAD_PAD).astype(BF16),
        "mla_q_gain": gain256(mla_q_norm[l]) * (MLA_QK_DIM ** -0.5 * LOG2E),
        "mla_kva_norm": row(mla_kva_norm),
        "mla_w_ukv": ukv.astype(BF16),
        "mla_k_gain": gain256(mla_k_norm[l]),
        "out_norm_a": row(out_norm_a), "out_norm_b": row(out_norm_b), "out_norm_c": row(out_norm_c),
        "w_out": w_out,
        "norm_ffn": row(norm_ffn), "w_gu": w_gu, "w_down": w_down,
    }


def _rope_tables(pos):
    half = MLA_ROPE_DIM // 2
    inv_freq = ROPE_THETA ** (-(jnp.arange(half, dtype=F32) / half))
    ang = pos.astype(F32)[:, None] * inv_freq[None, :]
    cos, sin = jnp.cos(ang), jnp.sin(ang)
    zero = jnp.zeros_like(cos)
    return (jnp.concatenate([cos, cos, zero, zero], axis=1),
            jnp.concatenate([-sin, sin, zero, zero], axis=1))


def _layer(x, caches, l, depth, p, tabs, stacks):
    nb, t, _ = x.shape
    m = nb * t
    x2 = x.reshape(m, D_MODEL)
    proj = rms_matmul(x2, p["norm_mix"], p["w_in"], l)
    prompt = caches is None
    outs = post(proj, p, *tabs["q"], with_kt=prompt, l=l, depth=depth, stacks=stacks)
    stacks = outs[:5]
    qa, vab, lf, qb, vbb, qc, krd, k2a, k2b = outs[5:]
    if prompt:
        kct, vc_new = kexp(stacks[4], krd, p, *tabs["q"], transposed=True, layer=l)
        f = cumsum_rows(lf.reshape(nb, t, LANES))
        oa = attn_prompt("fox", qa, k2a, vab, f[0, :N_HEADS])
        ob = attn_prompt("sb", qb, k2b, vbb)
        oc = attn_prompt("mla", qc, kct, vc_new)
    else:
        c_fk, c_fv, c_lf, c_sk, c_sv, c_ckv, c_kr = caches
        pl_ = c_fk.shape[2]
        kc_new, vc_new = kexp(stacks[4], krd, p, *tabs["q"], transposed=False, layer=l)
        lf_all = jnp.concatenate(
            [jnp.pad(c_lf[l], ((0, 0), (0, 0), (0, LANES - N_HEADS))), lf.reshape(nb, t, LANES)], axis=1)
        n_pad = -(-(pl_ + t) // LANES) * LANES
        lf_all = jnp.pad(lf_all, ((0, 0), (0, n_pad - pl_ - t), (0, 0)))
        f = cumsum_rows(lf_all)[:, :N_HEADS, None, :]
        kc_old, vc_old = kexp(c_ckv.reshape(c_ckv.shape[0], nb * pl_, MLA_KV_RANK),
                              _dup_rope(c_kr[l]).reshape(nb * pl_, LANES), p, *tabs["kc"],
                              transposed=False, layer=l)
        oa = attn_decode("fox", qa, k2a, vab, c_fk, c_fv, layer=l, fn=f[..., pl_:pl_ + t], fc=f[..., :pl_])
        ob = attn_decode("sb", qb, k2b, vbb, c_sk, c_sv, layer=l)
        oc = attn_decode("mla", qc, kc_new, vc_new, kc_old.reshape(nb, pl_, -1), vc_old.reshape(nb, pl_, -1))
    x2 = out_proj(oa, ob, oc, p, x2, l)
    x2 = ffn(x2, p["norm_ffn"], p["w_gu"], p["w_down"], l)
    small = (lf[:, :N_HEADS].reshape(nb, t, N_HEADS), krd[:, :MLA_ROPE_DIM].reshape(nb, t, MLA_ROPE_DIM))
    return x2.reshape(nb, t, D_MODEL), stacks, small


def kernel(x_prompt, x_sample, cache_fox_k, cache_fox_v, cache_fox_logf, cache_sb_k, cache_sb_v, cache_mla_ckv, cache_mla_krope, norm_mix, w_in, fox_f_bias, fox_q_norm, fox_k_norm, mla_qa_norm, mla_w_uq, mla_kva_norm, mla_w_ukv, mla_q_norm, mla_k_norm, out_norm_a, out_norm_b, out_norm_c, w_out, norm_ffn, w_gu, w_down):
    depth = w_in.shape[0]
    t_p = x_prompt.shape[1]
    nb_s, t_s = x_sample.shape[0], x_sample.shape[1]
    past_len = cache_fox_k.shape[2]
    tabs_p = {"q": _rope_tables(jnp.arange(t_p, dtype=jnp.int32))}
    tabs_s = {"q": _rope_tables(jnp.tile(past_len + jnp.arange(t_s, dtype=jnp.int32), nb_s)),
              "kc": _rope_tables(jnp.tile(jnp.arange(past_len, dtype=jnp.int32), nb_s))}
    caches = (cache_fox_k, cache_fox_v, cache_fox_logf, cache_sb_k, cache_sb_v, cache_mla_ckv, cache_mla_krope)
    y_p, y_s = x_prompt, x_sample
    rows_p, rows_s = [], []
    stacks_p = stacks_s = None
    w_in_b = w_in_prep(w_in)
    w_out_b, w_gu_b, w_down_b = w_out.astype(BF16), w_gu.astype(BF16), w_down.astype(BF16)
    for l in range(depth):
        p = _prep_layer(l, fox_f_bias, fox_q_norm, fox_k_norm, mla_qa_norm, mla_w_uq, mla_kva_norm, mla_w_ukv,
                        mla_q_norm, mla_k_norm, out_norm_a, out_norm_b, out_norm_c, w_in_b, w_out_b, w_gu_b,
                        w_down_b, norm_mix, norm_ffn)
        y_p, stacks_p, r_p = _layer(y_p, None, l, depth, p, tabs_p, stacks_p)
        y_s, stacks_s, r_s = _layer(y_s, caches, l, depth, p, tabs_s, stacks_s)
        rows_p.append(r_p)
        rows_s.append(r_s)

    def assemble(stacks, rows, nb, t):
        fk, fv, sk, sv, ckv = stacks
        heads = lambda a: a.reshape(depth, nb, t, N_HEADS, HEAD_DIM)
        return (heads(fk), heads(fv), jnp.stack([r[0] for r in rows], axis=0), heads(sk), heads(sv),
                ckv.reshape(depth, nb, t, MLA_KV_RANK), jnp.stack([r[1] for r in rows], axis=0))

    return ((y_p, y_s) + assemble(stacks_p, rows_p, x_prompt.shape[0], t_p)
            + assemble(stacks_s, rows_s, nb_s, t_s))
```

```python
import functools
import math

import jax
import jax.numpy as jnp
from jax import lax
from jax.experimental import pallas as pl
from jax.experimental.pallas import tpu as pltpu

D_MODEL = 2048
CHUNK = 64
HEAD_DIM = 128
N_HEADS = 4
W_GROUP = N_HEADS * HEAD_DIM
MLA_Q_RANK = 512
MLA_KV_RANK = 256
MLA_NOPE_DIM = 128
MLA_ROPE_DIM = 64
MLA_QK_DIM = MLA_NOPE_DIM + MLA_ROPE_DIM
MLA_HEAD_PAD = 256
ROPE_THETA = 10000.0
D_FF = 5632
RMS_EPS = 1e-6
NEG_INF = -1e30
LOG2E = math.log2(math.e)
N_IN_PAD = 4096
LANES = 128
TRI = 256
ATTN_TILE = 512
ROW_GROUP = 128
UNDERFLOW_BITS = 160.0
VMEM_LIMIT = 56 * 1024 * 1024

C_QA, C_KA, C_VA, C_QB, C_KB, C_VB, C_CQ, C_CKV, C_KR, C_FA = (
    0, 512, 1024, 1536, 2048, 2560, 3072, 3584, 3840, 3968)

BF16 = jnp.bfloat16
F32 = jnp.float32


def _params(*sem):
    return pltpu.CompilerParams(dimension_semantics=sem, vmem_limit_bytes=VMEM_LIMIT)


def _pick_tile(n, pref):
    t = min(n, pref)
    while n % t:
        t //= 2
    return t


def _rms(x, g):
    return x * lax.rsqrt(jnp.mean(x * x, axis=-1, keepdims=True) + RMS_EPS) * g


def _dot(a, b):
    return jnp.dot(a, b, preferred_element_type=F32)


def _dot_nt(a, b):
    return lax.dot_general(a, b, (((1,), (1,)), ((), ())), preferred_element_type=F32)


def _split2(x):
    hi = x.astype(BF16)
    lo = (x - hi.astype(F32)).astype(BF16)
    return hi, lo


def _neg_abs(x):
    bits = lax.bitcast_convert_type(x, jnp.uint32) | jnp.uint32(0x80000000)
    return lax.bitcast_convert_type(bits, F32)


def _split3(x):
    a = x.astype(BF16)
    r = x - a.astype(F32)
    b = r.astype(BF16)
    c = (r - b.astype(F32)).astype(BF16)
    return a, b, c


def _log_sigmoid(x):
    return -(jnp.maximum(-x, 0.0) + jnp.log1p(jnp.exp(-jnp.abs(x))))


def _log2_keep(z2):
    return -(jnp.maximum(z2, 0.0) + jnp.log2(1.0 + jnp.exp2(_neg_abs(z2))))


def _w_in_prep_kernel(w_ref, o_ref):
    rows = w_ref.shape[0]
    fa0 = C_QB
    o_ref[:, :C_QB] = w_ref[:, :fa0].astype(BF16)
    o_ref[:, C_QB:C_KR] = w_ref[:, fa0 + N_HEADS:C_KR + N_HEADS].astype(BF16)
    kr = w_ref[:, C_KR + N_HEADS:C_KR + N_HEADS + MLA_ROPE_DIM]
    half = MLA_ROPE_DIM // 2
    o_ref[:, C_KR:C_FA] = jnp.concatenate([kr, kr[:, half:], kr[:, :half]], axis=1).astype(BF16)
    fa = w_ref[:, fa0:fa0 + N_HEADS]
    o_ref[:, C_FA:] = jnp.concatenate([fa, jnp.zeros((rows, LANES - N_HEADS), F32)], axis=1).astype(BF16)


def w_in_prep(w_in):
    depth, d, n = w_in.shape
    tr = 256
    return pl.pallas_call(
        _w_in_prep_kernel,
        grid=(depth, d // tr),
        in_specs=[pl.BlockSpec((None, tr, n), lambda l, i: (l, i, 0))],
        out_specs=pl.BlockSpec((None, tr, N_IN_PAD), lambda l, i: (l, i, 0)),
        out_shape=jax.ShapeDtypeStruct((depth, d, N_IN_PAD), BF16),
        compiler_params=_params("parallel", "parallel"),
        name="w_in_prep",
    )(w_in)


def _rms_matmul_kernel(x_ref, g_ref, w_ref, o_ref, h_ref):
    @pl.when(pl.program_id(1) == 0)
    def _():
        h_ref[...] = _rms(x_ref[...], g_ref[...]).astype(BF16)

    o_ref[...] = _dot(h_ref[...], w_ref[...])


def rms_matmul(x, g, w, l):
    m, k = x.shape
    n = w.shape[2]
    tm = _pick_tile(m, 1024)
    tn = _pick_tile(n, 1024)
    return pl.pallas_call(
        _rms_matmul_kernel,
        grid=(m // tm, n // tn),
        in_specs=[pl.BlockSpec((tm, k), lambda i, j: (i, 0)),
                  pl.BlockSpec((1, k), lambda i, j: (0, 0)),
                  pl.BlockSpec((None, k, tn), lambda i, j: (l, 0, j))],
        out_specs=pl.BlockSpec((tm, tn), lambda i, j: (i, j)),
        out_shape=jax.ShapeDtypeStruct((m, n), F32),
        scratch_shapes=[pltpu.VMEM((tm, k), BF16)],
        compiler_params=_params("parallel", "arbitrary"),
        name="rms_matmul",
    )(x, g, w)


def _rope_chunk(y2, cos, sin):
    return y2 * cos + pltpu.roll(y2, 64, 1) * sin


def _rope_lane_mask():
    return lax.broadcasted_iota(jnp.int32, (1, LANES), 1) < MLA_ROPE_DIM


def _post_kernel(with_kt, n_alias, proj_ref, fqg_ref, fkg_ref, fb_ref, qag_ref, wuq_ref, qg_ref, kvg_ref, cos_ref,
                 sin_ref, *refs):
    refs = refs[n_alias:]
    ka_o, va_o, kb_o, vb_o, ckv_o, qa_o, vab_o, lf_o, qb_o, vbb_o, qc_o, kr_o, k2a_o, k2b_o = refs
    scale = HEAD_DIM ** -0.5 * LOG2E
    va = proj_ref[:, C_VA:C_VA + W_GROUP]
    vb = proj_ref[:, C_VB:C_VB + W_GROUP]
    for h in range(N_HEADS):
        sl = slice(h * HEAD_DIM, (h + 1) * HEAD_DIM)
        qa = proj_ref[:, C_QA + h * HEAD_DIM:C_QA + (h + 1) * HEAD_DIM]
        qa_o[:, sl] = (_rms(qa, fqg_ref[...]) * scale).astype(BF16)
        ka = _rms(proj_ref[:, C_KA + h * HEAD_DIM:C_KA + (h + 1) * HEAD_DIM], fkg_ref[...])
        kb = proj_ref[:, C_KB + h * HEAD_DIM:C_KB + (h + 1) * HEAD_DIM]
        ka_o[:, h, :] = ka
        kb_o[:, h, :] = kb
        va_o[:, h, :] = va[:, sl]
        vb_o[:, h, :] = vb[:, sl]
        if with_kt:
            k2a_o[h, 0] = ka.T.astype(BF16)
            k2b_o[h, 0] = kb.T.astype(BF16)
        else:
            k2a_o[:, sl] = ka.astype(BF16)
            k2b_o[:, sl] = kb.astype(BF16)
    vab_o[...] = va.astype(BF16)
    qb_o[...] = (proj_ref[:, C_QB:C_QB + W_GROUP] * scale).astype(BF16)
    vbb_o[...] = vb.astype(BF16)

    lane = lax.broadcasted_iota(jnp.int32, (1, LANES), 1)
    lf = _log_sigmoid(proj_ref[:, C_FA:C_FA + LANES] + fb_ref[...])
    lf_o[...] = jnp.where(lane < N_HEADS, lf, 0.0)

    cqn = _rms(proj_ref[:, C_CQ:C_CQ + MLA_Q_RANK], qag_ref[...]).astype(BF16)
    qc = _dot(cqn, wuq_ref[...])
    rmask = _rope_lane_mask()
    cos = cos_ref[...]
    sin = sin_ref[...]
    for h in range(N_HEADS):
        c1 = qc[:, h * MLA_HEAD_PAD:h * MLA_HEAD_PAD + LANES]
        c2 = qc[:, h * MLA_HEAD_PAD + LANES:(h + 1) * MLA_HEAD_PAD]
        ss = (jnp.sum(c1 * c1, axis=-1, keepdims=True)
              + jnp.sum(jnp.where(rmask, c2 * c2, 0.0), axis=-1, keepdims=True))
        rs = lax.rsqrt(ss * (1.0 / MLA_QK_DIM) + RMS_EPS)
        y1 = c1 * rs * qg_ref[:, :LANES]
        y2 = c2 * rs * qg_ref[:, LANES:]
        qc_o[:, h * MLA_HEAD_PAD:h * MLA_HEAD_PAD + LANES] = y1.astype(BF16)
        qc_o[:, h * MLA_HEAD_PAD + LANES:(h + 1) * MLA_HEAD_PAD] = _rope_chunk(y2, cos, sin).astype(BF16)

    ckv_o[...] = _rms(proj_ref[:, C_CKV:C_CKV + MLA_KV_RANK], kvg_ref[...])
    kr_o[...] = proj_ref[:, C_KR:C_KR + LANES]


def post(proj, p, cos, sin, with_kt, l, depth, stacks):
    m = proj.shape[0]
    tm = _pick_tile(m, ATTN_TILE)
    row = lambda w: pl.BlockSpec((tm, w), lambda i: (i, 0))
    full = lambda a: pl.BlockSpec(a.shape, lambda i: (0,) * a.ndim)
    consts = (p["fox_q_norm"], p["fox_k_norm"], p["fox_f_bias"], p["mla_qa_norm"], p["mla_w_uq"],
              p["mla_q_gain"], p["mla_kva_norm"])
    heads_spec = pl.BlockSpec((None, tm, N_HEADS, HEAD_DIM), lambda i: (l, i, 0, 0))
    heads_shape = jax.ShapeDtypeStruct((depth, m, N_HEADS, HEAD_DIM), F32)
    out_specs = [heads_spec] * 4 + [pl.BlockSpec((None, tm, MLA_KV_RANK), lambda i: (l, i, 0))]
    out_shape = [heads_shape] * 4 + [jax.ShapeDtypeStruct((depth, m, MLA_KV_RANK), F32)]
    outs = [(W_GROUP, BF16), (W_GROUP, BF16), (LANES, F32), (W_GROUP, BF16), (W_GROUP, BF16),
            (N_HEADS * MLA_HEAD_PAD, BF16), (LANES, F32)]
    out_specs += [row(w) for w, _ in outs]
    out_shape += [jax.ShapeDtypeStruct((m, w), dt) for w, dt in outs]
    for _ in range(2):
        if with_kt:
            out_specs.append(pl.BlockSpec((N_HEADS, 1, HEAD_DIM, tm), lambda i: (0, i, 0, 0)))
            out_shape.append(jax.ShapeDtypeStruct((N_HEADS, m // tm, HEAD_DIM, tm), BF16))
        else:
            out_specs.append(row(W_GROUP))
            out_shape.append(jax.ShapeDtypeStruct((m, W_GROUP), BF16))
    n_in = 1 + len(consts) + 2
    stacks = () if stacks is None else tuple(stacks)
    return pl.pallas_call(
        functools.partial(_post_kernel, with_kt, len(stacks)),
        grid=(m // tm,),
        in_specs=([row(N_IN_PAD)] + [full(a) for a in consts] + [row(LANES), row(LANES)]
                  + [pl.BlockSpec(memory_space=pl.ANY)] * len(stacks)),
        out_specs=out_specs,
        out_shape=out_shape,
        input_output_aliases={n_in + k: k for k in range(len(stacks))},
        compiler_params=_params("parallel"),
        name="post",
    )(proj, *consts, cos, sin, *stacks)


def _kexp_kernel(transposed, ckv_ref, kr_ref, w_ref, kg_ref, cos_ref, sin_ref, kc_o, vc_o):
    kv = _dot(ckv_ref[...].astype(BF16), w_ref[...])
    kr = kr_ref[...]
    krsq = jnp.sum(jnp.where(_rope_lane_mask(), kr * kr, 0.0), axis=-1, keepdims=True)
    cos = cos_ref[...]
    sin = sin_ref[...]
    for h in range(N_HEADS):
        n = kv[:, h * LANES:(h + 1) * LANES]
        ss = jnp.sum(n * n, axis=-1, keepdims=True) + krsq
        rs = lax.rsqrt(ss * (1.0 / MLA_QK_DIM) + RMS_EPS)
        k1 = n * rs * kg_ref[:, :LANES]
        k2 = _rope_chunk(kr * rs * kg_ref[:, LANES:], cos, sin)
        if transposed:
            kc_o[h, 0, :LANES, :] = k1.T.astype(BF16)
            kc_o[h, 0, LANES:, :] = k2.T.astype(BF16)
        else:
            kc_o[:, h * MLA_HEAD_PAD:h * MLA_HEAD_PAD + LANES] = k1.astype(BF16)
            kc_o[:, h * MLA_HEAD_PAD + LANES:(h + 1) * MLA_HEAD_PAD] = k2.astype(BF16)
    vc_o[...] = kv[:, W_GROUP:].astype(BF16)


def kexp(ckv, krdup, p, cos, sin, transposed, layer=None):
    m = krdup.shape[0]
    tm = _pick_tile(m, ATTN_TILE)
    row = lambda w: pl.BlockSpec((tm, w), lambda i: (i, 0))
    full = lambda a: pl.BlockSpec(a.shape, lambda i: (0,) * a.ndim)
    ckv_spec = row(MLA_KV_RANK) if layer is None else pl.BlockSpec(
        (None, tm, MLA_KV_RANK), lambda i: (layer, i, 0))
    if transposed:
        kc_spec = pl.BlockSpec((N_HEADS, 1, MLA_HEAD_PAD, tm), lambda i: (0, i, 0, 0))
        kc_shape = jax.ShapeDtypeStruct((N_HEADS, m // tm, MLA_HEAD_PAD, tm), BF16)
    else:
        kc_spec = row(N_HEADS * MLA_HEAD_PAD)
        kc_shape = jax.ShapeDtypeStruct((m, N_HEADS * MLA_HEAD_PAD), BF16)
    return pl.pallas_call(
        functools.partial(_kexp_kernel, transposed),
        grid=(m // tm,),
        in_specs=[ckv_spec, row(LANES), full(p["mla_w_ukv"]), full(p["mla_k_gain"]), row(LANES), row(LANES)],
        out_specs=[kc_spec, row(W_GROUP)],
        out_shape=[kc_shape, jax.ShapeDtypeStruct((m, W_GROUP), BF16)],
        compiler_params=_params("parallel"),
        name="kexp",
    )(ckv, krdup, p["mla_w_ukv"], p["mla_k_gain"], cos, sin)


def _cumsum_kernel(x_ref, o_ref, carry_ref):
    @pl.when(pl.program_id(1) == 0)
    def _():
        carry_ref[...] = jnp.zeros_like(carry_ref)

    t = x_ref.shape[1]
    xt = x_ref[0].T[:8, :]
    ri = lax.broadcasted_iota(jnp.int32, (t, t), 0)
    ci = lax.broadcasted_iota(jnp.int32, (t, t), 1)
    upper = jnp.where(ri <= ci, 1.0, 0.0).astype(BF16)
    a, b, c = _split3(xt)
    f = _dot(a, upper) + _dot(b, upper) + _dot(c, upper) + carry_ref[:, :1]
    o_ref[0] = f
    carry_ref[...] = jnp.broadcast_to(f[:, t - 1:t], carry_ref.shape)


def cumsum_rows(x):
    b, n, _ = x.shape
    t = n if n <= 1280 else _pick_tile(n, 512)
    return pl.pallas_call(
        _cumsum_kernel,
        grid=(b, n // t),
        in_specs=[pl.BlockSpec((1, t, LANES), lambda i, j: (i, j, 0))],
        out_specs=pl.BlockSpec((1, 8, t), lambda i, j: (i, 0, j)),
        out_shape=jax.ShapeDtypeStruct((b, 8, n), F32),
        scratch_shapes=[pltpu.VMEM((8, LANES), F32)],
        compiler_params=_params("parallel", "arbitrary"),
        name="cumsum_rows",
    )(x)


def _tile_mask(kind, qpos, kpos):
    if kind == "fox":
        return kpos <= qpos
    if kind == "sb":
        return kpos < qpos
    shift = CHUNK.bit_length() - 1
    return lax.shift_right_logical(kpos, shift) <= lax.shift_right_logical(qpos, shift)


def _tri_lower(n):
    ri = lax.broadcasted_iota(jnp.int32, (n, n), 0)
    ci = lax.broadcasted_iota(jnp.int32, (n, n), 1)
    return jnp.where(ri >= ci, 1.0, 0.0).astype(BF16)


def _sb_weights(z, carry, mask, tri):
    lk = _log2_keep(z)
    if mask is not None:
        lk = jnp.where(mask, lk, 0.0)
    blk = tri.shape[0]
    nblk = z.shape[1] // blk
    parts = [None] * nblk
    for c in reversed(range(nblk)):
        lkc = lk[:, c * blk:(c + 1) * blk]
        hi, lo = _split2(lkc)
        intra = _dot(hi, tri) + _dot(lo, tri)
        if carry.shape[1] == 1 or carry.shape[1] == blk:
            parts[c] = intra + carry
        else:
            parts[c] = intra + jnp.concatenate([carry] * (blk // carry.shape[1]), axis=1)
        carry = carry + jnp.sum(lkc, axis=-1, keepdims=True)
    r = parts[0] if nblk == 1 else jnp.concatenate(parts, axis=-1)
    a = jnp.exp2(z + r)
    if mask is not None:
        a = jnp.where(mask, a, 0.0)
    return a, carry


def _attn_prompt_kernel(kind, tq, *refs):
    f_ref = al_ref = kmax_ref = None
    if kind == "fox":
        q_ref, kt_ref, v_ref, f_ref, o_ref, m_ref, acc_ref, s_ref, p_ref, al_ref, kmax_ref = refs
    elif kind == "mla":
        q_ref, kt_ref, v_ref, o_ref, m_ref, acc_ref, s_ref, p_ref, al_ref = refs
    else:
        q_ref, kt_ref, v_ref, o_ref, m_ref, acc_ref, s_ref, p_ref, kmax_ref = refs
    early = kind != "mla"
    qb = pl.program_id(1)
    rg = min(TRI if kind == "sb" else ROW_GROUP, tq)
    n_rg = tq // rg
    q = q_ref[...]
    ones = jnp.ones((tq, LANES), BF16)
    tri2 = jnp.concatenate([_tri_lower(rg)] * 2, axis=0) if kind == "sb" else None
    fref = f_ref[0, qb][:, :1] if kind == "fox" else None

    def key_block(j):
        return jnp.clip(qb - j, 0, qb)

    def stage_a(j, slot):
        s_ref[slot] = _dot(q, kt_ref[0, key_block(j)])

    def stage_b(j, slot, diagonal):
        brow = (fref - f_ref[0, key_block(j)]) * LOG2E if kind == "fox" else None
        ms, als, prs = [], [], []
        for r in range(n_rg):
            rows = slice(r * rg, (r + 1) * rg)
            kw = (r + 1) * rg if diagonal else tq
            s = s_ref[slot, rows, :kw]
            mask = None
            if diagonal:
                qpos = r * rg + lax.broadcasted_iota(jnp.int32, (rg, kw), 0)
                kpos = lax.broadcasted_iota(jnp.int32, (rg, kw), 1)
                mask = _tile_mask(kind, qpos, kpos)
            m_prev = m_ref[rows, :]
            if kind == "sb":
                plk = jnp.maximum(s, 0.0) + jnp.log2(1.0 + jnp.exp2(_neg_abs(s)))
                if mask is not None:
                    plk = jnp.where(mask, plk, 0.0)
                carry = m_prev
                parts = [None] * (kw // rg)
                for c in reversed(range(kw // rg)):
                    pc = plk[:, c * rg:(c + 1) * rg]
                    hi, lo = _split2(pc)
                    later = _dot(jnp.concatenate([hi, lo], axis=1), tri2)
                    parts[c] = s[:, c * rg:(c + 1) * rg] - later - jnp.concatenate([carry] * (rg // LANES), 1)
                    carry = carry + jnp.sum(pc, axis=-1, keepdims=True)
                a = jnp.exp2(parts[0] if len(parts) == 1 else jnp.concatenate(parts, axis=1))
                if mask is not None:
                    a = jnp.where(mask, a, 0.0)
                pr = a.astype(BF16)
                ms.append(carry)
            else:
                if kind == "fox":
                    s = s + brow[:, :kw]
                if mask is not None:
                    s = jnp.where(mask, s, NEG_INF)
                chunks = [s[:, c * LANES:(c + 1) * LANES] for c in range(kw // LANES)]
                m_cur = jnp.max(functools.reduce(jnp.maximum, chunks), axis=-1, keepdims=True)
                m_new = jnp.maximum(m_prev, m_cur)
                als.append(jnp.exp2(m_prev - m_new))
                pr = jnp.concatenate([jnp.exp2(c - m_new).astype(BF16) for c in chunks], axis=1)
                ms.append(m_new)
            if kw < tq:
                pr = jnp.concatenate([pr, jnp.zeros((rg, tq - kw), BF16)], axis=1)
            prs.append(pr)
        m_ref[...] = jnp.concatenate(ms, axis=0)
        p_ref[slot] = jnp.concatenate(prs, axis=0)
        if kind != "sb":
            al_ref[slot] = jnp.concatenate(als, axis=0)

    def stage_c(j, slot):
        start = pl.multiple_of(key_block(j) * tq, tq)
        v = v_ref[pl.ds(start, tq), :]
        if kind == "sb":
            acc_ref[...] += _dot(p_ref[slot], v)
        else:
            al = al_ref[slot]
            pv = _dot(p_ref[slot], jnp.concatenate([v, ones], axis=1))
            acc_ref[...] = jnp.concatenate([al, al], axis=1) * acc_ref[...] + pv

    def step(j, slot):
        stage_b(j, slot, False)
        stage_a(j + 1, 1 - slot)
        stage_c(j - 1, 1 - slot)

    if early:
        @pl.when(qb == 0)
        def _():
            def norm_body(i, mx):
                kt = kt_ref[0, i].astype(F32)
                return jnp.maximum(mx, jnp.max(jnp.sum(kt * kt, axis=0, keepdims=True), axis=1, keepdims=True))

            mx = lax.fori_loop(0, kt_ref.shape[1], norm_body, jnp.zeros((1, 1), F32))
            kmax_ref[...] = jnp.broadcast_to(jnp.sqrt(mx), kmax_ref.shape)

        qf = q.astype(F32)
        zb = jnp.sqrt(jnp.sum(qf * qf, axis=1, keepdims=True)) * kmax_ref[:1, :]

    def exhausted(j):
        if kind == "sb":
            return jnp.min(m_ref[...] - zb) > UNDERFLOW_BITS
        brow_max = jnp.max((fref - f_ref[0, key_block(j + 1)]) * LOG2E)
        return jnp.min(m_ref[...] - zb) - brow_max > UNDERFLOW_BITS

    m_ref[...] = jnp.full(m_ref.shape, 0.0 if kind == "sb" else NEG_INF, F32)
    acc_ref[...] = jnp.zeros_like(acc_ref)
    stage_a(0, 0)
    stage_b(0, 0, True)
    stage_a(1, 1)

    if early:
        def cond(c):
            return jnp.logical_and(c[0] < qb // 2, jnp.logical_not(c[1]))

        def pair(c):
            t = c[0]
            step(1 + 2 * t, 1)
            step(2 + 2 * t, 0)
            return t + 1, exhausted(2 + 2 * t)

        pairs, done = lax.while_loop(cond, pair, (jnp.int32(0), exhausted(0)))
        last = 2 * pairs
        tail = jnp.logical_and(jnp.logical_not(done), last != qb)
    else:
        def body(t, c):
            step(1 + 2 * t, 1)
            step(2 + 2 * t, 0)
            return c

        lax.fori_loop(0, qb // 2, body, 0)
        last = 2 * (qb // 2)
        tail = last != qb

    @pl.when(tail)
    def _():
        step(qb, 1)
        stage_c(qb, 1)

    @pl.when(jnp.logical_not(tail))
    def _():
        stage_c(last, 0)

    if kind == "sb":
        o_ref[...] = acc_ref[...]
    else:
        o_ref[...] = acc_ref[:, :HEAD_DIM] / acc_ref[:, HEAD_DIM:]


def attn_prompt(kind, q, kt, v, f=None):
    t = q.shape[0]
    dq = q.shape[1] // N_HEADS
    tq = kt.shape[3]
    nq = t // tq
    in_specs = [pl.BlockSpec((tq, dq), lambda h, i: (i, h)),
                pl.BlockSpec((1, nq, dq, tq), lambda h, i: (h, 0, 0, 0)),
                pl.BlockSpec((t, HEAD_DIM), lambda h, i: (0, h))]
    args = [q, kt, v]
    if kind == "fox":
        in_specs.append(pl.BlockSpec((1, nq, 1, tq), lambda h, i: (h, 0, 0, 0)))
        args.append(f.reshape(N_HEADS, nq, 1, tq))
    acc_w = HEAD_DIM if kind == "sb" else 2 * HEAD_DIM
    scratch = [pltpu.VMEM((tq, LANES), F32), pltpu.VMEM((tq, acc_w), F32),
               pltpu.VMEM((2, tq, tq), F32), pltpu.VMEM((2, tq, tq), BF16)]
    if kind != "sb":
        scratch.append(pltpu.VMEM((2, tq, LANES), F32))
    if kind != "mla":
        scratch.append(pltpu.VMEM((8, LANES), F32))
    return pl.pallas_call(
        functools.partial(_attn_prompt_kernel, kind, tq),
        grid=(N_HEADS, nq),
        in_specs=in_specs,
        out_specs=pl.BlockSpec((tq, HEAD_DIM), lambda h, i: (i, h)),
        out_shape=jax.ShapeDtypeStruct((t, W_GROUP), F32),
        scratch_shapes=scratch,
        compiler_params=_params("parallel", "arbitrary"),
        name="attn_prompt_" + kind,
    )(*args)


def _softmax_tile(s, v, m_prev, l_prev, acc_prev):
    m_new = jnp.maximum(m_prev, jnp.max(s, axis=-1, keepdims=True))
    alpha = jnp.exp2(m_prev - m_new)
    pr = jnp.exp2(s - m_new)
    l_new = alpha * l_prev + jnp.sum(pr, axis=-1, keepdims=True)
    acc_new = alpha * acc_prev + _dot(pr.astype(BF16), v)
    return m_new, l_new, acc_new


def _attn_decode_kernel(kind, *refs):
    if kind == "fox":
        q_ref, kn_ref, vn_ref, kc_ref, vc_ref, fn_ref, fc_ref, o_ref = refs
    else:
        q_ref, kn_ref, vn_ref, kc_ref, vc_ref, o_ref = refs
    tq = q_ref.shape[0]
    dq = q_ref.shape[1] // N_HEADS
    past = kc_ref.shape[0]
    qpos = past + lax.broadcasted_iota(jnp.int32, (tq, tq), 0)
    kpos = past + lax.broadcasted_iota(jnp.int32, (tq, tq), 1)
    mask = _tile_mask(kind, qpos, kpos)
    for h in range(N_HEADS):
        q = q_ref[:, h * dq:(h + 1) * dq]
        kn = kn_ref[:, h * dq:(h + 1) * dq]
        vn = vn_ref[:, h * HEAD_DIM:(h + 1) * HEAD_DIM]
        if len(kc_ref.shape) == 3:
            kc = kc_ref[:, h, :].astype(BF16)
            vc = vc_ref[:, h, :].astype(BF16)
        else:
            kc = kc_ref[:, h * dq:(h + 1) * dq]
            vc = vc_ref[:, h * HEAD_DIM:(h + 1) * HEAD_DIM]
        s_n = _dot_nt(q, kn)
        s_c = _dot_nt(q, kc)
        if kind == "sb":
            a_n, carry = _sb_weights(s_n, jnp.zeros((tq, 1), F32), mask, _tri_lower(tq))
            a_c, _ = _sb_weights(s_c, carry, None, _tri_lower(min(TRI, past)))
            out = _dot(a_n.astype(BF16), vn) + _dot(a_c.astype(BF16), vc)
        else:
            if kind == "fox":
                fref = fn_ref[h][:, :1]
                s_n = s_n + (fref - fn_ref[h]) * LOG2E
                s_c = s_c + (fref - fc_ref[h]) * LOG2E
            s_n = jnp.where(mask, s_n, NEG_INF)
            m0 = jnp.full((tq, 1), NEG_INF, F32)
            z0 = jnp.zeros((tq, 1), F32)
            m, l, acc = _softmax_tile(s_n, vn, m0, z0, jnp.zeros((tq, HEAD_DIM), F32))
            m, l, acc = _softmax_tile(s_c, vc, m, l, acc)
            out = acc / l
        o_ref[:, h * HEAD_DIM:(h + 1) * HEAD_DIM] = out


def attn_decode(kind, q, kn, vn, kc, vc, layer=None, fn=None, fc=None):
    native = layer is not None
    nb, past = (kc.shape[1], kc.shape[2]) if native else (kc.shape[0], kc.shape[1])
    tq = q.shape[0] // nb
    new = lambda a: pl.BlockSpec((tq, a.shape[1]), lambda b: (b, 0))
    if native:
        old = lambda a: pl.BlockSpec((None, None, past, N_HEADS, HEAD_DIM), lambda b: (layer, b, 0, 0, 0))
    else:
        old = lambda a: pl.BlockSpec((None, past, a.shape[2]), lambda b: (b, 0, 0))
    in_specs = [new(q), new(kn), new(vn), old(kc), old(vc)]
    args = [q, kn, vn, kc, vc]
    if kind == "fox":
        in_specs += [pl.BlockSpec((None, N_HEADS, 1, tq), lambda b: (b, 0, 0, 0)),
                     pl.BlockSpec((None, N_HEADS, 1, past), lambda b: (b, 0, 0, 0))]
        args += [fn, fc]
    return pl.pallas_call(
        functools.partial(_attn_decode_kernel, kind),
        grid=(nb,),
        in_specs=in_specs,
        out_specs=pl.BlockSpec((tq, W_GROUP), lambda b: (b, 0)),
        out_shape=jax.ShapeDtypeStruct((nb * tq, W_GROUP), F32),
        compiler_params=_params("parallel"),
        name="attn_decode_" + kind,
    )(*args)


def _out_proj_kernel(oa_ref, ob_ref, oc_ref, ga_ref, gb_ref, gc_ref, w_ref, x_ref, o_ref):
    acc = x_ref[...]
    for g, (o, gn) in enumerate(((oa_ref, ga_ref), (ob_ref, gb_ref), (oc_ref, gc_ref))):
        y = _rms(o[...], gn[...]).astype(BF16)
        acc = acc + _dot(y, w_ref[g * W_GROUP:(g + 1) * W_GROUP, :])
    o_ref[...] = acc


def out_proj(oa, ob, oc, p, x, l):
    m = x.shape[0]
    tm = _pick_tile(m, 512)
    row = lambda w: pl.BlockSpec((tm, w), lambda i: (i, 0))
    full = lambda a: pl.BlockSpec(a.shape, lambda i: (0,) * a.ndim)
    consts = (p["out_norm_a"], p["out_norm_b"], p["out_norm_c"], p["w_out"])
    w_spec = pl.BlockSpec((None,) + p["w_out"].shape[1:], lambda i: (l, 0, 0))
    return pl.pallas_call(
        _out_proj_kernel,
        grid=(m // tm,),
        in_specs=[row(W_GROUP)] * 3 + [full(a) for a in consts[:3]] + [w_spec, row(D_MODEL)],
        out_specs=row(D_MODEL),
        out_shape=jax.ShapeDtypeStruct((m, D_MODEL), F32),
        compiler_params=_params("parallel"),
        name="out_proj",
    )(oa, ob, oc, *consts, x)


def _ffn_kernel(x_ref, g_ref, wg_ref, wu_ref, wd_ref, o_ref, h_ref):
    @pl.when(pl.program_id(1) == 0)
    def _():
        x = x_ref[...]
        h_ref[...] = _rms(x, g_ref[...]).astype(BF16)
        o_ref[...] = x

    h = h_ref[...]
    gate = _dot(h, wg_ref[...])
    up = _dot(h, wu_ref[...])
    act = (gate * jax.nn.sigmoid(gate) * up).astype(BF16)
    o_ref[...] += _dot(act, wd_ref[...])


def ffn(x, g, w_gu, w_down, l):
    m = x.shape[0]
    tm = _pick_tile(m, 512)
    tf = 512
    nf = D_FF // tf
    return pl.pallas_call(
        _ffn_kernel,
        grid=(m // tm, nf),
        in_specs=[pl.BlockSpec((tm, D_MODEL), lambda i, j: (i, 0)),
                  pl.BlockSpec((1, D_MODEL), lambda i, j: (0, 0)),
                  pl.BlockSpec((None, D_MODEL, tf), lambda i, j: (l, 0, j)),
                  pl.BlockSpec((None, D_MODEL, tf), lambda i, j: (l, 0, j + nf)),
                  pl.BlockSpec((None, tf, D_MODEL), lambda i, j: (l, j, 0))],
        out_specs=pl.BlockSpec((tm, D_MODEL), lambda i, j: (i, 0)),
        out_shape=jax.ShapeDtypeStruct((m, D_MODEL), F32),
        scratch_shapes=[pltpu.VMEM((tm, D_MODEL), BF16)],
        compiler_params=_params("parallel", "arbitrary"),
        name="ffn",
    )(x, g, w_gu, w_gu, w_down)


def _dup_rope(r):
    half = MLA_ROPE_DIM // 2
    return jnp.concatenate([r, r[..., half:], r[..., :half]], axis=-1)


def _prep_layer(l, fox_f_bias, fox_q_norm, fox_k_norm, mla_qa_norm, mla_w_uq, mla_kva_norm, mla_w_ukv,
                mla_q_norm, mla_k_norm, out_norm_a, out_norm_b, out_norm_c, w_in, w_out, w_gu, w_down,
                norm_mix, norm_ffn):
    uq = mla_w_uq[l].reshape(MLA_Q_RANK, N_HEADS, MLA_QK_DIM)
    uq = jnp.concatenate([uq[..., :MLA_NOPE_DIM], _dup_rope(uq[..., MLA_NOPE_DIM:])], axis=-1)
    ukv = mla_w_ukv[l].reshape(MLA_KV_RANK, N_HEADS, 2 * LANES)
    ukv = jnp.concatenate([ukv[..., :LANES].reshape(MLA_KV_RANK, W_GROUP),
                           ukv[..., LANES:].reshape(MLA_KV_RANK, W_GROUP)], axis=1)

    def gain256(g):
        return jnp.concatenate([g[:MLA_NOPE_DIM], _dup_rope(g[MLA_NOPE_DIM:])])[None, :]

    row = lambda a: a[l][None, :]
    return {
        "norm_mix": row(norm_mix), "w_in": w_in,
        "fox_q_norm": row(fox_q_norm), "fox_k_norm": row(fox_k_norm),
        "fox_f_bias": jnp.pad(fox_f_bias[l], (0, LANES - N_HEADS))[None, :],
        "mla_qa_norm": row(mla_qa_norm),
        "mla_w_uq": uq.reshape(MLA_Q_RANK, N_HEADS * MLA_HEAD_PAD).astype(BF16),
        "mla_q_gain": gain256(mla_q_norm[l]) * (MLA_QK_DIM ** -0.5 * LOG2E),
        "mla_kva_norm": row(mla_kva_norm),
        "mla_w_ukv": ukv.astype(BF16),
        "mla_k_gain": gain256(mla_k_norm[l]),
        "out_norm_a": row(out_norm_a), "out_norm_b": row(out_norm_b), "out_norm_c": row(out_norm_c),
        "w_out": w_out,
        "norm_ffn": row(norm_ffn), "w_gu": w_gu, "w_down": w_down,
    }


def _rope_tables(pos):
    half = MLA_ROPE_DIM // 2
    inv_freq = ROPE_THETA ** (-(jnp.arange(half, dtype=F32) / half))
    ang = pos.astype(F32)[:, None] * inv_freq[None, :]
    cos, sin = jnp.cos(ang), jnp.sin(ang)
    zero = jnp.zeros_like(cos)
    return (jnp.concatenate([cos, cos, zero, zero], axis=1),
            jnp.concatenate([-sin, sin, zero, zero], axis=1))


def _layer(x, caches, l, depth, p, tabs, stacks):
    nb, t, _ = x.shape
    m = nb * t
    x2 = x.reshape(m, D_MODEL)
    proj = rms_matmul(x2, p["norm_mix"], p["w_in"], l)
    prompt = caches is None
    outs = post(proj, p, *tabs["q"], with_kt=prompt, l=l, depth=depth, stacks=stacks)
    stacks = outs[:5]
    qa, vab, lf, qb, vbb, qc, krd, k2a, k2b = outs[5:]
    if prompt:
        kct, vc_new = kexp(stacks[4], krd, p, *tabs["q"], transposed=True, layer=l)
        f = cumsum_rows(lf.reshape(nb, t, LANES))
        oa = attn_prompt("fox", qa, k2a, vab, f[0, :N_HEADS])
        ob = attn_prompt("sb", qb, k2b, vbb)
        oc = attn_prompt("mla", qc, kct, vc_new)
    else:
        c_fk, c_fv, c_lf, c_sk, c_sv, c_ckv, c_kr = caches
        pl_ = c_fk.shape[2]
        kc_new, vc_new = kexp(stacks[4], krd, p, *tabs["q"], transposed=False, layer=l)
        lf_all = jnp.concatenate(
            [jnp.pad(c_lf[l], ((0, 0), (0, 0), (0, LANES - N_HEADS))), lf.reshape(nb, t, LANES)], axis=1)
        n_pad = -(-(pl_ + t) // LANES) * LANES
        lf_all = jnp.pad(lf_all, ((0, 0), (0, n_pad - pl_ - t), (0, 0)))
        f = cumsum_rows(lf_all)[:, :N_HEADS, None, :]
        kc_old, vc_old = kexp(c_ckv.reshape(c_ckv.shape[0], nb * pl_, MLA_KV_RANK),
                              _dup_rope(c_kr[l]).reshape(nb * pl_, LANES), p, *tabs["kc"],
                              transposed=False, layer=l)
        oa = attn_decode("fox", qa, k2a, vab, c_fk, c_fv, layer=l, fn=f[..., pl_:pl_ + t], fc=f[..., :pl_])
        ob = attn_decode("sb", qb, k2b, vbb, c_sk, c_sv, layer=l)
        oc = attn_decode("mla", qc, kc_new, vc_new, kc_old.reshape(nb, pl_, -1), vc_old.reshape(nb, pl_, -1))
    x2 = out_proj(oa, ob, oc, p, x2, l)
    x2 = ffn(x2, p["norm_ffn"], p["w_gu"], p["w_down"], l)
    small = (lf[:, :N_HEADS].reshape(nb, t, N_HEADS), krd[:, :MLA_ROPE_DIM].reshape(nb, t, MLA_ROPE_DIM))
    return x2.reshape(nb, t, D_MODEL), stacks, small


def kernel(x_prompt, x_sample, cache_fox_k, cache_fox_v, cache_fox_logf, cache_sb_k, cache_sb_v, cache_mla_ckv, cache_mla_krope, norm_mix, w_in, fox_f_bias, fox_q_norm, fox_k_norm, mla_qa_norm, mla_w_uq, mla_kva_norm, mla_w_ukv, mla_q_norm, mla_k_norm, out_norm_a, out_norm_b, out_norm_c, w_out, norm_ffn, w_gu, w_down):
    depth = w_in.shape[0]
    t_p = x_prompt.shape[1]
    nb_s, t_s = x_sample.shape[0], x_sample.shape[1]
    past_len = cache_fox_k.shape[2]
    tabs_p = {"q": _rope_tables(jnp.arange(t_p, dtype=jnp.int32))}
    tabs_s = {"q": _rope_tables(jnp.tile(past_len + jnp.arange(t_s, dtype=jnp.int32), nb_s)),
              "kc": _rope_tables(jnp.tile(jnp.arange(past_len, dtype=jnp.int32), nb_s))}
    caches = (cache_fox_k, cache_fox_v, cache_fox_logf, cache_sb_k, cache_sb_v, cache_mla_ckv, cache_mla_krope)
    y_p, y_s = x_prompt, x_sample
    rows_p, rows_s = [], []
    stacks_p = stacks_s = None
    w_in_b = w_in_prep(w_in)
    w_out_b, w_gu_b, w_down_b = w_out.astype(BF16), w_gu.astype(BF16), w_down.astype(BF16)
    for l in range(depth):
        p = _prep_layer(l, fox_f_bias, fox_q_norm, fox_k_norm, mla_qa_norm, mla_w_uq, mla_kva_norm, mla_w_ukv,
                        mla_q_norm, mla_k_norm, out_norm_a, out_norm_b, out_norm_c, w_in_b, w_out_b, w_gu_b,
                        w_down_b, norm_mix, norm_ffn)
        y_p, stacks_p, r_p = _layer(y_p, None, l, depth, p, tabs_p, stacks_p)
        y_s, stacks_s, r_s = _layer(y_s, caches, l, depth, p, tabs_s, stacks_s)
        rows_p.append(r_p)
        rows_s.append(r_s)

    def assemble(stacks, rows, nb, t):
        fk, fv, sk, sv, ckv = stacks
        heads = lambda a: a.reshape(depth, nb, t, N_HEADS, HEAD_DIM)
        return (heads(fk), heads(fv), jnp.stack([r[0] for r in rows], axis=0), heads(sk), heads(sv),
                ckv.reshape(depth, nb, t, MLA_KV_RANK), jnp.stack([r[1] for r in rows], axis=0))

    return ((y_p, y_s) + assemble(stacks_p, rows_p, x_prompt.shape[0], t_p)
            + assemble(stacks_s, rows_s, nb_s, t_s))
```

```python
import functools
import math

import jax
import jax.numpy as jnp
from jax import lax
from jax.experimental import pallas as pl
from jax.experimental.pallas import tpu as pltpu

D_MODEL = 2048
CHUNK = 64
HEAD_DIM = 128
N_HEADS = 4
W_GROUP = N_HEADS * HEAD_DIM
MLA_Q_RANK = 512
MLA_KV_RANK = 256
MLA_NOPE_DIM = 128
MLA_ROPE_DIM = 64
MLA_QK_DIM = MLA_NOPE_DIM + MLA_ROPE_DIM
MLA_HEAD_PAD = 256
ROPE_THETA = 10000.0
D_FF = 5632
RMS_EPS = 1e-6
NEG_INF = -1e30
LOG2E = math.log2(math.e)
N_IN_PAD = 4096
LANES = 128
TRI = 256
ATTN_TILE = 512
ROW_GROUP = 128
UNDERFLOW_BITS = 160.0
VMEM_LIMIT = 56 * 1024 * 1024

C_QA, C_KA, C_VA, C_QB, C_KB, C_VB, C_CQ, C_CKV, C_KR, C_FA = (
    0, 512, 1024, 1536, 2048, 2560, 3072, 3584, 3840, 3968)

BF16 = jnp.bfloat16
F32 = jnp.float32


def _params(*sem):
    return pltpu.CompilerParams(dimension_semantics=sem, vmem_limit_bytes=VMEM_LIMIT)


def _pick_tile(n, pref):
    t = min(n, pref)
    while n % t:
        t //= 2
    return t


def _rms(x, g):
    return x * lax.rsqrt(jnp.mean(x * x, axis=-1, keepdims=True) + RMS_EPS) * g


def _dot(a, b):
    return jnp.dot(a, b, preferred_element_type=F32)


def _dot_nt(a, b):
    return lax.dot_general(a, b, (((1,), (1,)), ((), ())), preferred_element_type=F32)


def _split2(x):
    hi = x.astype(BF16)
    lo = (x - hi.astype(F32)).astype(BF16)
    return hi, lo


def _neg_abs(x):
    bits = lax.bitcast_convert_type(x, jnp.uint32) | jnp.uint32(0x80000000)
    return lax.bitcast_convert_type(bits, F32)


def _split3(x):
    a = x.astype(BF16)
    r = x - a.astype(F32)
    b = r.astype(BF16)
    c = (r - b.astype(F32)).astype(BF16)
    return a, b, c


def _log_sigmoid(x):
    return -(jnp.maximum(-x, 0.0) + jnp.log1p(jnp.exp(-jnp.abs(x))))


def _log2_keep(z2):
    return -(jnp.maximum(z2, 0.0) + jnp.log2(1.0 + jnp.exp2(_neg_abs(z2))))


def _w_in_prep_kernel(w_ref, o_ref):
    rows = w_ref.shape[0]
    fa0 = C_QB
    o_ref[:, :C_QB] = w_ref[:, :fa0].astype(BF16)
    o_ref[:, C_QB:C_KR] = w_ref[:, fa0 + N_HEADS:C_KR + N_HEADS].astype(BF16)
    kr = w_ref[:, C_KR + N_HEADS:C_KR + N_HEADS + MLA_ROPE_DIM]
    half = MLA_ROPE_DIM // 2
    o_ref[:, C_KR:C_FA] = jnp.concatenate([kr, kr[:, half:], kr[:, :half]], axis=1).astype(BF16)
    fa = w_ref[:, fa0:fa0 + N_HEADS]
    o_ref[:, C_FA:] = jnp.concatenate([fa, jnp.zeros((rows, LANES - N_HEADS), F32)], axis=1).astype(BF16)


def w_in_prep(w_in):
    depth, d, n = w_in.shape
    tr = 256
    return pl.pallas_call(
        _w_in_prep_kernel,
        grid=(depth, d // tr),
        in_specs=[pl.BlockSpec((None, tr, n), lambda l, i: (l, i, 0))],
        out_specs=pl.BlockSpec((None, tr, N_IN_PAD), lambda l, i: (l, i, 0)),
        out_shape=jax.ShapeDtypeStruct((depth, d, N_IN_PAD), BF16),
        compiler_params=_params("parallel", "parallel"),
        name="w_in_prep",
    )(w_in)


def _rms_matmul_kernel(x_ref, g_ref, w_ref, o_ref, h_ref):
    @pl.when(pl.program_id(1) == 0)
    def _():
        h_ref[...] = _rms(x_ref[...], g_ref[...]).astype(BF16)

    o_ref[...] = _dot(h_ref[...], w_ref[...])


def rms_matmul(x, g, w, l):
    m, k = x.shape
    n = w.shape[2]
    tm = _pick_tile(m, 1024)
    tn = _pick_tile(n, 1024)
    return pl.pallas_call(
        _rms_matmul_kernel,
        grid=(m // tm, n // tn),
        in_specs=[pl.BlockSpec((tm, k), lambda i, j: (i, 0)),
                  pl.BlockSpec((1, k), lambda i, j: (0, 0)),
                  pl.BlockSpec((None, k, tn), lambda i, j: (l, 0, j))],
        out_specs=pl.BlockSpec((tm, tn), lambda i, j: (i, j)),
        out_shape=jax.ShapeDtypeStruct((m, n), F32),
        scratch_shapes=[pltpu.VMEM((tm, k), BF16)],
        compiler_params=_params("parallel", "arbitrary"),
        name="rms_matmul",
    )(x, g, w)


def _rope_chunk(y2, cos, sin):
    return y2 * cos + pltpu.roll(y2, 64, 1) * sin


def _rope_lane_mask():
    return lax.broadcasted_iota(jnp.int32, (1, LANES), 1) < MLA_ROPE_DIM


def _post_kernel(with_kt, n_alias, proj_ref, fqg_ref, fkg_ref, fb_ref, qag_ref, wuq_ref, qg_ref, kvg_ref, cos_ref,
                 sin_ref, *refs):
    refs = refs[n_alias:]
    ka_o, va_o, kb_o, vb_o, ckv_o, qa_o, vab_o, lf_o, qb_o, vbb_o, qc_o, kr_o, k2a_o, k2b_o = refs
    scale = HEAD_DIM ** -0.5 * LOG2E
    va = proj_ref[:, C_VA:C_VA + W_GROUP]
    vb = proj_ref[:, C_VB:C_VB + W_GROUP]
    for h in range(N_HEADS):
        sl = slice(h * HEAD_DIM, (h + 1) * HEAD_DIM)
        qa = proj_ref[:, C_QA + h * HEAD_DIM:C_QA + (h + 1) * HEAD_DIM]
        qa_o[:, sl] = (_rms(qa, fqg_ref[...]) * scale).astype(BF16)
        ka = _rms(proj_ref[:, C_KA + h * HEAD_DIM:C_KA + (h + 1) * HEAD_DIM], fkg_ref[...])
        kb = proj_ref[:, C_KB + h * HEAD_DIM:C_KB + (h + 1) * HEAD_DIM]
        ka_o[:, h, :] = ka
        kb_o[:, h, :] = kb
        va_o[:, h, :] = va[:, sl]
        vb_o[:, h, :] = vb[:, sl]
        if with_kt:
            k2a_o[h, 0] = ka.T.astype(BF16)
            k2b_o[h, 0] = kb.T.astype(BF16)
        else:
            k2a_o[:, sl] = ka.astype(BF16)
            k2b_o[:, sl] = kb.astype(BF16)
    vab_o[...] = va.astype(BF16)
    qb_o[...] = (proj_ref[:, C_QB:C_QB + W_GROUP] * scale).astype(BF16)
    vbb_o[...] = vb.astype(BF16)

    lane = lax.broadcasted_iota(jnp.int32, (1, LANES), 1)
    lf = _log_sigmoid(proj_ref[:, C_FA:C_FA + LANES] + fb_ref[...])
    lf_o[...] = jnp.where(lane < N_HEADS, lf, 0.0)

    cqn = _rms(proj_ref[:, C_CQ:C_CQ + MLA_Q_RANK], qag_ref[...]).astype(BF16)
    qc = _dot(cqn, wuq_ref[...])
    rmask = _rope_lane_mask()
    cos = cos_ref[...]
    sin = sin_ref[...]
    for h in range(N_HEADS):
        c1 = qc[:, h * MLA_HEAD_PAD:h * MLA_HEAD_PAD + LANES]
        c2 = qc[:, h * MLA_HEAD_PAD + LANES:(h + 1) * MLA_HEAD_PAD]
        ss = (jnp.sum(c1 * c1, axis=-1, keepdims=True)
              + jnp.sum(jnp.where(rmask, c2 * c2, 0.0), axis=-1, keepdims=True))
        rs = lax.rsqrt(ss * (1.0 / MLA_QK_DIM) + RMS_EPS)
        y1 = c1 * rs * qg_ref[:, :LANES]
        y2 = c2 * rs * qg_ref[:, LANES:]
        qc_o[:, h * MLA_HEAD_PAD:h * MLA_HEAD_PAD + LANES] = y1.astype(BF16)
        qc_o[:, h * MLA_HEAD_PAD + LANES:(h + 1) * MLA_HEAD_PAD] = _rope_chunk(y2, cos, sin).astype(BF16)

    ckv_o[...] = _rms(proj_ref[:, C_CKV:C_CKV + MLA_KV_RANK], kvg_ref[...])
    kr_o[...] = proj_ref[:, C_KR:C_KR + LANES]


def post(proj, p, cos, sin, with_kt, l, depth, stacks):
    m = proj.shape[0]
    tm = _pick_tile(m, ATTN_TILE)
    row = lambda w: pl.BlockSpec((tm, w), lambda i: (i, 0))
    full = lambda a: pl.BlockSpec(a.shape, lambda i: (0,) * a.ndim)
    consts = (p["fox_q_norm"], p["fox_k_norm"], p["fox_f_bias"], p["mla_qa_norm"], p["mla_w_uq"],
              p["mla_q_gain"], p["mla_kva_norm"])
    heads_spec = pl.BlockSpec((None, tm, N_HEADS, HEAD_DIM), lambda i: (l, i, 0, 0))
    heads_shape = jax.ShapeDtypeStruct((depth, m, N_HEADS, HEAD_DIM), F32)
    out_specs = [heads_spec] * 4 + [pl.BlockSpec((None, tm, MLA_KV_RANK), lambda i: (l, i, 0))]
    out_shape = [heads_shape] * 4 + [jax.ShapeDtypeStruct((depth, m, MLA_KV_RANK), F32)]
    outs = [(W_GROUP, BF16), (W_GROUP, BF16), (LANES, F32), (W_GROUP, BF16), (W_GROUP, BF16),
            (N_HEADS * MLA_HEAD_PAD, BF16), (LANES, F32)]
    out_specs += [row(w) for w, _ in outs]
    out_shape += [jax.ShapeDtypeStruct((m, w), dt) for w, dt in outs]
    for _ in range(2):
        if with_kt:
            out_specs.append(pl.BlockSpec((N_HEADS, 1, HEAD_DIM, tm), lambda i: (0, i, 0, 0)))
            out_shape.append(jax.ShapeDtypeStruct((N_HEADS, m // tm, HEAD_DIM, tm), BF16))
        else:
            out_specs.append(row(W_GROUP))
            out_shape.append(jax.ShapeDtypeStruct((m, W_GROUP), BF16))
    n_in = 1 + len(consts) + 2
    stacks = () if stacks is None else tuple(stacks)
    return pl.pallas_call(
        functools.partial(_post_kernel, with_kt, len(stacks)),
        grid=(m // tm,),
        in_specs=([row(N_IN_PAD)] + [full(a) for a in consts] + [row(LANES), row(LANES)]
                  + [pl.BlockSpec(memory_space=pl.ANY)] * len(stacks)),
        out_specs=out_specs,
        out_shape=out_shape,
        input_output_aliases={n_in + k: k for k in range(len(stacks))},
        compiler_params=_params("parallel"),
        name="post",
    )(proj, *consts, cos, sin, *stacks)


def _kexp_kernel(transposed, ckv_ref, kr_ref, w_ref, kg_ref, cos_ref, sin_ref, kc_o, vc_o):
    kv = _dot(ckv_ref[...].astype(BF16), w_ref[...])
    kr = kr_ref[...]
    krsq = jnp.sum(jnp.where(_rope_lane_mask(), kr * kr, 0.0), axis=-1, keepdims=True)
    cos = cos_ref[...]
    sin = sin_ref[...]
    for h in range(N_HEADS):
        n = kv[:, h * LANES:(h + 1) * LANES]
        ss = jnp.sum(n * n, axis=-1, keepdims=True) + krsq
        rs = lax.rsqrt(ss * (1.0 / MLA_QK_DIM) + RMS_EPS)
        k1 = n * rs * kg_ref[:, :LANES]
        k2 = _rope_chunk(kr * rs * kg_ref[:, LANES:], cos, sin)
        if transposed:
            kc_o[h, 0, :LANES, :] = k1.T.astype(BF16)
            kc_o[h, 0, LANES:, :] = k2.T.astype(BF16)
        else:
            kc_o[:, h * MLA_HEAD_PAD:h * MLA_HEAD_PAD + LANES] = k1.astype(BF16)
            kc_o[:, h * MLA_HEAD_PAD + LANES:(h + 1) * MLA_HEAD_PAD] = k2.astype(BF16)
    vc_o[...] = kv[:, W_GROUP:].astype(BF16)


def kexp(ckv, krdup, p, cos, sin, transposed, layer=None):
    m = krdup.shape[0]
    tm = _pick_tile(m, ATTN_TILE)
    row = lambda w: pl.BlockSpec((tm, w), lambda i: (i, 0))
    full = lambda a: pl.BlockSpec(a.shape, lambda i: (0,) * a.ndim)
    ckv_spec = row(MLA_KV_RANK) if layer is None else pl.BlockSpec(
        (None, tm, MLA_KV_RANK), lambda i: (layer, i, 0))
    if transposed:
        kc_spec = pl.BlockSpec((N_HEADS, 1, MLA_HEAD_PAD, tm), lambda i: (0, i, 0, 0))
        kc_shape = jax.ShapeDtypeStruct((N_HEADS, m // tm, MLA_HEAD_PAD, tm), BF16)
    else:
        kc_spec = row(N_HEADS * MLA_HEAD_PAD)
        kc_shape = jax.ShapeDtypeStruct((m, N_HEADS * MLA_HEAD_PAD), BF16)
    return pl.pallas_call(
        functools.partial(_kexp_kernel, transposed),
        grid=(m // tm,),
        in_specs=[ckv_spec, row(LANES), full(p["mla_w_ukv"]), full(p["mla_k_gain"]), row(LANES), row(LANES)],
        out_specs=[kc_spec, row(W_GROUP)],
        out_shape=[kc_shape, jax.ShapeDtypeStruct((m, W_GROUP), BF16)],
        compiler_params=_params("parallel"),
        name="kexp",
    )(ckv, krdup, p["mla_w_ukv"], p["mla_k_gain"], cos, sin)


def _cumsum_kernel(x_ref, o_ref, carry_ref):
    @pl.when(pl.program_id(1) == 0)
    def _():
        carry_ref[...] = jnp.zeros_like(carry_ref)

    t = x_ref.shape[1]
    xt = x_ref[0].T[:8, :]
    ri = lax.broadcasted_iota(jnp.int32, (t, t), 0)
    ci = lax.broadcasted_iota(jnp.int32, (t, t), 1)
    upper = jnp.where(ri <= ci, 1.0, 0.0).astype(BF16)
    a, b, c = _split3(xt)
    f = _dot(a, upper) + _dot(b, upper) + _dot(c, upper) + carry_ref[:, :1]
    o_ref[0] = f
    carry_ref[...] = jnp.broadcast_to(f[:, t - 1:t], carry_ref.shape)


def cumsum_rows(x):
    b, n, _ = x.shape
    t = n if n <= 1280 else _pick_tile(n, 512)
    return pl.pallas_call(
        _cumsum_kernel,
        grid=(b, n // t),
        in_specs=[pl.BlockSpec((1, t, LANES), lambda i, j: (i, j, 0))],
        out_specs=pl.BlockSpec((1, 8, t), lambda i, j: (i, 0, j)),
        out_shape=jax.ShapeDtypeStruct((b, 8, n), F32),
        scratch_shapes=[pltpu.VMEM((8, LANES), F32)],
        compiler_params=_params("parallel", "arbitrary"),
        name="cumsum_rows",
    )(x)


def _tile_mask(kind, qpos, kpos):
    if kind == "fox":
        return kpos <= qpos
    if kind == "sb":
        return kpos < qpos
    shift = CHUNK.bit_length() - 1
    return lax.shift_right_logical(kpos, shift) <= lax.shift_right_logical(qpos, shift)


def _tri_lower(n):
    ri = lax.broadcasted_iota(jnp.int32, (n, n), 0)
    ci = lax.broadcasted_iota(jnp.int32, (n, n), 1)
    return jnp.where(ri >= ci, 1.0, 0.0).astype(BF16)


def _sb_weights(z, carry, mask, tri):
    lk = _log2_keep(z)
    if mask is not None:
        lk = jnp.where(mask, lk, 0.0)
    blk = tri.shape[0]
    nblk = z.shape[1] // blk
    parts = [None] * nblk
    for c in reversed(range(nblk)):
        lkc = lk[:, c * blk:(c + 1) * blk]
        hi, lo = _split2(lkc)
        intra = _dot(hi, tri) + _dot(lo, tri)
        if carry.shape[1] == 1 or carry.shape[1] == blk:
            parts[c] = intra + carry
        else:
            parts[c] = intra + jnp.concatenate([carry] * (blk // carry.shape[1]), axis=1)
        carry = carry + jnp.sum(lkc, axis=-1, keepdims=True)
    r = parts[0] if nblk == 1 else jnp.concatenate(parts, axis=-1)
    a = jnp.exp2(z + r)
    if mask is not None:
        a = jnp.where(mask, a, 0.0)
    return a, carry


def _attn_prompt_kernel(kind, tq, *refs):
    f_ref = al_ref = kmax_ref = None
    if kind == "fox":
        q_ref, kt_ref, v_ref, f_ref, o_ref, m_ref, acc_ref, s_ref, p_ref, al_ref, kmax_ref = refs
    elif kind == "mla":
        q_ref, kt_ref, v_ref, o_ref, m_ref, acc_ref, s_ref, p_ref, al_ref = refs
    else:
        q_ref, kt_ref, v_ref, o_ref, m_ref, acc_ref, s_ref, p_ref, kmax_ref = refs
    early = kind != "mla"
    qb = pl.program_id(1)
    rg = min(TRI if kind == "sb" else ROW_GROUP, tq)
    n_rg = tq // rg
    q = q_ref[...]
    ones = jnp.ones((tq, LANES), BF16)
    tri2 = jnp.concatenate([_tri_lower(rg)] * 2, axis=0) if kind == "sb" else None
    fref = f_ref[0, qb][:, :1] if kind == "fox" else None

    def key_block(j):
        return jnp.clip(qb - j, 0, qb)

    def stage_a(j, slot):
        s_ref[slot] = _dot(q, kt_ref[0, key_block(j)])

    def stage_b(j, slot, diagonal):
        brow = (fref - f_ref[0, key_block(j)]) * LOG2E if kind == "fox" else None
        ms, als, prs = [], [], []
        for r in range(n_rg):
            rows = slice(r * rg, (r + 1) * rg)
            kw = (r + 1) * rg if diagonal else tq
            s = s_ref[slot, rows, :kw]
            mask = None
            if diagonal:
                qpos = r * rg + lax.broadcasted_iota(jnp.int32, (rg, kw), 0)
                kpos = lax.broadcasted_iota(jnp.int32, (rg, kw), 1)
                mask = _tile_mask(kind, qpos, kpos)
            m_prev = m_ref[rows, :]
            if kind == "sb":
                plk = jnp.maximum(s, 0.0) + jnp.log2(1.0 + jnp.exp2(_neg_abs(s)))
                if mask is not None:
                    plk = jnp.where(mask, plk, 0.0)
                carry = m_prev
                parts = [None] * (kw // rg)
                for c in reversed(range(kw // rg)):
                    pc = plk[:, c * rg:(c + 1) * rg]
                    hi, lo = _split2(pc)
                    later = _dot(jnp.concatenate([hi, lo], axis=1), tri2)
                    parts[c] = s[:, c * rg:(c + 1) * rg] - later - jnp.concatenate([carry] * (rg // LANES), 1)
                    carry = carry + jnp.sum(pc, axis=-1, keepdims=True)
                a = jnp.exp2(parts[0] if len(parts) == 1 else jnp.concatenate(parts, axis=1))
                if mask is not None:
                    a = jnp.where(mask, a, 0.0)
                pr = a.astype(BF16)
                ms.append(carry)
            else:
                if kind == "fox":
                    s = s + brow[:, :kw]
                if mask is not None:
                    s = jnp.where(mask, s, NEG_INF)
                chunks = [s[:, c * LANES:(c + 1) * LANES] for c in range(kw // LANES)]
                m_cur = jnp.max(functools.reduce(jnp.maximum, chunks), axis=-1, keepdims=True)
                m_new = jnp.maximum(m_prev, m_cur)
                als.append(jnp.exp2(m_prev - m_new))
                pr = jnp.concatenate([jnp.exp2(c - m_new).astype(BF16) for c in chunks], axis=1)
                ms.append(m_new)
            if kw < tq:
                pr = jnp.concatenate([pr, jnp.zeros((rg, tq - kw), BF16)], axis=1)
            prs.append(pr)
        m_ref[...] = jnp.concatenate(ms, axis=0)
        p_ref[slot] = jnp.concatenate(prs, axis=0)
        if kind != "sb":
            al_ref[slot] = jnp.concatenate(als, axis=0)

    def stage_c(j, slot):
        start = pl.multiple_of(key_block(j) * tq, tq)
        v = v_ref[pl.ds(start, tq), :]
        if kind == "sb":
            acc_ref[...] += _dot(p_ref[slot], v)
        else:
            al = al_ref[slot]
            pv = _dot(p_ref[slot], jnp.concatenate([v, ones], axis=1))
            acc_ref[...] = jnp.concatenate([al, al], axis=1) * acc_ref[...] + pv

    def step(j, slot):
        stage_b(j, slot, False)
        stage_a(j + 1, 1 - slot)
        stage_c(j - 1, 1 - slot)

    if early:
        @pl.when(qb == 0)
        def _():
            def norm_body(i, mx):
                kt = kt_ref[0, i].astype(F32)
                return jnp.maximum(mx, jnp.max(jnp.sum(kt * kt, axis=0, keepdims=True), axis=1, keepdims=True))

            mx = lax.fori_loop(0, kt_ref.shape[1], norm_body, jnp.zeros((1, 1), F32))
            kmax_ref[...] = jnp.broadcast_to(jnp.sqrt(mx), kmax_ref.shape)

        qf = q.astype(F32)
        zb = jnp.sqrt(jnp.sum(qf * qf, axis=1, keepdims=True)) * kmax_ref[:1, :]

    def exhausted(j):
        if kind == "sb":
            return jnp.min(m_ref[...] - zb) > UNDERFLOW_BITS
        brow_max = jnp.max((fref - f_ref[0, key_block(j + 1)]) * LOG2E)
        return jnp.min(m_ref[...] - zb) - brow_max > UNDERFLOW_BITS

    m_ref[...] = jnp.full(m_ref.shape, 0.0 if kind == "sb" else NEG_INF, F32)
    acc_ref[...] = jnp.zeros_like(acc_ref)
    stage_a(0, 0)
    stage_b(0, 0, True)
    stage_a(1, 1)

    if kind == "sb":
        def cond1(c):
            return jnp.logical_and(c[0] <= qb, jnp.logical_not(c[1]))

        def single(c):
            j = c[0]
            step(j, j & 1)
            return j + 1, exhausted(j)

        nxt, _ = lax.while_loop(cond1, single, (jnp.int32(1), exhausted(0)))
        stage_c(nxt - 1, (nxt - 1) & 1)
        o_ref[...] = acc_ref[...]
        return

    if early:
        def cond(c):
            return jnp.logical_and(c[0] < qb // 2, jnp.logical_not(c[1]))

        def pair(c):
            t = c[0]
            step(1 + 2 * t, 1)
            step(2 + 2 * t, 0)
            return t + 1, exhausted(2 + 2 * t)

        pairs, done = lax.while_loop(cond, pair, (jnp.int32(0), exhausted(0)))
        last = 2 * pairs
        tail = jnp.logical_and(jnp.logical_not(done), last != qb)
    else:
        def body(t, c):
            step(1 + 2 * t, 1)
            step(2 + 2 * t, 0)
            return c

        lax.fori_loop(0, qb // 2, body, 0)
        last = 2 * (qb // 2)
        tail = last != qb

    @pl.when(tail)
    def _():
        step(qb, 1)
        stage_c(qb, 1)

    @pl.when(jnp.logical_not(tail))
    def _():
        stage_c(last, 0)

    o_ref[...] = acc_ref[:, :HEAD_DIM] / acc_ref[:, HEAD_DIM:]


def attn_prompt(kind, q, kt, v, f=None):
    t = q.shape[0]
    dq = q.shape[1] // N_HEADS
    tq = kt.shape[3]
    nq = t // tq
    in_specs = [pl.BlockSpec((tq, dq), lambda h, i: (i, h)),
                pl.BlockSpec((1, nq, dq, tq), lambda h, i: (h, 0, 0, 0)),
                pl.BlockSpec((t, HEAD_DIM), lambda h, i: (0, h))]
    args = [q, kt, v]
    if kind == "fox":
        in_specs.append(pl.BlockSpec((1, nq, 1, tq), lambda h, i: (h, 0, 0, 0)))
        args.append(f.reshape(N_HEADS, nq, 1, tq))
    acc_w = HEAD_DIM if kind == "sb" else 2 * HEAD_DIM
    scratch = [pltpu.VMEM((tq, LANES), F32), pltpu.VMEM((tq, acc_w), F32),
               pltpu.VMEM((2, tq, tq), F32), pltpu.VMEM((2, tq, tq), BF16)]
    if kind != "sb":
        scratch.append(pltpu.VMEM((2, tq, LANES), F32))
    if kind != "mla":
        scratch.append(pltpu.VMEM((8, LANES), F32))
    return pl.pallas_call(
        functools.partial(_attn_prompt_kernel, kind, tq),
        grid=(N_HEADS, nq),
        in_specs=in_specs,
        out_specs=pl.BlockSpec((tq, HEAD_DIM), lambda h, i: (i, h)),
        out_shape=jax.ShapeDtypeStruct((t, W_GROUP), F32),
        scratch_shapes=scratch,
        compiler_params=_params("parallel", "arbitrary"),
        name="attn_prompt_" + kind,
    )(*args)


def _softmax_tile(s, v, m_prev, l_prev, acc_prev):
    m_new = jnp.maximum(m_prev, jnp.max(s, axis=-1, keepdims=True))
    alpha = jnp.exp2(m_prev - m_new)
    pr = jnp.exp2(s - m_new)
    l_new = alpha * l_prev + jnp.sum(pr, axis=-1, keepdims=True)
    acc_new = alpha * acc_prev + _dot(pr.astype(BF16), v)
    return m_new, l_new, acc_new


def _attn_decode_kernel(kind, *refs):
    if kind == "fox":
        q_ref, kn_ref, vn_ref, kc_ref, vc_ref, fn_ref, fc_ref, o_ref = refs
    else:
        q_ref, kn_ref, vn_ref, kc_ref, vc_ref, o_ref = refs
    tq = q_ref.shape[0]
    dq = q_ref.shape[1] // N_HEADS
    past = kc_ref.shape[0]
    qpos = past + lax.broadcasted_iota(jnp.int32, (tq, tq), 0)
    kpos = past + lax.broadcasted_iota(jnp.int32, (tq, tq), 1)
    mask = _tile_mask(kind, qpos, kpos)
    for h in range(N_HEADS):
        q = q_ref[:, h * dq:(h + 1) * dq]
        kn = kn_ref[:, h * dq:(h + 1) * dq]
        vn = vn_ref[:, h * HEAD_DIM:(h + 1) * HEAD_DIM]
        if len(kc_ref.shape) == 3:
            kc = kc_ref[:, h, :].astype(BF16)
            vc = vc_ref[:, h, :].astype(BF16)
        else:
            kc = kc_ref[:, h * dq:(h + 1) * dq]
            vc = vc_ref[:, h * HEAD_DIM:(h + 1) * HEAD_DIM]
        s_n = _dot_nt(q, kn)
        s_c = _dot_nt(q, kc)
        if kind == "sb":
            a_n, carry = _sb_weights(s_n, jnp.zeros((tq, 1), F32), mask, _tri_lower(tq))
            a_c, _ = _sb_weights(s_c, carry, None, _tri_lower(min(TRI, past)))
            out = _dot(a_n.astype(BF16), vn) + _dot(a_c.astype(BF16), vc)
        else:
            if kind == "fox":
                fref = fn_ref[h][:, :1]
                s_n = s_n + (fref - fn_ref[h]) * LOG2E
                s_c = s_c + (fref - fc_ref[h]) * LOG2E
            s_n = jnp.where(mask, s_n, NEG_INF)
            m0 = jnp.full((tq, 1), NEG_INF, F32)
            z0 = jnp.zeros((tq, 1), F32)
            m, l, acc = _softmax_tile(s_n, vn, m0, z0, jnp.zeros((tq, HEAD_DIM), F32))
            m, l, acc = _softmax_tile(s_c, vc, m, l, acc)
            out = acc / l
        o_ref[:, h * HEAD_DIM:(h + 1) * HEAD_DIM] = out


def attn_decode(kind, q, kn, vn, kc, vc, layer=None, fn=None, fc=None):
    native = layer is not None
    nb, past = (kc.shape[1], kc.shape[2]) if native else (kc.shape[0], kc.shape[1])
    tq = q.shape[0] // nb
    new = lambda a: pl.BlockSpec((tq, a.shape[1]), lambda b: (b, 0))
    if native:
        old = lambda a: pl.BlockSpec((None, None, past, N_HEADS, HEAD_DIM), lambda b: (layer, b, 0, 0, 0))
    else:
        old = lambda a: pl.BlockSpec((None, past, a.shape[2]), lambda b: (b, 0, 0))
    in_specs = [new(q), new(kn), new(vn), old(kc), old(vc)]
    args = [q, kn, vn, kc, vc]
    if kind == "fox":
        in_specs += [pl.BlockSpec((None, N_HEADS, 1, tq), lambda b: (b, 0, 0, 0)),
                     pl.BlockSpec((None, N_HEADS, 1, past), lambda b: (b, 0, 0, 0))]
        args += [fn, fc]
    return pl.pallas_call(
        functools.partial(_attn_decode_kernel, kind),
        grid=(nb,),
        in_specs=in_specs,
        out_specs=pl.BlockSpec((tq, W_GROUP), lambda b: (b, 0)),
        out_shape=jax.ShapeDtypeStruct((nb * tq, W_GROUP), F32),
        compiler_params=_params("parallel"),
        name="attn_decode_" + kind,
    )(*args)


def _out_proj_kernel(oa_ref, ob_ref, oc_ref, ga_ref, gb_ref, gc_ref, w_ref, x_ref, o_ref):
    acc = x_ref[...]
    for g, (o, gn) in enumerate(((oa_ref, ga_ref), (ob_ref, gb_ref), (oc_ref, gc_ref))):
        y = _rms(o[...], gn[...]).astype(BF16)
        acc = acc + _dot(y, w_ref[g * W_GROUP:(g + 1) * W_GROUP, :])
    o_ref[...] = acc


def out_proj(oa, ob, oc, p, x, l):
    m = x.shape[0]
    tm = _pick_tile(m, 512)
    row = lambda w: pl.BlockSpec((tm, w), lambda i: (i, 0))
    full = lambda a: pl.BlockSpec(a.shape, lambda i: (0,) * a.ndim)
    consts = (p["out_norm_a"], p["out_norm_b"], p["out_norm_c"], p["w_out"])
    w_spec = pl.BlockSpec((None,) + p["w_out"].shape[1:], lambda i: (l, 0, 0))
    return pl.pallas_call(
        _out_proj_kernel,
        grid=(m // tm,),
        in_specs=[row(W_GROUP)] * 3 + [full(a) for a in consts[:3]] + [w_spec, row(D_MODEL)],
        out_specs=row(D_MODEL),
        out_shape=jax.ShapeDtypeStruct((m, D_MODEL), F32),
        compiler_params=_params("parallel"),
        name="out_proj",
    )(oa, ob, oc, *consts, x)


def _ffn_kernel(x_ref, g_ref, wg_ref, wu_ref, wd_ref, o_ref, h_ref):
    @pl.when(pl.program_id(1) == 0)
    def _():
        x = x_ref[...]
        h_ref[...] = _rms(x, g_ref[...]).astype(BF16)
        o_ref[...] = x

    h = h_ref[...]
    gate = _dot(h, wg_ref[...])
    up = _dot(h, wu_ref[...])
    act = (gate * jax.nn.sigmoid(gate) * up).astype(BF16)
    o_ref[...] += _dot(act, wd_ref[...])


def ffn(x, g, w_gu, w_down, l):
    m = x.shape[0]
    tm = _pick_tile(m, 1024)
    tf = 512
    nf = D_FF // tf
    return pl.pallas_call(
        _ffn_kernel,
        grid=(m // tm, nf),
        in_specs=[pl.BlockSpec((tm, D_MODEL), lambda i, j: (i, 0)),
                  pl.BlockSpec((1, D_MODEL), lambda i, j: (0, 0)),
                  pl.BlockSpec((None, D_MODEL, tf), lambda i, j: (l, 0, j)),
                  pl.BlockSpec((None, D_MODEL, tf), lambda i, j: (l, 0, j + nf)),
                  pl.BlockSpec((None, tf, D_MODEL), lambda i, j: (l, j, 0))],
        out_specs=pl.BlockSpec((tm, D_MODEL), lambda i, j: (i, 0)),
        out_shape=jax.ShapeDtypeStruct((m, D_MODEL), F32),
        scratch_shapes=[pltpu.VMEM((tm, D_MODEL), BF16)],
        compiler_params=_params("parallel", "arbitrary"),
        name="ffn",
    )(x, g, w_gu, w_gu, w_down)


def _dup_rope(r):
    half = MLA_ROPE_DIM // 2
    return jnp.concatenate([r, r[..., half:], r[..., :half]], axis=-1)


def _prep_layer(l, fox_f_bias, fox_q_norm, fox_k_norm, mla_qa_norm, mla_w_uq, mla_kva_norm, mla_w_ukv,
                mla_q_norm, mla_k_norm, out_norm_a, out_norm_b, out_norm_c, w_in, w_out, w_gu, w_down,
                norm_mix, norm_ffn):
    uq = mla_w_uq[l].reshape(MLA_Q_RANK, N_HEADS, MLA_QK_DIM)
    uq = jnp.concatenate([uq[..., :MLA_NOPE_DIM], _dup_rope(uq[..., MLA_NOPE_DIM:])], axis=-1)
    ukv = mla_w_ukv[l].reshape(MLA_KV_RANK, N_HEADS, 2 * LANES)
    ukv = jnp.concatenate([ukv[..., :LANES].reshape(MLA_KV_RANK, W_GROUP),
                           ukv[..., LANES:].reshape(MLA_KV_RANK, W_GROUP)], axis=1)

    def gain256(g):
        return jnp.concatenate([g[:MLA_NOPE_DIM], _dup_rope(g[MLA_NOPE_DIM:])])[None, :]

    row = lambda a: a[l][None, :]
    return {
        "norm_mix": row(norm_mix), "w_in": w_in,
        "fox_q_norm": row(fox_q_norm), "fox_k_norm": row(fox_k_norm),
        "fox_f_bias": jnp.pad(fox_f_bias[l], (0, LANES - N_HEADS))[None, :],
        "mla_qa_norm": row(mla_qa_norm),
        "mla_w_uq": uq.reshape(MLA_Q_RANK, N_HEADS * MLA_HEAD_PAD).astype(BF16),
        "mla_q_gain": gain256(mla_q_norm[l]) * (MLA_QK_DIM ** -0.5 * LOG2E),
        "mla_kva_norm": row(mla_kva_norm),
        "mla_w_ukv": ukv.astype(BF16),
        "mla_k_gain": gain256(mla_k_norm[l]),
        "out_norm_a": row(out_norm_a), "out_norm_b": row(out_norm_b), "out_norm_c": row(out_norm_c),
        "w_out": w_out,
        "norm_ffn": row(norm_ffn), "w_gu": w_gu, "w_down": w_down,
    }


def _rope_tables(pos):
    half = MLA_ROPE_DIM // 2
    inv_freq = ROPE_THETA ** (-(jnp.arange(half, dtype=F32) / half))
    ang = pos.astype(F32)[:, None] * inv_freq[None, :]
    cos, sin = jnp.cos(ang), jnp.sin(ang)
    zero = jnp.zeros_like(cos)
    return (jnp.concatenate([cos, cos, zero, zero], axis=1),
            jnp.concatenate([-sin, sin, zero, zero], axis=1))


def _layer(x, caches, l, depth, p, tabs, stacks):
    nb, t, _ = x.shape
    m = nb * t
    x2 = x.reshape(m, D_MODEL)
    proj = rms_matmul(x2, p["norm_mix"], p["w_in"], l)
    prompt = caches is None
    outs = post(proj, p, *tabs["q"], with_kt=prompt, l=l, depth=depth, stacks=stacks)
    stacks = outs[:5]
    qa, vab, lf, qb, vbb, qc, krd, k2a, k2b = outs[5:]
    if prompt:
        kct, vc_new = kexp(stacks[4], krd, p, *tabs["q"], transposed=True, layer=l)
        f = cumsum_rows(lf.reshape(nb, t, LANES))
        oa = attn_prompt("fox", qa, k2a, vab, f[0, :N_HEADS])
        ob = attn_prompt("sb", qb, k2b, vbb)
        oc = attn_prompt("mla", qc, kct, vc_new)
    else:
        c_fk, c_fv, c_lf, c_sk, c_sv, c_ckv, c_kr = caches
        pl_ = c_fk.shape[2]
        kc_new, vc_new = kexp(stacks[4], krd, p, *tabs["q"], transposed=False, layer=l)
        lf_all = jnp.concatenate(
            [jnp.pad(c_lf[l], ((0, 0), (0, 0), (0, LANES - N_HEADS))), lf.reshape(nb, t, LANES)], axis=1)
        n_pad = -(-(pl_ + t) // LANES) * LANES
        lf_all = jnp.pad(lf_all, ((0, 0), (0, n_pad - pl_ - t), (0, 0)))
        f = cumsum_rows(lf_all)[:, :N_HEADS, None, :]
        kc_old, vc_old = kexp(c_ckv.reshape(c_ckv.shape[0], nb * pl_, MLA_KV_RANK),
                              _dup_rope(c_kr[l]).reshape(nb * pl_, LANES), p, *tabs["kc"],
                              transposed=False, layer=l)
        oa = attn_decode("fox", qa, k2a, vab, c_fk, c_fv, layer=l, fn=f[..., pl_:pl_ + t], fc=f[..., :pl_])
        ob = attn_decode("sb", qb, k2b, vbb, c_sk, c_sv, layer=l)
        oc = attn_decode("mla", qc, kc_new, vc_new, kc_old.reshape(nb, pl_, -1), vc_old.reshape(nb, pl_, -1))
    x2 = out_proj(oa, ob, oc, p, x2, l)
    x2 = ffn(x2, p["norm_ffn"], p["w_gu"], p["w_down"], l)
    small = (lf[:, :N_HEADS].reshape(nb, t, N_HEADS), krd[:, :MLA_ROPE_DIM].reshape(nb, t, MLA_ROPE_DIM))
    return x2.reshape(nb, t, D_MODEL), stacks, small


def kernel(x_prompt, x_sample, cache_fox_k, cache_fox_v, cache_fox_logf, cache_sb_k, cache_sb_v, cache_mla_ckv, cache_mla_krope, norm_mix, w_in, fox_f_bias, fox_q_norm, fox_k_norm, mla_qa_norm, mla_w_uq, mla_kva_norm, mla_w_ukv, mla_q_norm, mla_k_norm, out_norm_a, out_norm_b, out_norm_c, w_out, norm_ffn, w_gu, w_down):
    depth = w_in.shape[0]
    t_p = x_prompt.shape[1]
    nb_s, t_s = x_sample.shape[0], x_sample.shape[1]
    past_len = cache_fox_k.shape[2]
    tabs_p = {"q": _rope_tables(jnp.arange(t_p, dtype=jnp.int32))}
    tabs_s = {"q": _rope_tables(jnp.tile(past_len + jnp.arange(t_s, dtype=jnp.int32), nb_s)),
              "kc": _rope_tables(jnp.tile(jnp.arange(past_len, dtype=jnp.int32), nb_s))}
    caches = (cache_fox_k, cache_fox_v, cache_fox_logf, cache_sb_k, cache_sb_v, cache_mla_ckv, cache_mla_krope)
    y_p, y_s = x_prompt, x_sample
    rows_p, rows_s = [], []
    def new_stacks(m):
        heads = tuple(jnp.zeros((depth, m, N_HEADS, HEAD_DIM), F32) for _ in range(4))
        return heads + (jnp.zeros((depth, m, MLA_KV_RANK), F32),)

    stacks_p = new_stacks(x_prompt.shape[0] * t_p)
    stacks_s = new_stacks(nb_s * t_s)
    w_in_b = w_in_prep(w_in)
    w_out_b, w_gu_b, w_down_b = w_out.astype(BF16), w_gu.astype(BF16), w_down.astype(BF16)
    for l in range(depth):
        p = _prep_layer(l, fox_f_bias, fox_q_norm, fox_k_norm, mla_qa_norm, mla_w_uq, mla_kva_norm, mla_w_ukv,
                        mla_q_norm, mla_k_norm, out_norm_a, out_norm_b, out_norm_c, w_in_b, w_out_b, w_gu_b,
                        w_down_b, norm_mix, norm_ffn)
        y_p, stacks_p, r_p = _layer(y_p, None, l, depth, p, tabs_p, stacks_p)
        y_s, stacks_s, r_s = _layer(y_s, caches, l, depth, p, tabs_s, stacks_s)
        rows_p.append(r_p)
        rows_s.append(r_s)

    def assemble(stacks, rows, nb, t):
        fk, fv, sk, sv, ckv = stacks
        heads = lambda a: a.reshape(depth, nb, t, N_HEADS, HEAD_DIM)
        return (heads(fk), heads(fv), jnp.stack([r[0] for r in rows], axis=0), heads(sk), heads(sv),
                ckv.reshape(depth, nb, t, MLA_KV_RANK), jnp.stack([r[1] for r in rows], axis=0))

    return ((y_p, y_s) + assemble(stacks_p, rows_p, x_prompt.shape[0], t_p)
            + assemble(stacks_s, rows_s, nb_s, t_s))
```

```python
import functools
import math

import jax
import jax.numpy as jnp
from jax import lax
from jax.experimental import pallas as pl
from jax.experimental.pallas import tpu as pltpu

D_MODEL = 2048
CHUNK = 64
HEAD_DIM = 128
N_HEADS = 4
W_GROUP = N_HEADS * HEAD_DIM
MLA_Q_RANK = 512
MLA_KV_RANK = 256
MLA_NOPE_DIM = 128
MLA_ROPE_DIM = 64
MLA_QK_DIM = MLA_NOPE_DIM + MLA_ROPE_DIM
MLA_HEAD_PAD = 256
ROPE_THETA = 10000.0
D_FF = 5632
RMS_EPS = 1e-6
NEG_INF = -1e30
LOG2E = math.log2(math.e)
N_IN_PAD = 4096
LANES = 128
TRI = 256
ATTN_TILE = 512
INPROJ_ROWS = 256
PROJ_CHUNK = 1024
ROW_GROUP = 128
UNDERFLOW_BITS = 160.0
VMEM_LIMIT = 56 * 1024 * 1024

C_QA, C_KA, C_VA, C_QB, C_KB, C_VB, C_CQ, C_CKV, C_KR, C_FA = (
    0, 512, 1024, 1536, 2048, 2560, 3072, 3584, 3840, 3968)

BF16 = jnp.bfloat16
F32 = jnp.float32


def _params(*sem):
    return pltpu.CompilerParams(dimension_semantics=sem, vmem_limit_bytes=VMEM_LIMIT)


def _pick_tile(n, pref):
    t = min(n, pref)
    while n % t:
        t //= 2
    return t


def _rms(x, g):
    return x * lax.rsqrt(jnp.mean(x * x, axis=-1, keepdims=True) + RMS_EPS) * g


def _dot(a, b):
    return jnp.dot(a, b, preferred_element_type=F32)


def _dot_nt(a, b):
    return lax.dot_general(a, b, (((1,), (1,)), ((), ())), preferred_element_type=F32)


def _split2(x):
    hi = x.astype(BF16)
    lo = (x - hi.astype(F32)).astype(BF16)
    return hi, lo


def _neg_abs(x):
    bits = lax.bitcast_convert_type(x, jnp.uint32) | jnp.uint32(0x80000000)
    return lax.bitcast_convert_type(bits, F32)


def _split3(x):
    a = x.astype(BF16)
    r = x - a.astype(F32)
    b = r.astype(BF16)
    c = (r - b.astype(F32)).astype(BF16)
    return a, b, c


def _log_sigmoid(x):
    return -(jnp.maximum(-x, 0.0) + jnp.log1p(jnp.exp(-jnp.abs(x))))


def _log2_keep(z2):
    return -(jnp.maximum(z2, 0.0) + jnp.log2(1.0 + jnp.exp2(_neg_abs(z2))))


def _w_in_prep_kernel(w_ref, o_ref):
    rows = w_ref.shape[0]
    fa0 = C_QB
    o_ref[:, :C_QB] = w_ref[:, :fa0].astype(BF16)
    o_ref[:, C_QB:C_KR] = w_ref[:, fa0 + N_HEADS:C_KR + N_HEADS].astype(BF16)
    kr = w_ref[:, C_KR + N_HEADS:C_KR + N_HEADS + MLA_ROPE_DIM]
    half = MLA_ROPE_DIM // 2
    o_ref[:, C_KR:C_FA] = jnp.concatenate([kr, kr[:, half:], kr[:, :half]], axis=1).astype(BF16)
    fa = w_ref[:, fa0:fa0 + N_HEADS]
    o_ref[:, C_FA:] = jnp.concatenate([fa, jnp.zeros((rows, LANES - N_HEADS), F32)], axis=1).astype(BF16)


def w_in_prep(w_in):
    depth, d, n = w_in.shape
    tr = 256
    return pl.pallas_call(
        _w_in_prep_kernel,
        grid=(depth, d // tr),
        in_specs=[pl.BlockSpec((None, tr, n), lambda l, i: (l, i, 0))],
        out_specs=pl.BlockSpec((None, tr, N_IN_PAD), lambda l, i: (l, i, 0)),
        out_shape=jax.ShapeDtypeStruct((depth, d, N_IN_PAD), BF16),
        compiler_params=_params("parallel", "parallel"),
        name="w_in_prep",
    )(w_in)


def _rope_chunk(y2, cos, sin):
    return y2 * cos + pltpu.roll(y2, 64, 1) * sin


def _rope_lane_mask():
    return lax.broadcasted_iota(jnp.int32, (1, LANES), 1) < MLA_ROPE_DIM


def _inproj_post_kernel(with_kt, n_alias, x_ref, g_ref, w_ref, fqg_ref, fkg_ref, fb_ref, qag_ref, wuq_ref, qg_ref,
                        kvg_ref, cos_ref, sin_ref, *refs):
    refs = refs[n_alias:]
    (ka_o, va_o, kb_o, vb_o, ckv_o, qa_o, vab_o, lf_o, qb_o, vbb_o, qc_o, kr_o, k2a_o, k2b_o,
     proj_ref) = refs
    hn = _rms(x_ref[...], g_ref[...]).astype(BF16)
    for c in range(N_IN_PAD // PROJ_CHUNK):
        cols = slice(c * PROJ_CHUNK, (c + 1) * PROJ_CHUNK)
        proj_ref[:, cols] = _dot(hn, w_ref[:, cols])
    scale = HEAD_DIM ** -0.5 * LOG2E
    va = proj_ref[:, C_VA:C_VA + W_GROUP]
    vb = proj_ref[:, C_VB:C_VB + W_GROUP]
    for h in range(N_HEADS):
        sl = slice(h * HEAD_DIM, (h + 1) * HEAD_DIM)
        qa = proj_ref[:, C_QA + h * HEAD_DIM:C_QA + (h + 1) * HEAD_DIM]
        qa_o[:, sl] = (_rms(qa, fqg_ref[...]) * scale).astype(BF16)
        ka = _rms(proj_ref[:, C_KA + h * HEAD_DIM:C_KA + (h + 1) * HEAD_DIM], fkg_ref[...])
        kb = proj_ref[:, C_KB + h * HEAD_DIM:C_KB + (h + 1) * HEAD_DIM]
        ka_o[:, h, :] = ka
        kb_o[:, h, :] = kb
        va_o[:, h, :] = va[:, sl]
        vb_o[:, h, :] = vb[:, sl]
        if with_kt:
            k2a_o[h, 0] = ka.T.astype(BF16)
            k2b_o[h, 0] = kb.T.astype(BF16)
        else:
            k2a_o[:, sl] = ka.astype(BF16)
            k2b_o[:, sl] = kb.astype(BF16)
    vab_o[...] = va.astype(BF16)
    qb_o[...] = (proj_ref[:, C_QB:C_QB + W_GROUP] * scale).astype(BF16)
    vbb_o[...] = vb.astype(BF16)

    lane = lax.broadcasted_iota(jnp.int32, (1, LANES), 1)
    lf = _log_sigmoid(proj_ref[:, C_FA:C_FA + LANES] + fb_ref[...])
    lf_o[...] = jnp.where(lane < N_HEADS, lf, 0.0)

    cqn = _rms(proj_ref[:, C_CQ:C_CQ + MLA_Q_RANK], qag_ref[...]).astype(BF16)
    qc = _dot(cqn, wuq_ref[...])
    rmask = _rope_lane_mask()
    cos = cos_ref[...]
    sin = sin_ref[...]
    for h in range(N_HEADS):
        c1 = qc[:, h * MLA_HEAD_PAD:h * MLA_HEAD_PAD + LANES]
        c2 = qc[:, h * MLA_HEAD_PAD + LANES:(h + 1) * MLA_HEAD_PAD]
        ss = (jnp.sum(c1 * c1, axis=-1, keepdims=True)
              + jnp.sum(jnp.where(rmask, c2 * c2, 0.0), axis=-1, keepdims=True))
        rs = lax.rsqrt(ss * (1.0 / MLA_QK_DIM) + RMS_EPS)
        y1 = c1 * rs * qg_ref[:, :LANES]
        y2 = c2 * rs * qg_ref[:, LANES:]
        qc_o[:, h * MLA_HEAD_PAD:h * MLA_HEAD_PAD + LANES] = y1.astype(BF16)
        qc_o[:, h * MLA_HEAD_PAD + LANES:(h + 1) * MLA_HEAD_PAD] = _rope_chunk(y2, cos, sin).astype(BF16)

    ckv_o[...] = _rms(proj_ref[:, C_CKV:C_CKV + MLA_KV_RANK], kvg_ref[...])
    kr_o[...] = proj_ref[:, C_KR:C_KR + LANES]


def inproj_post(x, p, cos, sin, with_kt, l, depth, stacks):
    m = x.shape[0]
    tm = _pick_tile(m, INPROJ_ROWS)
    kt_tile = _pick_tile(m, ATTN_TILE)
    sub = kt_tile // tm
    row = lambda w: pl.BlockSpec((tm, w), lambda i: (i, 0))
    full = lambda a: pl.BlockSpec(a.shape, lambda i: (0,) * a.ndim)
    consts = (p["fox_q_norm"], p["fox_k_norm"], p["fox_f_bias"], p["mla_qa_norm"], p["mla_w_uq"],
              p["mla_q_gain"], p["mla_kva_norm"])
    w_spec = pl.BlockSpec((None, D_MODEL, N_IN_PAD), lambda i: (l, 0, 0), pipeline_mode=pl.Buffered(1))
    heads_spec = pl.BlockSpec((None, tm, N_HEADS, HEAD_DIM), lambda i: (l, i, 0, 0))
    heads_shape = jax.ShapeDtypeStruct((depth, m, N_HEADS, HEAD_DIM), F32)
    out_specs = [heads_spec] * 4 + [pl.BlockSpec((None, tm, MLA_KV_RANK), lambda i: (l, i, 0))]
    out_shape = [heads_shape] * 4 + [jax.ShapeDtypeStruct((depth, m, MLA_KV_RANK), F32)]
    outs = [(W_GROUP, BF16), (W_GROUP, BF16), (LANES, F32), (W_GROUP, BF16), (W_GROUP, BF16),
            (N_HEADS * MLA_HEAD_PAD, BF16), (LANES, F32)]
    out_specs += [row(w) for w, _ in outs]
    out_shape += [jax.ShapeDtypeStruct((m, w), dt) for w, dt in outs]
    for _ in range(2):
        if with_kt:
            out_specs.append(pl.BlockSpec((N_HEADS, 1, HEAD_DIM, tm), lambda i: (0, i // sub, 0, i % sub)))
            out_shape.append(jax.ShapeDtypeStruct((N_HEADS, m // kt_tile, HEAD_DIM, kt_tile), BF16))
        else:
            out_specs.append(row(W_GROUP))
            out_shape.append(jax.ShapeDtypeStruct((m, W_GROUP), BF16))
    n_in = 3 + len(consts) + 2
    stacks = () if stacks is None else tuple(stacks)
    return pl.pallas_call(
        functools.partial(_inproj_post_kernel, with_kt, len(stacks)),
        grid=(m // tm,),
        in_specs=([row(D_MODEL), full(p["norm_mix"]), w_spec] + [full(a) for a in consts]
                  + [row(LANES), row(LANES)] + [pl.BlockSpec(memory_space=pl.ANY)] * len(stacks)),
        out_specs=out_specs,
        out_shape=out_shape,
        scratch_shapes=[pltpu.VMEM((tm, N_IN_PAD), F32)],
        input_output_aliases={n_in + k: k for k in range(len(stacks))},
        compiler_params=_params("parallel"),
        name="inproj_post",
    )(x, p["norm_mix"], p["w_in"], *consts, cos, sin, *stacks)


def _kexp_kernel(transposed, ckv_ref, kr_ref, w_ref, kg_ref, cos_ref, sin_ref, kc_o, vc_o):
    kv = _dot(ckv_ref[...].astype(BF16), w_ref[...])
    kr = kr_ref[...]
    krsq = jnp.sum(jnp.where(_rope_lane_mask(), kr * kr, 0.0), axis=-1, keepdims=True)
    cos = cos_ref[...]
    sin = sin_ref[...]
    for h in range(N_HEADS):
        n = kv[:, h * LANES:(h + 1) * LANES]
        ss = jnp.sum(n * n, axis=-1, keepdims=True) + krsq
        rs = lax.rsqrt(ss * (1.0 / MLA_QK_DIM) + RMS_EPS)
        k1 = n * rs * kg_ref[:, :LANES]
        k2 = _rope_chunk(kr * rs * kg_ref[:, LANES:], cos, sin)
        if transposed:
            kc_o[h, 0, :LANES, :] = k1.T.astype(BF16)
            kc_o[h, 0, LANES:, :] = k2.T.astype(BF16)
        else:
            kc_o[:, h * MLA_HEAD_PAD:h * MLA_HEAD_PAD + LANES] = k1.astype(BF16)
            kc_o[:, h * MLA_HEAD_PAD + LANES:(h + 1) * MLA_HEAD_PAD] = k2.astype(BF16)
    vc_o[...] = kv[:, W_GROUP:].astype(BF16)


def kexp(ckv, krdup, p, cos, sin, transposed, layer=None):
    m = krdup.shape[0]
    tm = _pick_tile(m, ATTN_TILE)
    row = lambda w: pl.BlockSpec((tm, w), lambda i: (i, 0))
    full = lambda a: pl.BlockSpec(a.shape, lambda i: (0,) * a.ndim)
    ckv_spec = row(MLA_KV_RANK) if layer is None else pl.BlockSpec(
        (None, tm, MLA_KV_RANK), lambda i: (layer, i, 0))
    if transposed:
        kc_spec = pl.BlockSpec((N_HEADS, 1, MLA_HEAD_PAD, tm), lambda i: (0, i, 0, 0))
        kc_shape = jax.ShapeDtypeStruct((N_HEADS, m // tm, MLA_HEAD_PAD, tm), BF16)
    else:
        kc_spec = row(N_HEADS * MLA_HEAD_PAD)
        kc_shape = jax.ShapeDtypeStruct((m, N_HEADS * MLA_HEAD_PAD), BF16)
    return pl.pallas_call(
        functools.partial(_kexp_kernel, transposed),
        grid=(m // tm,),
        in_specs=[ckv_spec, row(LANES), full(p["mla_w_ukv"]), full(p["mla_k_gain"]), row(LANES), row(LANES)],
        out_specs=[kc_spec, row(W_GROUP)],
        out_shape=[kc_shape, jax.ShapeDtypeStruct((m, W_GROUP), BF16)],
        compiler_params=_params("parallel"),
        name="kexp",
    )(ckv, krdup, p["mla_w_ukv"], p["mla_k_gain"], cos, sin)


def _cumsum_kernel(x_ref, o_ref, carry_ref):
    @pl.when(pl.program_id(1) == 0)
    def _():
        carry_ref[...] = jnp.zeros_like(carry_ref)

    t = x_ref.shape[1]
    xt = x_ref[0].T[:8, :]
    ri = lax.broadcasted_iota(jnp.int32, (t, t), 0)
    ci = lax.broadcasted_iota(jnp.int32, (t, t), 1)
    upper = jnp.where(ri <= ci, 1.0, 0.0).astype(BF16)
    a, b, c = _split3(xt)
    f = _dot(a, upper) + _dot(b, upper) + _dot(c, upper) + carry_ref[:, :1]
    o_ref[0] = f
    carry_ref[...] = jnp.broadcast_to(f[:, t - 1:t], carry_ref.shape)


def cumsum_rows(x):
    b, n, _ = x.shape
    t = n if n <= 1280 else _pick_tile(n, 512)
    return pl.pallas_call(
        _cumsum_kernel,
        grid=(b, n // t),
        in_specs=[pl.BlockSpec((1, t, LANES), lambda i, j: (i, j, 0))],
        out_specs=pl.BlockSpec((1, 8, t), lambda i, j: (i, 0, j)),
        out_shape=jax.ShapeDtypeStruct((b, 8, n), F32),
        scratch_shapes=[pltpu.VMEM((8, LANES), F32)],
        compiler_params=_params("parallel", "arbitrary"),
        name="cumsum_rows",
    )(x)


def _tile_mask(kind, qpos, kpos):
    if kind == "fox":
        return kpos <= qpos
    if kind == "sb":
        return kpos < qpos
    shift = CHUNK.bit_length() - 1
    return lax.shift_right_logical(kpos, shift) <= lax.shift_right_logical(qpos, shift)


def _tri_lower(n):
    ri = lax.broadcasted_iota(jnp.int32, (n, n), 0)
    ci = lax.broadcasted_iota(jnp.int32, (n, n), 1)
    return jnp.where(ri >= ci, 1.0, 0.0).astype(BF16)


def _sb_weights(z, carry, mask, tri):
    lk = _log2_keep(z)
    if mask is not None:
        lk = jnp.where(mask, lk, 0.0)
    blk = tri.shape[0]
    nblk = z.shape[1] // blk
    parts = [None] * nblk
    for c in reversed(range(nblk)):
        lkc = lk[:, c * blk:(c + 1) * blk]
        hi, lo = _split2(lkc)
        intra = _dot(hi, tri) + _dot(lo, tri)
        if carry.shape[1] == 1 or carry.shape[1] == blk:
            parts[c] = intra + carry
        else:
            parts[c] = intra + jnp.concatenate([carry] * (blk // carry.shape[1]), axis=1)
        carry = carry + jnp.sum(lkc, axis=-1, keepdims=True)
    r = parts[0] if nblk == 1 else jnp.concatenate(parts, axis=-1)
    a = jnp.exp2(z + r)
    if mask is not None:
        a = jnp.where(mask, a, 0.0)
    return a, carry


def _attn_prompt_kernel(kind, tq, *refs):
    f_ref = al_ref = kmax_ref = None
    if kind == "fox":
        q_ref, kt_ref, v_ref, f_ref, o_ref, m_ref, acc_ref, s_ref, p_ref, al_ref, kmax_ref = refs
    elif kind == "mla":
        q_ref, kt_ref, v_ref, o_ref, m_ref, acc_ref, s_ref, p_ref, al_ref = refs
    else:
        q_ref, kt_ref, v_ref, o_ref, m_ref, acc_ref, s_ref, p_ref, kmax_ref = refs
    early = kind != "mla"
    qb = pl.program_id(1)
    rg = min(TRI if kind == "sb" else ROW_GROUP, tq)
    n_rg = tq // rg
    q = q_ref[...]
    ones = jnp.ones((tq, LANES), BF16)
    tri2 = jnp.concatenate([_tri_lower(rg)] * 2, axis=0) if kind == "sb" else None
    fref = f_ref[0, qb][:, :1] if kind == "fox" else None

    def key_block(j):
        return jnp.clip(qb - j, 0, qb)

    def stage_a(j, slot):
        s_ref[slot] = _dot(q, kt_ref[0, key_block(j)])

    def stage_b(j, slot, diagonal):
        brow = (fref - f_ref[0, key_block(j)]) * LOG2E if kind == "fox" else None
        ms, als, prs = [], [], []
        for r in range(n_rg):
            rows = slice(r * rg, (r + 1) * rg)
            kw = (r + 1) * rg if diagonal else tq
            s = s_ref[slot, rows, :kw]
            mask = None
            if diagonal:
                qpos = r * rg + lax.broadcasted_iota(jnp.int32, (rg, kw), 0)
                kpos = lax.broadcasted_iota(jnp.int32, (rg, kw), 1)
                mask = _tile_mask(kind, qpos, kpos)
            m_prev = m_ref[rows, :]
            if kind == "sb":
                plk = jnp.maximum(s, 0.0) + jnp.log2(1.0 + jnp.exp2(_neg_abs(s)))
                if mask is not None:
                    plk = jnp.where(mask, plk, 0.0)
                carry = m_prev
                parts = [None] * (kw // rg)
                for c in reversed(range(kw // rg)):
                    pc = plk[:, c * rg:(c + 1) * rg]
                    hi, lo = _split2(pc)
                    later = _dot(jnp.concatenate([hi, lo], axis=1), tri2)
                    parts[c] = s[:, c * rg:(c + 1) * rg] - later - jnp.concatenate([carry] * (rg // LANES), 1)
                    carry = carry + jnp.sum(pc, axis=-1, keepdims=True)
                a = jnp.exp2(parts[0] if len(parts) == 1 else jnp.concatenate(parts, axis=1))
                if mask is not None:
                    a = jnp.where(mask, a, 0.0)
                pr = a.astype(BF16)
                ms.append(carry)
            else:
                if kind == "fox":
                    s = s + brow[:, :kw]
                if mask is not None:
                    s = jnp.where(mask, s, NEG_INF)
                chunks = [s[:, c * LANES:(c + 1) * LANES] for c in range(kw // LANES)]
                m_cur = jnp.max(functools.reduce(jnp.maximum, chunks), axis=-1, keepdims=True)
                m_new = jnp.maximum(m_prev, m_cur)
                als.append(jnp.exp2(m_prev - m_new))
                pr = jnp.concatenate([jnp.exp2(c - m_new).astype(BF16) for c in chunks], axis=1)
                ms.append(m_new)
            if kw < tq:
                pr = jnp.concatenate([pr, jnp.zeros((rg, tq - kw), BF16)], axis=1)
            prs.append(pr)
        m_ref[...] = jnp.concatenate(ms, axis=0)
        p_ref[slot] = jnp.concatenate(prs, axis=0)
        if kind != "sb":
            al_ref[slot] = jnp.concatenate(als, axis=0)

    def stage_c(j, slot):
        start = pl.multiple_of(key_block(j) * tq, tq)
        v = v_ref[pl.ds(start, tq), :]
        if kind == "sb":
            acc_ref[...] += _dot(p_ref[slot], v)
        else:
            al = al_ref[slot]
            pv = _dot(p_ref[slot], jnp.concatenate([v, ones], axis=1))
            acc_ref[...] = jnp.concatenate([al, al], axis=1) * acc_ref[...] + pv

    def step(j, slot):
        stage_b(j, slot, False)
        stage_a(j + 1, 1 - slot)
        stage_c(j - 1, 1 - slot)

    if early:
        @pl.when(qb == 0)
        def _():
            def norm_body(i, mx):
                kt = kt_ref[0, i].astype(F32)
                return jnp.maximum(mx, jnp.max(jnp.sum(kt * kt, axis=0, keepdims=True), axis=1, keepdims=True))

            mx = lax.fori_loop(0, kt_ref.shape[1], norm_body, jnp.zeros((1, 1), F32))
            kmax_ref[...] = jnp.broadcast_to(jnp.sqrt(mx), kmax_ref.shape)

        qf = q.astype(F32)
        zb = jnp.sqrt(jnp.sum(qf * qf, axis=1, keepdims=True)) * kmax_ref[:1, :]

    def exhausted(j):
        if kind == "sb":
            return jnp.min(m_ref[...] - zb) > UNDERFLOW_BITS
        brow_max = jnp.max((fref - f_ref[0, key_block(j + 1)]) * LOG2E)
        return jnp.min(m_ref[...] - zb) - brow_max > UNDERFLOW_BITS

    m_ref[...] = jnp.full(m_ref.shape, 0.0 if kind == "sb" else NEG_INF, F32)
    acc_ref[...] = jnp.zeros_like(acc_ref)
    stage_a(0, 0)
    stage_b(0, 0, True)
    stage_a(1, 1)

    if kind == "sb":
        def cond1(c):
            return jnp.logical_and(c[0] <= qb, jnp.logical_not(c[1]))

        def single(c):
            j = c[0]
            step(j, j & 1)
            return j + 1, exhausted(j)

        nxt, _ = lax.while_loop(cond1, single, (jnp.int32(1), exhausted(0)))
        stage_c(nxt - 1, (nxt - 1) & 1)
        o_ref[...] = acc_ref[...]
        return

    if early:
        def cond(c):
            return jnp.logical_and(c[0] < qb // 2, jnp.logical_not(c[1]))

        def pair(c):
            t = c[0]
            step(1 + 2 * t, 1)
            step(2 + 2 * t, 0)
            return t + 1, exhausted(2 + 2 * t)

        pairs, done = lax.while_loop(cond, pair, (jnp.int32(0), exhausted(0)))
        last = 2 * pairs
        tail = jnp.logical_and(jnp.logical_not(done), last != qb)
    else:
        def body(t, c):
            step(1 + 2 * t, 1)
            step(2 + 2 * t, 0)
            return c

        lax.fori_loop(0, qb // 2, body, 0)
        last = 2 * (qb // 2)
        tail = last != qb

    @pl.when(tail)
    def _():
        step(qb, 1)
        stage_c(qb, 1)

    @pl.when(jnp.logical_not(tail))
    def _():
        stage_c(last, 0)

    o_ref[...] = acc_ref[:, :HEAD_DIM] / acc_ref[:, HEAD_DIM:]


def attn_prompt(kind, q, kt, v, f=None):
    t = q.shape[0]
    dq = q.shape[1] // N_HEADS
    tq = kt.shape[3]
    nq = t // tq
    in_specs = [pl.BlockSpec((tq, dq), lambda h, i: (i, h)),
                pl.BlockSpec((1, nq, dq, tq), lambda h, i: (h, 0, 0, 0)),
                pl.BlockSpec((t, HEAD_DIM), lambda h, i: (0, h))]
    args = [q, kt, v]
    if kind == "fox":
        in_specs.append(pl.BlockSpec((1, nq, 1, tq), lambda h, i: (h, 0, 0, 0)))
        args.append(f.reshape(N_HEADS, nq, 1, tq))
    acc_w = HEAD_DIM if kind == "sb" else 2 * HEAD_DIM
    scratch = [pltpu.VMEM((tq, LANES), F32), pltpu.VMEM((tq, acc_w), F32),
               pltpu.VMEM((2, tq, tq), F32), pltpu.VMEM((2, tq, tq), BF16)]
    if kind != "sb":
        scratch.append(pltpu.VMEM((2, tq, LANES), F32))
    if kind != "mla":
        scratch.append(pltpu.VMEM((8, LANES), F32))
    return pl.pallas_call(
        functools.partial(_attn_prompt_kernel, kind, tq),
        grid=(N_HEADS, nq),
        in_specs=in_specs,
        out_specs=pl.BlockSpec((tq, HEAD_DIM), lambda h, i: (i, h)),
        out_shape=jax.ShapeDtypeStruct((t, W_GROUP), F32),
        scratch_shapes=scratch,
        compiler_params=_params("parallel", "arbitrary"),
        name="attn_prompt_" + kind,
    )(*args)


def _softmax_tile(s, v, m_prev, l_prev, acc_prev):
    m_new = jnp.maximum(m_prev, jnp.max(s, axis=-1, keepdims=True))
    alpha = jnp.exp2(m_prev - m_new)
    pr = jnp.exp2(s - m_new)
    l_new = alpha * l_prev + jnp.sum(pr, axis=-1, keepdims=True)
    acc_new = alpha * acc_prev + _dot(pr.astype(BF16), v)
    return m_new, l_new, acc_new


def _attn_decode_kernel(kind, *refs):
    if kind == "fox":
        q_ref, kn_ref, vn_ref, kc_ref, vc_ref, fn_ref, fc_ref, o_ref = refs
    else:
        q_ref, kn_ref, vn_ref, kc_ref, vc_ref, o_ref = refs
    tq = q_ref.shape[0]
    dq = q_ref.shape[1] // N_HEADS
    past = kc_ref.shape[0]
    qpos = past + lax.broadcasted_iota(jnp.int32, (tq, tq), 0)
    kpos = past + lax.broadcasted_iota(jnp.int32, (tq, tq), 1)
    mask = _tile_mask(kind, qpos, kpos)
    for h in range(N_HEADS):
        q = q_ref[:, h * dq:(h + 1) * dq]
        kn = kn_ref[:, h * dq:(h + 1) * dq]
        vn = vn_ref[:, h * HEAD_DIM:(h + 1) * HEAD_DIM]
        if len(kc_ref.shape) == 3:
            kc = kc_ref[:, h, :].astype(BF16)
            vc = vc_ref[:, h, :].astype(BF16)
        else:
            kc = kc_ref[:, h * dq:(h + 1) * dq]
            vc = vc_ref[:, h * HEAD_DIM:(h + 1) * HEAD_DIM]
        s_n = _dot_nt(q, kn)
        s_c = _dot_nt(q, kc)
        if kind == "sb":
            a_n, carry = _sb_weights(s_n, jnp.zeros((tq, 1), F32), mask, _tri_lower(tq))
            a_c, _ = _sb_weights(s_c, carry, None, _tri_lower(min(TRI, past)))
            out = _dot(a_n.astype(BF16), vn) + _dot(a_c.astype(BF16), vc)
        else:
            if kind == "fox":
                fref = fn_ref[h][:, :1]
                s_n = s_n + (fref - fn_ref[h]) * LOG2E
                s_c = s_c + (fref - fc_ref[h]) * LOG2E
            s_n = jnp.where(mask, s_n, NEG_INF)
            m0 = jnp.full((tq, 1), NEG_INF, F32)
            z0 = jnp.zeros((tq, 1), F32)
            m, l, acc = _softmax_tile(s_n, vn, m0, z0, jnp.zeros((tq, HEAD_DIM), F32))
            m, l, acc = _softmax_tile(s_c, vc, m, l, acc)
            out = acc / l
        o_ref[:, h * HEAD_DIM:(h + 1) * HEAD_DIM] = out


def attn_decode(kind, q, kn, vn, kc, vc, layer=None, fn=None, fc=None):
    native = layer is not None
    nb, past = (kc.shape[1], kc.shape[2]) if native else (kc.shape[0], kc.shape[1])
    tq = q.shape[0] // nb
    new = lambda a: pl.BlockSpec((tq, a.shape[1]), lambda b: (b, 0))
    if native:
        old = lambda a: pl.BlockSpec((None, None, past, N_HEADS, HEAD_DIM), lambda b: (layer, b, 0, 0, 0))
    else:
        old = lambda a: pl.BlockSpec((None, past, a.shape[2]), lambda b: (b, 0, 0))
    in_specs = [new(q), new(kn), new(vn), old(kc), old(vc)]
    args = [q, kn, vn, kc, vc]
    if kind == "fox":
        in_specs += [pl.BlockSpec((None, N_HEADS, 1, tq), lambda b: (b, 0, 0, 0)),
                     pl.BlockSpec((None, N_HEADS, 1, past), lambda b: (b, 0, 0, 0))]
        args += [fn, fc]
    return pl.pallas_call(
        functools.partial(_attn_decode_kernel, kind),
        grid=(nb,),
        in_specs=in_specs,
        out_specs=pl.BlockSpec((tq, W_GROUP), lambda b: (b, 0)),
        out_shape=jax.ShapeDtypeStruct((nb * tq, W_GROUP), F32),
        compiler_params=_params("parallel"),
        name="attn_decode_" + kind,
    )(*args)


def _out_proj_kernel(oa_ref, ob_ref, oc_ref, ga_ref, gb_ref, gc_ref, w_ref, x_ref, o_ref):
    acc = x_ref[...]
    for g, (o, gn) in enumerate(((oa_ref, ga_ref), (ob_ref, gb_ref), (oc_ref, gc_ref))):
        y = _rms(o[...], gn[...]).astype(BF16)
        acc = acc + _dot(y, w_ref[g * W_GROUP:(g + 1) * W_GROUP, :])
    o_ref[...] = acc


def out_proj(oa, ob, oc, p, x, l):
    m = x.shape[0]
    tm = _pick_tile(m, 512)
    row = lambda w: pl.BlockSpec((tm, w), lambda i: (i, 0))
    full = lambda a: pl.BlockSpec(a.shape, lambda i: (0,) * a.ndim)
    consts = (p["out_norm_a"], p["out_norm_b"], p["out_norm_c"], p["w_out"])
    w_spec = pl.BlockSpec((None,) + p["w_out"].shape[1:], lambda i: (l, 0, 0))
    return pl.pallas_call(
        _out_proj_kernel,
        grid=(m // tm,),
        in_specs=[row(W_GROUP)] * 3 + [full(a) for a in consts[:3]] + [w_spec, row(D_MODEL)],
        out_specs=row(D_MODEL),
        out_shape=jax.ShapeDtypeStruct((m, D_MODEL), F32),
        compiler_params=_params("parallel"),
        name="out_proj",
    )(oa, ob, oc, *consts, x)


def _ffn_kernel(x_ref, g_ref, wg_ref, wu_ref, wd_ref, o_ref, h_ref):
    @pl.when(pl.program_id(1) == 0)
    def _():
        x = x_ref[...]
        h_ref[...] = _rms(x, g_ref[...]).astype(BF16)
        o_ref[...] = x

    h = h_ref[...]
    gate = _dot(h, wg_ref[...])
    up = _dot(h, wu_ref[...])
    act = (gate * jax.nn.sigmoid(gate) * up).astype(BF16)
    o_ref[...] += _dot(act, wd_ref[...])


def ffn(x, g, w_gu, w_down, l):
    m = x.shape[0]
    tm = _pick_tile(m, 1024)
    tf = 512
    nf = D_FF // tf
    return pl.pallas_call(
        _ffn_kernel,
        grid=(m // tm, nf),
        in_specs=[pl.BlockSpec((tm, D_MODEL), lambda i, j: (i, 0)),
                  pl.BlockSpec((1, D_MODEL), lambda i, j: (0, 0)),
                  pl.BlockSpec((None, D_MODEL, tf), lambda i, j: (l, 0, j)),
                  pl.BlockSpec((None, D_MODEL, tf), lambda i, j: (l, 0, j + nf)),
                  pl.BlockSpec((None, tf, D_MODEL), lambda i, j: (l, j, 0))],
        out_specs=pl.BlockSpec((tm, D_MODEL), lambda i, j: (i, 0)),
        out_shape=jax.ShapeDtypeStruct((m, D_MODEL), F32),
        scratch_shapes=[pltpu.VMEM((tm, D_MODEL), BF16)],
        compiler_params=_params("parallel", "arbitrary"),
        name="ffn",
    )(x, g, w_gu, w_gu, w_down)


def _dup_rope(r):
    half = MLA_ROPE_DIM // 2
    return jnp.concatenate([r, r[..., half:], r[..., :half]], axis=-1)


def _prep_layer(l, fox_f_bias, fox_q_norm, fox_k_norm, mla_qa_norm, mla_w_uq, mla_kva_norm, mla_w_ukv,
                mla_q_norm, mla_k_norm, out_norm_a, out_norm_b, out_norm_c, w_in, w_out, w_gu, w_down,
                norm_mix, norm_ffn):
    uq = mla_w_uq[l].reshape(MLA_Q_RANK, N_HEADS, MLA_QK_DIM)
    uq = jnp.concatenate([uq[..., :MLA_NOPE_DIM], _dup_rope(uq[..., MLA_NOPE_DIM:])], axis=-1)
    ukv = mla_w_ukv[l].reshape(MLA_KV_RANK, N_HEADS, 2 * LANES)
    ukv = jnp.concatenate([ukv[..., :LANES].reshape(MLA_KV_RANK, W_GROUP),
                           ukv[..., LANES:].reshape(MLA_KV_RANK, W_GROUP)], axis=1)

    def gain256(g):
        return jnp.concatenate([g[:MLA_NOPE_DIM], _dup_rope(g[MLA_NOPE_DIM:])])[None, :]

    row = lambda a: a[l][None, :]
    return {
        "norm_mix": row(norm_mix), "w_in": w_in,
        "fox_q_norm": row(fox_q_norm), "fox_k_norm": row(fox_k_norm),
        "fox_f_bias": jnp.pad(fox_f_bias[l], (0, LANES - N_HEADS))[None, :],
        "mla_qa_norm": row(mla_qa_norm),
        "mla_w_uq": uq.reshape(MLA_Q_RANK, N_HEADS * MLA_HEAD_PAD).astype(BF16),
        "mla_q_gain": gain256(mla_q_norm[l]) * (MLA_QK_DIM ** -0.5 * LOG2E),
        "mla_kva_norm": row(mla_kva_norm),
        "mla_w_ukv": ukv.astype(BF16),
        "mla_k_gain": gain256(mla_k_norm[l]),
        "out_norm_a": row(out_norm_a), "out_norm_b": row(out_norm_b), "out_norm_c": row(out_norm_c),
        "w_out": w_out,
        "norm_ffn": row(norm_ffn), "w_gu": w_gu, "w_down": w_down,
    }


def _rope_tables(pos):
    half = MLA_ROPE_DIM // 2
    inv_freq = ROPE_THETA ** (-(jnp.arange(half, dtype=F32) / half))
    ang = pos.astype(F32)[:, None] * inv_freq[None, :]
    cos, sin = jnp.cos(ang), jnp.sin(ang)
    zero = jnp.zeros_like(cos)
    return (jnp.concatenate([cos, cos, zero, zero], axis=1),
            jnp.concatenate([-sin, sin, zero, zero], axis=1))


def _layer(x, caches, l, depth, p, tabs, stacks):
    nb, t, _ = x.shape
    m = nb * t
    x2 = x.reshape(m, D_MODEL)
    prompt = caches is None
    outs = inproj_post(x2, p, *tabs["q"], with_kt=prompt, l=l, depth=depth, stacks=stacks)
    stacks = outs[:5]
    qa, vab, lf, qb, vbb, qc, krd, k2a, k2b = outs[5:]
    if prompt:
        kct, vc_new = kexp(stacks[4], krd, p, *tabs["q"], transposed=True, layer=l)
        f = cumsum_rows(lf.reshape(nb, t, LANES))
        oa = attn_prompt("fox", qa, k2a, vab, f[0, :N_HEADS])
        ob = attn_prompt("sb", qb, k2b, vbb)
        oc = attn_prompt("mla", qc, kct, vc_new)
    else:
        c_fk, c_fv, c_lf, c_sk, c_sv, c_ckv, c_kr = caches
        pl_ = c_fk.shape[2]
        kc_new, vc_new = kexp(stacks[4], krd, p, *tabs["q"], transposed=False, layer=l)
        lf_all = jnp.concatenate(
            [jnp.pad(c_lf[l], ((0, 0), (0, 0), (0, LANES - N_HEADS))), lf.reshape(nb, t, LANES)], axis=1)
        n_pad = -(-(pl_ + t) // LANES) * LANES
        lf_all = jnp.pad(lf_all, ((0, 0), (0, n_pad - pl_ - t), (0, 0)))
        f = cumsum_rows(lf_all)[:, :N_HEADS, None, :]
        kc_old, vc_old = kexp(c_ckv.reshape(c_ckv.shape[0], nb * pl_, MLA_KV_RANK),
                              _dup_rope(c_kr[l]).reshape(nb * pl_, LANES), p, *tabs["kc"],
                              transposed=False, layer=l)
        oa = attn_decode("fox", qa, k2a, vab, c_fk, c_fv, layer=l, fn=f[..., pl_:pl_ + t], fc=f[..., :pl_])
        ob = attn_decode("sb", qb, k2b, vbb, c_sk, c_sv, layer=l)
        oc = attn_decode("mla", qc, kc_new, vc_new, kc_old.reshape(nb, pl_, -1), vc_old.reshape(nb, pl_, -1))
    x2 = out_proj(oa, ob, oc, p, x2, l)
    x2 = ffn(x2, p["norm_ffn"], p["w_gu"], p["w_down"], l)
    small = (lf[:, :N_HEADS].reshape(nb, t, N_HEADS), krd[:, :MLA_ROPE_DIM].reshape(nb, t, MLA_ROPE_DIM))
    return x2.reshape(nb, t, D_MODEL), stacks, small


def kernel(x_prompt, x_sample, cache_fox_k, cache_fox_v, cache_fox_logf, cache_sb_k, cache_sb_v, cache_mla_ckv, cache_mla_krope, norm_mix, w_in, fox_f_bias, fox_q_norm, fox_k_norm, mla_qa_norm, mla_w_uq, mla_kva_norm, mla_w_ukv, mla_q_norm, mla_k_norm, out_norm_a, out_norm_b, out_norm_c, w_out, norm_ffn, w_gu, w_down):
    depth = w_in.shape[0]
    t_p = x_prompt.shape[1]
    nb_s, t_s = x_sample.shape[0], x_sample.shape[1]
    past_len = cache_fox_k.shape[2]
    tabs_p = {"q": _rope_tables(jnp.arange(t_p, dtype=jnp.int32))}
    tabs_s = {"q": _rope_tables(jnp.tile(past_len + jnp.arange(t_s, dtype=jnp.int32), nb_s)),
              "kc": _rope_tables(jnp.tile(jnp.arange(past_len, dtype=jnp.int32), nb_s))}
    caches = (cache_fox_k, cache_fox_v, cache_fox_logf, cache_sb_k, cache_sb_v, cache_mla_ckv, cache_mla_krope)
    y_p, y_s = x_prompt, x_sample
    rows_p, rows_s = [], []
    def new_stacks(m):
        heads = tuple(jnp.zeros((depth, m, N_HEADS, HEAD_DIM), F32) for _ in range(4))
        return heads + (jnp.zeros((depth, m, MLA_KV_RANK), F32),)

    stacks_p = new_stacks(x_prompt.shape[0] * t_p)
    stacks_s = new_stacks(nb_s * t_s)
    w_in_b = w_in_prep(w_in)
    w_out_b, w_gu_b, w_down_b = w_out.astype(BF16), w_gu.astype(BF16), w_down.astype(BF16)
    for l in range(depth):
        p = _prep_layer(l, fox_f_bias, fox_q_norm, fox_k_norm, mla_qa_norm, mla_w_uq, mla_kva_norm, mla_w_ukv,
                        mla_q_norm, mla_k_norm, out_norm_a, out_norm_b, out_norm_c, w_in_b, w_out_b, w_gu_b,
                        w_down_b, norm_mix, norm_ffn)
        y_p, stacks_p, r_p = _layer(y_p, None, l, depth, p, tabs_p, stacks_p)
        y_s, stacks_s, r_s = _layer(y_s, caches, l, depth, p, tabs_s, stacks_s)
        rows_p.append(r_p)
        rows_s.append(r_s)

    def assemble(stacks, rows, nb, t):
        fk, fv, sk, sv, ckv = stacks
        heads = lambda a: a.reshape(depth, nb, t, N_HEADS, HEAD_DIM)
        return (heads(fk), heads(fv), jnp.stack([r[0] for r in rows], axis=0), heads(sk), heads(sv),
                ckv.reshape(depth, nb, t, MLA_KV_RANK), jnp.stack([r[1] for r in rows], axis=0))

    return ((y_p, y_s) + assemble(stacks_p, rows_p, x_prompt.shape[0], t_p)
            + assemble(stacks_s, rows_s, nb_s, t_s))
```

```python
import functools
import math

import jax
import jax.numpy as jnp
from jax import lax
from jax.experimental import pallas as pl
from jax.experimental.pallas import tpu as pltpu

D_MODEL = 2048
CHUNK = 64
HEAD_DIM = 128
N_HEADS = 4
W_GROUP = N_HEADS * HEAD_DIM
MLA_Q_RANK = 512
MLA_KV_RANK = 256
MLA_NOPE_DIM = 128
MLA_ROPE_DIM = 64
MLA_QK_DIM = MLA_NOPE_DIM + MLA_ROPE_DIM
MLA_HEAD_PAD = 256
ROPE_THETA = 10000.0
D_FF = 5632
RMS_EPS = 1e-6
NEG_INF = -1e30
LOG2E = math.log2(math.e)
N_IN_PAD = 4096
LANES = 128
TRI = 256
ATTN_TILE = 512
INPROJ_ROWS = 256
PROJ_CHUNK = 1024
ROW_GROUP = 128
STABILISER_LIMIT = 48.0
UNDERFLOW_BITS = 160.0
VMEM_LIMIT = 56 * 1024 * 1024

C_QA, C_KA, C_VA, C_QB, C_KB, C_VB, C_CQ, C_CKV, C_KR, C_FA = (
    0, 512, 1024, 1536, 2048, 2560, 3072, 3584, 3840, 3968)

BF16 = jnp.bfloat16
F32 = jnp.float32


def _params(*sem):
    return pltpu.CompilerParams(dimension_semantics=sem, vmem_limit_bytes=VMEM_LIMIT)


def _pick_tile(n, pref):
    t = min(n, pref)
    while n % t:
        t //= 2
    return t


def _rms(x, g):
    return x * lax.rsqrt(jnp.mean(x * x, axis=-1, keepdims=True) + RMS_EPS) * g


def _dot(a, b):
    return jnp.dot(a, b, preferred_element_type=F32)


def _dot_nt(a, b):
    return lax.dot_general(a, b, (((1,), (1,)), ((), ())), preferred_element_type=F32)


def _split2(x):
    hi = x.astype(BF16)
    lo = (x - hi.astype(F32)).astype(BF16)
    return hi, lo


def _neg_abs(x):
    bits = lax.bitcast_convert_type(x, jnp.uint32) | jnp.uint32(0x80000000)
    return lax.bitcast_convert_type(bits, F32)


def _split3(x):
    a = x.astype(BF16)
    r = x - a.astype(F32)
    b = r.astype(BF16)
    c = (r - b.astype(F32)).astype(BF16)
    return a, b, c


def _log_sigmoid(x):
    return -(jnp.maximum(-x, 0.0) + jnp.log1p(jnp.exp(-jnp.abs(x))))


def _log2_keep(z2):
    return -(jnp.maximum(z2, 0.0) + jnp.log2(1.0 + jnp.exp2(_neg_abs(z2))))


def _w_in_prep_kernel(w_ref, o_ref):
    rows = w_ref.shape[0]
    fa0 = C_QB
    o_ref[:, :C_QB] = w_ref[:, :fa0].astype(BF16)
    o_ref[:, C_QB:C_KR] = w_ref[:, fa0 + N_HEADS:C_KR + N_HEADS].astype(BF16)
    kr = w_ref[:, C_KR + N_HEADS:C_KR + N_HEADS + MLA_ROPE_DIM]
    half = MLA_ROPE_DIM // 2
    o_ref[:, C_KR:C_FA] = jnp.concatenate([kr, kr[:, half:], kr[:, :half]], axis=1).astype(BF16)
    fa = w_ref[:, fa0:fa0 + N_HEADS]
    o_ref[:, C_FA:] = jnp.concatenate([fa, jnp.zeros((rows, LANES - N_HEADS), F32)], axis=1).astype(BF16)


def w_in_prep(w_in):
    depth, d, n = w_in.shape
    tr = 256
    return pl.pallas_call(
        _w_in_prep_kernel,
        grid=(depth, d // tr),
        in_specs=[pl.BlockSpec((None, tr, n), lambda l, i: (l, i, 0))],
        out_specs=pl.BlockSpec((None, tr, N_IN_PAD), lambda l, i: (l, i, 0)),
        out_shape=jax.ShapeDtypeStruct((depth, d, N_IN_PAD), BF16),
        compiler_params=_params("parallel", "parallel"),
        name="w_in_prep",
    )(w_in)


def _rope_chunk(y2, cos, sin):
    return y2 * cos + pltpu.roll(y2, 64, 1) * sin


def _rope_lane_mask():
    return lax.broadcasted_iota(jnp.int32, (1, LANES), 1) < MLA_ROPE_DIM


def _inproj_post_kernel(with_kt, n_alias, x_ref, g_ref, w_ref, fqg_ref, fkg_ref, fb_ref, qag_ref, wuq_ref, qg_ref,
                        kvg_ref, cos_ref, sin_ref, *refs):
    refs = refs[n_alias:]
    (ka_o, va_o, kb_o, vb_o, ckv_o, qa_o, vab_o, lf_o, qb_o, vbb_o, qc_o, kr_o, k2a_o, k2b_o,
     proj_ref) = refs
    hn = _rms(x_ref[...], g_ref[...]).astype(BF16)
    for c in range(N_IN_PAD // PROJ_CHUNK):
        cols = slice(c * PROJ_CHUNK, (c + 1) * PROJ_CHUNK)
        proj_ref[:, cols] = _dot(hn, w_ref[:, cols])
    scale = HEAD_DIM ** -0.5 * LOG2E
    va = proj_ref[:, C_VA:C_VA + W_GROUP]
    vb = proj_ref[:, C_VB:C_VB + W_GROUP]
    for h in range(N_HEADS):
        sl = slice(h * HEAD_DIM, (h + 1) * HEAD_DIM)
        qa = proj_ref[:, C_QA + h * HEAD_DIM:C_QA + (h + 1) * HEAD_DIM]
        qa_o[:, sl] = (_rms(qa, fqg_ref[...]) * scale).astype(BF16)
        ka = _rms(proj_ref[:, C_KA + h * HEAD_DIM:C_KA + (h + 1) * HEAD_DIM], fkg_ref[...])
        kb = proj_ref[:, C_KB + h * HEAD_DIM:C_KB + (h + 1) * HEAD_DIM]
        ka_o[:, h, :] = ka
        kb_o[:, h, :] = kb
        va_o[:, h, :] = va[:, sl]
        vb_o[:, h, :] = vb[:, sl]
        if with_kt:
            k2a_o[h, 0] = ka.T.astype(BF16)
            k2b_o[h, 0] = kb.T.astype(BF16)
        else:
            k2a_o[:, sl] = ka.astype(BF16)
            k2b_o[:, sl] = kb.astype(BF16)
    vab_o[...] = va.astype(BF16)
    qb_o[...] = (proj_ref[:, C_QB:C_QB + W_GROUP] * scale).astype(BF16)
    vbb_o[...] = vb.astype(BF16)

    lane = lax.broadcasted_iota(jnp.int32, (1, LANES), 1)
    lf = _log_sigmoid(proj_ref[:, C_FA:C_FA + LANES] + fb_ref[...])
    lf_o[...] = jnp.where(lane < N_HEADS, lf, 0.0)

    cqn = _rms(proj_ref[:, C_CQ:C_CQ + MLA_Q_RANK], qag_ref[...]).astype(BF16)
    qc = _dot(cqn, wuq_ref[...])
    rmask = _rope_lane_mask()
    cos = cos_ref[...]
    sin = sin_ref[...]
    for h in range(N_HEADS):
        c1 = qc[:, h * MLA_HEAD_PAD:h * MLA_HEAD_PAD + LANES]
        c2 = qc[:, h * MLA_HEAD_PAD + LANES:(h + 1) * MLA_HEAD_PAD]
        ss = (jnp.sum(c1 * c1, axis=-1, keepdims=True)
              + jnp.sum(jnp.where(rmask, c2 * c2, 0.0), axis=-1, keepdims=True))
        rs = lax.rsqrt(ss * (1.0 / MLA_QK_DIM) + RMS_EPS)
        y1 = c1 * rs * qg_ref[:, :LANES]
        y2 = c2 * rs * qg_ref[:, LANES:]
        qc_o[:, h * MLA_HEAD_PAD:h * MLA_HEAD_PAD + LANES] = y1.astype(BF16)
        qc_o[:, h * MLA_HEAD_PAD + LANES:(h + 1) * MLA_HEAD_PAD] = _rope_chunk(y2, cos, sin).astype(BF16)

    ckv_o[...] = _rms(proj_ref[:, C_CKV:C_CKV + MLA_KV_RANK], kvg_ref[...])
    kr_o[...] = proj_ref[:, C_KR:C_KR + LANES]


def inproj_post(x, p, cos, sin, with_kt, l, depth, stacks):
    m = x.shape[0]
    tm = _pick_tile(m, INPROJ_ROWS)
    kt_tile = _pick_tile(m, ATTN_TILE)
    sub = kt_tile // tm
    row = lambda w: pl.BlockSpec((tm, w), lambda i: (i, 0))
    full = lambda a: pl.BlockSpec(a.shape, lambda i: (0,) * a.ndim)
    consts = (p["fox_q_norm"], p["fox_k_norm"], p["fox_f_bias"], p["mla_qa_norm"], p["mla_w_uq"],
              p["mla_q_gain"], p["mla_kva_norm"])
    w_spec = pl.BlockSpec((None, D_MODEL, N_IN_PAD), lambda i: (l, 0, 0), pipeline_mode=pl.Buffered(1))
    heads_spec = pl.BlockSpec((None, tm, N_HEADS, HEAD_DIM), lambda i: (l, i, 0, 0))
    heads_shape = jax.ShapeDtypeStruct((depth, m, N_HEADS, HEAD_DIM), F32)
    out_specs = [heads_spec] * 4 + [pl.BlockSpec((None, tm, MLA_KV_RANK), lambda i: (l, i, 0))]
    out_shape = [heads_shape] * 4 + [jax.ShapeDtypeStruct((depth, m, MLA_KV_RANK), F32)]
    outs = [(W_GROUP, BF16), (W_GROUP, BF16), (LANES, F32), (W_GROUP, BF16), (W_GROUP, BF16),
            (N_HEADS * MLA_HEAD_PAD, BF16), (LANES, F32)]
    out_specs += [row(w) for w, _ in outs]
    out_shape += [jax.ShapeDtypeStruct((m, w), dt) for w, dt in outs]
    for _ in range(2):
        if with_kt:
            out_specs.append(pl.BlockSpec((N_HEADS, 1, HEAD_DIM, tm), lambda i: (0, i // sub, 0, i % sub)))
            out_shape.append(jax.ShapeDtypeStruct((N_HEADS, m // kt_tile, HEAD_DIM, kt_tile), BF16))
        else:
            out_specs.append(row(W_GROUP))
            out_shape.append(jax.ShapeDtypeStruct((m, W_GROUP), BF16))
    n_in = 3 + len(consts) + 2
    stacks = () if stacks is None else tuple(stacks)
    return pl.pallas_call(
        functools.partial(_inproj_post_kernel, with_kt, len(stacks)),
        grid=(m // tm,),
        in_specs=([row(D_MODEL), full(p["norm_mix"]), w_spec] + [full(a) for a in consts]
                  + [row(LANES), row(LANES)] + [pl.BlockSpec(memory_space=pl.ANY)] * len(stacks)),
        out_specs=out_specs,
        out_shape=out_shape,
        scratch_shapes=[pltpu.VMEM((tm, N_IN_PAD), F32)],
        input_output_aliases={n_in + k: k for k in range(len(stacks))},
        compiler_params=_params("parallel"),
        name="inproj_post",
    )(x, p["norm_mix"], p["w_in"], *consts, cos, sin, *stacks)


def _kexp_kernel(transposed, ckv_ref, kr_ref, w_ref, kg_ref, cos_ref, sin_ref, kc_o, vc_o):
    kv = _dot(ckv_ref[...].astype(BF16), w_ref[...])
    kr = kr_ref[...]
    krsq = jnp.sum(jnp.where(_rope_lane_mask(), kr * kr, 0.0), axis=-1, keepdims=True)
    cos = cos_ref[...]
    sin = sin_ref[...]
    for h in range(N_HEADS):
        n = kv[:, h * LANES:(h + 1) * LANES]
        ss = jnp.sum(n * n, axis=-1, keepdims=True) + krsq
        rs = lax.rsqrt(ss * (1.0 / MLA_QK_DIM) + RMS_EPS)
        k1 = n * rs * kg_ref[:, :LANES]
        k2 = _rope_chunk(kr * rs * kg_ref[:, LANES:], cos, sin)
        if transposed:
            kc_o[h, 0, :LANES, :] = k1.T.astype(BF16)
            kc_o[h, 0, LANES:, :] = k2.T.astype(BF16)
        else:
            kc_o[:, h * MLA_HEAD_PAD:h * MLA_HEAD_PAD + LANES] = k1.astype(BF16)
            kc_o[:, h * MLA_HEAD_PAD + LANES:(h + 1) * MLA_HEAD_PAD] = k2.astype(BF16)
    vc_o[...] = kv[:, W_GROUP:].astype(BF16)


def kexp(ckv, krdup, p, cos, sin, transposed, layer=None):
    m = krdup.shape[0]
    tm = _pick_tile(m, ATTN_TILE)
    row = lambda w: pl.BlockSpec((tm, w), lambda i: (i, 0))
    full = lambda a: pl.BlockSpec(a.shape, lambda i: (0,) * a.ndim)
    ckv_spec = row(MLA_KV_RANK) if layer is None else pl.BlockSpec(
        (None, tm, MLA_KV_RANK), lambda i: (layer, i, 0))
    if transposed:
        kc_spec = pl.BlockSpec((N_HEADS, 1, MLA_HEAD_PAD, tm), lambda i: (0, i, 0, 0))
        kc_shape = jax.ShapeDtypeStruct((N_HEADS, m // tm, MLA_HEAD_PAD, tm), BF16)
    else:
        kc_spec = row(N_HEADS * MLA_HEAD_PAD)
        kc_shape = jax.ShapeDtypeStruct((m, N_HEADS * MLA_HEAD_PAD), BF16)
    return pl.pallas_call(
        functools.partial(_kexp_kernel, transposed),
        grid=(m // tm,),
        in_specs=[ckv_spec, row(LANES), full(p["mla_w_ukv"]), full(p["mla_k_gain"]), row(LANES), row(LANES)],
        out_specs=[kc_spec, row(W_GROUP)],
        out_shape=[kc_shape, jax.ShapeDtypeStruct((m, W_GROUP), BF16)],
        compiler_params=_params("parallel"),
        name="kexp",
    )(ckv, krdup, p["mla_w_ukv"], p["mla_k_gain"], cos, sin)


def _cumsum_kernel(x_ref, o_ref, carry_ref):
    @pl.when(pl.program_id(1) == 0)
    def _():
        carry_ref[...] = jnp.zeros_like(carry_ref)

    t = x_ref.shape[1]
    xt = x_ref[0].T[:8, :]
    ri = lax.broadcasted_iota(jnp.int32, (t, t), 0)
    ci = lax.broadcasted_iota(jnp.int32, (t, t), 1)
    upper = jnp.where(ri <= ci, 1.0, 0.0).astype(BF16)
    a, b, c = _split3(xt)
    f = _dot(a, upper) + _dot(b, upper) + _dot(c, upper) + carry_ref[:, :1]
    o_ref[0] = f
    carry_ref[...] = jnp.broadcast_to(f[:, t - 1:t], carry_ref.shape)


def cumsum_rows(x):
    b, n, _ = x.shape
    t = n if n <= 1280 else _pick_tile(n, 512)
    return pl.pallas_call(
        _cumsum_kernel,
        grid=(b, n // t),
        in_specs=[pl.BlockSpec((1, t, LANES), lambda i, j: (i, j, 0))],
        out_specs=pl.BlockSpec((1, 8, t), lambda i, j: (i, 0, j)),
        out_shape=jax.ShapeDtypeStruct((b, 8, n), F32),
        scratch_shapes=[pltpu.VMEM((8, LANES), F32)],
        compiler_params=_params("parallel", "arbitrary"),
        name="cumsum_rows",
    )(x)


def _tile_mask(kind, qpos, kpos):
    if kind == "fox":
        return kpos <= qpos
    if kind == "sb":
        return kpos < qpos
    shift = CHUNK.bit_length() - 1
    return lax.shift_right_logical(kpos, shift) <= lax.shift_right_logical(qpos, shift)


def _tri_lower(n):
    ri = lax.broadcasted_iota(jnp.int32, (n, n), 0)
    ci = lax.broadcasted_iota(jnp.int32, (n, n), 1)
    return jnp.where(ri >= ci, 1.0, 0.0).astype(BF16)


def _sb_weights(z, carry, mask, tri):
    lk = _log2_keep(z)
    if mask is not None:
        lk = jnp.where(mask, lk, 0.0)
    blk = tri.shape[0]
    nblk = z.shape[1] // blk
    parts = [None] * nblk
    for c in reversed(range(nblk)):
        lkc = lk[:, c * blk:(c + 1) * blk]
        hi, lo = _split2(lkc)
        intra = _dot(hi, tri) + _dot(lo, tri)
        if carry.shape[1] == 1 or carry.shape[1] == blk:
            parts[c] = intra + carry
        else:
            parts[c] = intra + jnp.concatenate([carry] * (blk // carry.shape[1]), axis=1)
        carry = carry + jnp.sum(lkc, axis=-1, keepdims=True)
    r = parts[0] if nblk == 1 else jnp.concatenate(parts, axis=-1)
    a = jnp.exp2(z + r)
    if mask is not None:
        a = jnp.where(mask, a, 0.0)
    return a, carry


def _attn_prompt_kernel(kind, tq, *refs):
    f_ref = al_ref = kmax_ref = None
    if kind == "fox":
        q_ref, kt_ref, v_ref, f_ref, o_ref, m_ref, acc_ref, s_ref, p_ref, al_ref, kmax_ref = refs
    elif kind == "mla":
        q_ref, kt_ref, v_ref, o_ref, m_ref, acc_ref, s_ref, p_ref, al_ref, kmax_ref = refs
    else:
        q_ref, kt_ref, v_ref, o_ref, m_ref, acc_ref, s_ref, p_ref, kmax_ref = refs
    qb = pl.program_id(1)
    rg = min(TRI if kind == "sb" else ROW_GROUP, tq)
    n_rg = tq // rg
    q = q_ref[...]
    ones = jnp.ones((tq, LANES), BF16)
    tri2 = jnp.concatenate([_tri_lower(rg)] * 2, axis=0) if kind == "sb" else None
    fref = f_ref[0, qb][:, :1] if kind == "fox" else None

    def key_block(j):
        return jnp.clip(qb - j, 0, qb)

    def stage_a(j, slot):
        s_ref[slot] = _dot(q, kt_ref[0, key_block(j)])

    def stage_b(j, slot, diagonal, fixed=False):
        brow = (fref - f_ref[0, key_block(j)]) * LOG2E if kind == "fox" else None
        ms, als, prs = [], [], []
        for r in range(n_rg):
            rows = slice(r * rg, (r + 1) * rg)
            kw = (r + 1) * rg if diagonal else tq
            s = s_ref[slot, rows, :kw]
            mask = None
            if diagonal:
                qpos = r * rg + lax.broadcasted_iota(jnp.int32, (rg, kw), 0)
                kpos = lax.broadcasted_iota(jnp.int32, (rg, kw), 1)
                mask = _tile_mask(kind, qpos, kpos)
            m_prev = m_ref[rows, :]
            if kind == "sb":
                plk = jnp.maximum(s, 0.0) + jnp.log2(1.0 + jnp.exp2(_neg_abs(s)))
                if mask is not None:
                    plk = jnp.where(mask, plk, 0.0)
                carry = m_prev
                parts = [None] * (kw // rg)
                for c in reversed(range(kw // rg)):
                    pc = plk[:, c * rg:(c + 1) * rg]
                    hi, lo = _split2(pc)
                    later = _dot(jnp.concatenate([hi, lo], axis=1), tri2)
                    parts[c] = s[:, c * rg:(c + 1) * rg] - later - jnp.concatenate([carry] * (rg // LANES), 1)
                    carry = carry + jnp.sum(pc, axis=-1, keepdims=True)
                a = jnp.exp2(parts[0] if len(parts) == 1 else jnp.concatenate(parts, axis=1))
                if mask is not None:
                    a = jnp.where(mask, a, 0.0)
                pr = a.astype(BF16)
                ms.append(carry)
            else:
                if kind == "fox":
                    s = s + brow[:, :kw]
                if mask is not None:
                    s = jnp.where(mask, s, NEG_INF)
                chunks = [s[:, c * LANES:(c + 1) * LANES] for c in range(kw // LANES)]
                if fixed:
                    m_new = zb[rows, :]
                else:
                    m_cur = jnp.max(functools.reduce(jnp.maximum, chunks), axis=-1, keepdims=True)
                    m_new = jnp.maximum(m_prev, m_cur)
                    als.append(jnp.exp2(m_prev - m_new))
                    ms.append(m_new)
                pr = jnp.concatenate([jnp.exp2(c - m_new).astype(BF16) for c in chunks], axis=1)
            if kw < tq:
                pr = jnp.concatenate([pr, jnp.zeros((rg, tq - kw), BF16)], axis=1)
            prs.append(pr)
        p_ref[slot] = jnp.concatenate(prs, axis=0)
        if not fixed:
            m_ref[...] = jnp.concatenate(ms, axis=0)
            if kind != "sb":
                al_ref[slot] = jnp.concatenate(als, axis=0)

    def stage_c(j, slot, fixed=False):
        start = pl.multiple_of(key_block(j) * tq, tq)
        v = v_ref[pl.ds(start, tq), :]
        if kind == "sb":
            acc_ref[...] += _dot(p_ref[slot], v)
        else:
            pv = _dot(p_ref[slot], jnp.concatenate([v, ones], axis=1))
            if fixed:
                acc_ref[...] += pv
            else:
                al = al_ref[slot]
                acc_ref[...] = jnp.concatenate([al, al], axis=1) * acc_ref[...] + pv

    def step(j, slot, fixed=False):
        stage_b(j, slot, False, fixed)
        stage_a(j + 1, 1 - slot)
        stage_c(j - 1, 1 - slot, fixed)

    @pl.when(qb == 0)
    def _():
        def norm_body(i, mx):
            kt = kt_ref[0, i].astype(F32)
            return jnp.maximum(mx, jnp.max(jnp.sum(kt * kt, axis=0, keepdims=True), axis=1, keepdims=True))

        mx = lax.fori_loop(0, kt_ref.shape[1], norm_body, jnp.zeros((1, 1), F32))
        kmax_ref[...] = jnp.broadcast_to(jnp.sqrt(mx), kmax_ref.shape)

    qf = q.astype(F32)
    zb = jnp.sqrt(jnp.sum(qf * qf, axis=1, keepdims=True)) * kmax_ref[:1, :]

    if kind == "mla":
        def dense(fixed, unroll):
            m_ref[...] = jnp.full(m_ref.shape, NEG_INF, F32)
            acc_ref[...] = jnp.zeros_like(acc_ref)
            stage_a(0, 0)
            stage_b(0, 0, True, fixed)
            stage_a(1, 1)

            def body(t, c):
                for u in range(unroll):
                    step(unroll * t + 1 + u, (1 + u) % 2, fixed)
                return c

            trips = qb // unroll
            lax.fori_loop(0, trips, body, 0)
            for u in range(unroll - 1):
                @pl.when(qb - unroll * trips > u)
                def _():
                    step(unroll * trips + 1 + u, (1 + u) % 2, fixed)

            for slot in range(2):
                @pl.when(qb % 2 == slot)
                def _():
                    stage_c(qb, slot, fixed)

            o_ref[...] = acc_ref[:, :HEAD_DIM] / acc_ref[:, HEAD_DIM:]

        small = jnp.max(zb) < STABILISER_LIMIT

        @pl.when(small)
        def _():
            dense(True, 4)

        @pl.when(jnp.logical_not(small))
        def _():
            dense(False, 2)

        return

    def exhausted(j):
        if kind == "sb":
            return jnp.min(m_ref[...] - zb) > UNDERFLOW_BITS
        brow_max = jnp.max((fref - f_ref[0, key_block(j + 1)]) * LOG2E)
        return jnp.min(m_ref[...] - zb) - brow_max > UNDERFLOW_BITS

    m_ref[...] = jnp.full(m_ref.shape, 0.0 if kind == "sb" else NEG_INF, F32)
    acc_ref[...] = jnp.zeros_like(acc_ref)
    stage_a(0, 0)
    stage_b(0, 0, True)
    stage_a(1, 1)

    if kind == "sb":
        def cond1(c):
            return jnp.logical_and(c[0] <= qb, jnp.logical_not(c[1]))

        def single(c):
            j = c[0]
            step(j, j & 1)
            return j + 1, exhausted(j)

        nxt, _ = lax.while_loop(cond1, single, (jnp.int32(1), exhausted(0)))
        stage_c(nxt - 1, (nxt - 1) & 1)
        o_ref[...] = acc_ref[...]
        return

    def cond(c):
        return jnp.logical_and(c[0] < qb // 2, jnp.logical_not(c[1]))

    def pair(c):
        t = c[0]
        step(1 + 2 * t, 1)
        step(2 + 2 * t, 0)
        return t + 1, exhausted(2 + 2 * t)

    pairs, done = lax.while_loop(cond, pair, (jnp.int32(0), exhausted(0)))
    last = 2 * pairs
    tail = jnp.logical_and(jnp.logical_not(done), last != qb)

    @pl.when(tail)
    def _():
        step(qb, 1)
        stage_c(qb, 1)

    @pl.when(jnp.logical_not(tail))
    def _():
        stage_c(last, 0)

    o_ref[...] = acc_ref[:, :HEAD_DIM] / acc_ref[:, HEAD_DIM:]


def attn_prompt(kind, q, kt, v, f=None):
    t = q.shape[0]
    dq = q.shape[1] // N_HEADS
    tq = kt.shape[3]
    nq = t // tq
    in_specs = [pl.BlockSpec((tq, dq), lambda h, i: (i, h)),
                pl.BlockSpec((1, nq, dq, tq), lambda h, i: (h, 0, 0, 0)),
                pl.BlockSpec((t, HEAD_DIM), lambda h, i: (0, h))]
    args = [q, kt, v]
    if kind == "fox":
        in_specs.append(pl.BlockSpec((1, nq, 1, tq), lambda h, i: (h, 0, 0, 0)))
        args.append(f.reshape(N_HEADS, nq, 1, tq))
    acc_w = HEAD_DIM if kind == "sb" else 2 * HEAD_DIM
    scratch = [pltpu.VMEM((tq, LANES), F32), pltpu.VMEM((tq, acc_w), F32),
               pltpu.VMEM((2, tq, tq), F32), pltpu.VMEM((2, tq, tq), BF16)]
    if kind != "sb":
        scratch.append(pltpu.VMEM((2, tq, LANES), F32))
    scratch.append(pltpu.VMEM((8, LANES), F32))
    return pl.pallas_call(
        functools.partial(_attn_prompt_kernel, kind, tq),
        grid=(N_HEADS, nq),
        in_specs=in_specs,
        out_specs=pl.BlockSpec((tq, HEAD_DIM), lambda h, i: (i, h)),
        out_shape=jax.ShapeDtypeStruct((t, W_GROUP), F32),
        scratch_shapes=scratch,
        compiler_params=_params("parallel", "arbitrary"),
        name="attn_prompt_" + kind,
    )(*args)


def _softmax_tile(s, v, m_prev, l_prev, acc_prev):
    m_new = jnp.maximum(m_prev, jnp.max(s, axis=-1, keepdims=True))
    alpha = jnp.exp2(m_prev - m_new)
    pr = jnp.exp2(s - m_new)
    l_new = alpha * l_prev + jnp.sum(pr, axis=-1, keepdims=True)
    acc_new = alpha * acc_prev + _dot(pr.astype(BF16), v)
    return m_new, l_new, acc_new


def _attn_decode_kernel(kind, *refs):
    if kind == "fox":
        q_ref, kn_ref, vn_ref, kc_ref, vc_ref, fn_ref, fc_ref, o_ref = refs
    else:
        q_ref, kn_ref, vn_ref, kc_ref, vc_ref, o_ref = refs
    tq = q_ref.shape[0]
    dq = q_ref.shape[1] // N_HEADS
    past = kc_ref.shape[0]
    qpos = past + lax.broadcasted_iota(jnp.int32, (tq, tq), 0)
    kpos = past + lax.broadcasted_iota(jnp.int32, (tq, tq), 1)
    mask = _tile_mask(kind, qpos, kpos)
    for h in range(N_HEADS):
        q = q_ref[:, h * dq:(h + 1) * dq]
        kn = kn_ref[:, h * dq:(h + 1) * dq]
        vn = vn_ref[:, h * HEAD_DIM:(h + 1) * HEAD_DIM]
        if len(kc_ref.shape) == 3:
            kc = kc_ref[:, h, :].astype(BF16)
            vc = vc_ref[:, h, :].astype(BF16)
        else:
            kc = kc_ref[:, h * dq:(h + 1) * dq]
            vc = vc_ref[:, h * HEAD_DIM:(h + 1) * HEAD_DIM]
        s_n = _dot_nt(q, kn)
        s_c = _dot_nt(q, kc)
        if kind == "sb":
            a_n, carry = _sb_weights(s_n, jnp.zeros((tq, 1), F32), mask, _tri_lower(tq))
            a_c, _ = _sb_weights(s_c, carry, None, _tri_lower(min(TRI, past)))
            out = _dot(a_n.astype(BF16), vn) + _dot(a_c.astype(BF16), vc)
        else:
            if kind == "fox":
                fref = fn_ref[h][:, :1]
                s_n = s_n + (fref - fn_ref[h]) * LOG2E
                s_c = s_c + (fref - fc_ref[h]) * LOG2E
            s_n = jnp.where(mask, s_n, NEG_INF)
            m0 = jnp.full((tq, 1), NEG_INF, F32)
            z0 = jnp.zeros((tq, 1), F32)
            m, l, acc = _softmax_tile(s_n, vn, m0, z0, jnp.zeros((tq, HEAD_DIM), F32))
            m, l, acc = _softmax_tile(s_c, vc, m, l, acc)
            out = acc / l
        o_ref[:, h * HEAD_DIM:(h + 1) * HEAD_DIM] = out


def attn_decode(kind, q, kn, vn, kc, vc, layer=None, fn=None, fc=None):
    native = layer is not None
    nb, past = (kc.shape[1], kc.shape[2]) if native else (kc.shape[0], kc.shape[1])
    tq = q.shape[0] // nb
    new = lambda a: pl.BlockSpec((tq, a.shape[1]), lambda b: (b, 0))
    if native:
        old = lambda a: pl.BlockSpec((None, None, past, N_HEADS, HEAD_DIM), lambda b: (layer, b, 0, 0, 0))
    else:
        old = lambda a: pl.BlockSpec((None, past, a.shape[2]), lambda b: (b, 0, 0))
    in_specs = [new(q), new(kn), new(vn), old(kc), old(vc)]
    args = [q, kn, vn, kc, vc]
    if kind == "fox":
        in_specs += [pl.BlockSpec((None, N_HEADS, 1, tq), lambda b: (b, 0, 0, 0)),
                     pl.BlockSpec((None, N_HEADS, 1, past), lambda b: (b, 0, 0, 0))]
        args += [fn, fc]
    return pl.pallas_call(
        functools.partial(_attn_decode_kernel, kind),
        grid=(nb,),
        in_specs=in_specs,
        out_specs=pl.BlockSpec((tq, W_GROUP), lambda b: (b, 0)),
        out_shape=jax.ShapeDtypeStruct((nb * tq, W_GROUP), F32),
        compiler_params=_params("parallel"),
        name="attn_decode_" + kind,
    )(*args)


def _out_proj_kernel(oa_ref, ob_ref, oc_ref, ga_ref, gb_ref, gc_ref, w_ref, x_ref, o_ref):
    acc = x_ref[...]
    for g, (o, gn) in enumerate(((oa_ref, ga_ref), (ob_ref, gb_ref), (oc_ref, gc_ref))):
        y = _rms(o[...], gn[...]).astype(BF16)
        acc = acc + _dot(y, w_ref[g * W_GROUP:(g + 1) * W_GROUP, :])
    o_ref[...] = acc


def out_proj(oa, ob, oc, p, x, l):
    m = x.shape[0]
    tm = _pick_tile(m, 512)
    row = lambda w: pl.BlockSpec((tm, w), lambda i: (i, 0))
    full = lambda a: pl.BlockSpec(a.shape, lambda i: (0,) * a.ndim)
    consts = (p["out_norm_a"], p["out_norm_b"], p["out_norm_c"], p["w_out"])
    w_spec = pl.BlockSpec((None,) + p["w_out"].shape[1:], lambda i: (l, 0, 0))
    return pl.pallas_call(
        _out_proj_kernel,
        grid=(m // tm,),
        in_specs=[row(W_GROUP)] * 3 + [full(a) for a in consts[:3]] + [w_spec, row(D_MODEL)],
        out_specs=row(D_MODEL),
        out_shape=jax.ShapeDtypeStruct((m, D_MODEL), F32),
        compiler_params=_params("parallel"),
        name="out_proj",
    )(oa, ob, oc, *consts, x)


def _ffn_kernel(x_ref, g_ref, wg_ref, wu_ref, wd_ref, o_ref, h_ref):
    @pl.when(pl.program_id(1) == 0)
    def _():
        x = x_ref[...]
        h_ref[...] = _rms(x, g_ref[...]).astype(BF16)
        o_ref[...] = x

    h = h_ref[...]
    gate = _dot(h, wg_ref[...])
    up = _dot(h, wu_ref[...])
    act = (gate * jax.nn.sigmoid(gate) * up).astype(BF16)
    o_ref[...] += _dot(act, wd_ref[...])


def ffn(x, g, w_gu, w_down, l):
    m = x.shape[0]
    tm = _pick_tile(m, 1024)
    tf = 512
    nf = D_FF // tf
    return pl.pallas_call(
        _ffn_kernel,
        grid=(m // tm, nf),
        in_specs=[pl.BlockSpec((tm, D_MODEL), lambda i, j: (i, 0)),
                  pl.BlockSpec((1, D_MODEL), lambda i, j: (0, 0)),
                  pl.BlockSpec((None, D_MODEL, tf), lambda i, j: (l, 0, j)),
                  pl.BlockSpec((None, D_MODEL, tf), lambda i, j: (l, 0, j + nf)),
                  pl.BlockSpec((None, tf, D_MODEL), lambda i, j: (l, j, 0))],
        out_specs=pl.BlockSpec((tm, D_MODEL), lambda i, j: (i, 0)),
        out_shape=jax.ShapeDtypeStruct((m, D_MODEL), F32),
        scratch_shapes=[pltpu.VMEM((tm, D_MODEL), BF16)],
        compiler_params=_params("parallel", "arbitrary"),
        name="ffn",
    )(x, g, w_gu, w_gu, w_down)


def _dup_rope(r):
    half = MLA_ROPE_DIM // 2
    return jnp.concatenate([r, r[..., half:], r[..., :half]], axis=-1)


def _prep_layer(l, fox_f_bias, fox_q_norm, fox_k_norm, mla_qa_norm, mla_w_uq, mla_kva_norm, mla_w_ukv,
                mla_q_norm, mla_k_norm, out_norm_a, out_norm_b, out_norm_c, w_in, w_out, w_gu, w_down,
                norm_mix, norm_ffn):
    uq = mla_w_uq[l].reshape(MLA_Q_RANK, N_HEADS, MLA_QK_DIM)
    uq = jnp.concatenate([uq[..., :MLA_NOPE_DIM], _dup_rope(uq[..., MLA_NOPE_DIM:])], axis=-1)
    ukv = mla_w_ukv[l].reshape(MLA_KV_RANK, N_HEADS, 2 * LANES)
    ukv = jnp.concatenate([ukv[..., :LANES].reshape(MLA_KV_RANK, W_GROUP),
                           ukv[..., LANES:].reshape(MLA_KV_RANK, W_GROUP)], axis=1)

    def gain256(g):
        return jnp.concatenate([g[:MLA_NOPE_DIM], _dup_rope(g[MLA_NOPE_DIM:])])[None, :]

    row = lambda a: a[l][None, :]
    return {
        "norm_mix": row(norm_mix), "w_in": w_in,
        "fox_q_norm": row(fox_q_norm), "fox_k_norm": row(fox_k_norm),
        "fox_f_bias": jnp.pad(fox_f_bias[l], (0, LANES - N_HEADS))[None, :],
        "mla_qa_norm": row(mla_qa_norm),
        "mla_w_uq": uq.reshape(MLA_Q_RANK, N_HEADS * MLA_HEAD_PAD).astype(BF16),
        "mla_q_gain": gain256(mla_q_norm[l]) * (MLA_QK_DIM ** -0.5 * LOG2E),
        "mla_kva_norm": row(mla_kva_norm),
        "mla_w_ukv": ukv.astype(BF16),
        "mla_k_gain": gain256(mla_k_norm[l]),
        "out_norm_a": row(out_norm_a), "out_norm_b": row(out_norm_b), "out_norm_c": row(out_norm_c),
        "w_out": w_out,
        "norm_ffn": row(norm_ffn), "w_gu": w_gu, "w_down": w_down,
    }


def _rope_tables(pos):
    half = MLA_ROPE_DIM // 2
    inv_freq = ROPE_THETA ** (-(jnp.arange(half, dtype=F32) / half))
    ang = pos.astype(F32)[:, None] * inv_freq[None, :]
    cos, sin = jnp.cos(ang), jnp.sin(ang)
    zero = jnp.zeros_like(cos)
    return (jnp.concatenate([cos, cos, zero, zero], axis=1),
            jnp.concatenate([-sin, sin, zero, zero], axis=1))


def _layer(x, caches, l, depth, p, tabs, stacks):
    nb, t, _ = x.shape
    m = nb * t
    x2 = x.reshape(m, D_MODEL)
    prompt = caches is None
    outs = inproj_post(x2, p, *tabs["q"], with_kt=prompt, l=l, depth=depth, stacks=stacks)
    stacks = outs[:5]
    qa, vab, lf, qb, vbb, qc, krd, k2a, k2b = outs[5:]
    if prompt:
        kct, vc_new = kexp(stacks[4], krd, p, *tabs["q"], transposed=True, layer=l)
        f = cumsum_rows(lf.reshape(nb, t, LANES))
        oa = attn_prompt("fox", qa, k2a, vab, f[0, :N_HEADS])
        ob = attn_prompt("sb", qb, k2b, vbb)
        oc = attn_prompt("mla", qc, kct, vc_new)
    else:
        c_fk, c_fv, c_lf, c_sk, c_sv, c_ckv, c_kr = caches
        pl_ = c_fk.shape[2]
        kc_new, vc_new = kexp(stacks[4], krd, p, *tabs["q"], transposed=False, layer=l)
        lf_all = jnp.concatenate(
            [jnp.pad(c_lf[l], ((0, 0), (0, 0), (0, LANES - N_HEADS))), lf.reshape(nb, t, LANES)], axis=1)
        n_pad = -(-(pl_ + t) // LANES) * LANES
        lf_all = jnp.pad(lf_all, ((0, 0), (0, n_pad - pl_ - t), (0, 0)))
        f = cumsum_rows(lf_all)[:, :N_HEADS, None, :]
        kc_old, vc_old = kexp(c_ckv.reshape(c_ckv.shape[0], nb * pl_, MLA_KV_RANK),
                              _dup_rope(c_kr[l]).reshape(nb * pl_, LANES), p, *tabs["kc"],
                              transposed=False, layer=l)
        oa = attn_decode("fox", qa, k2a, vab, c_fk, c_fv, layer=l, fn=f[..., pl_:pl_ + t], fc=f[..., :pl_])
        ob = attn_decode("sb", qb, k2b, vbb, c_sk, c_sv, layer=l)
        oc = attn_decode("mla", qc, kc_new, vc_new, kc_old.reshape(nb, pl_, -1), vc_old.reshape(nb, pl_, -1))
    x2 = out_proj(oa, ob, oc, p, x2, l)
    x2 = ffn(x2, p["norm_ffn"], p["w_gu"], p["w_down"], l)
    small = (lf[:, :N_HEADS].reshape(nb, t, N_HEADS), krd[:, :MLA_ROPE_DIM].reshape(nb, t, MLA_ROPE_DIM))
    return x2.reshape(nb, t, D_MODEL), stacks, small


def kernel(x_prompt, x_sample, cache_fox_k, cache_fox_v, cache_fox_logf, cache_sb_k, cache_sb_v, cache_mla_ckv, cache_mla_krope, norm_mix, w_in, fox_f_bias, fox_q_norm, fox_k_norm, mla_qa_norm, mla_w_uq, mla_kva_norm, mla_w_ukv, mla_q_norm, mla_k_norm, out_norm_a, out_norm_b, out_norm_c, w_out, norm_ffn, w_gu, w_down):
    depth = w_in.shape[0]
    t_p = x_prompt.shape[1]
    nb_s, t_s = x_sample.shape[0], x_sample.shape[1]
    past_len = cache_fox_k.shape[2]
    tabs_p = {"q": _rope_tables(jnp.arange(t_p, dtype=jnp.int32))}
    tabs_s = {"q": _rope_tables(jnp.tile(past_len + jnp.arange(t_s, dtype=jnp.int32), nb_s)),
              "kc": _rope_tables(jnp.tile(jnp.arange(past_len, dtype=jnp.int32), nb_s))}
    caches = (cache_fox_k, cache_fox_v, cache_fox_logf, cache_sb_k, cache_sb_v, cache_mla_ckv, cache_mla_krope)
    y_p, y_s = x_prompt, x_sample
    rows_p, rows_s = [], []
    def new_stacks(m):
        heads = tuple(jnp.zeros((depth, m, N_HEADS, HEAD_DIM), F32) for _ in range(4))
        return heads + (jnp.zeros((depth, m, MLA_KV_RANK), F32),)

    stacks_p = new_stacks(x_prompt.shape[0] * t_p)
    stacks_s = new_stacks(nb_s * t_s)
    w_in_b = w_in_prep(w_in)
    w_out_b, w_gu_b, w_down_b = w_out.astype(BF16), w_gu.astype(BF16), w_down.astype(BF16)
    for l in range(depth):
        p = _prep_layer(l, fox_f_bias, fox_q_norm, fox_k_norm, mla_qa_norm, mla_w_uq, mla_kva_norm, mla_w_ukv,
                        mla_q_norm, mla_k_norm, out_norm_a, out_norm_b, out_norm_c, w_in_b, w_out_b, w_gu_b,
                        w_down_b, norm_mix, norm_ffn)
        y_p, stacks_p, r_p = _layer(y_p, None, l, depth, p, tabs_p, stacks_p)
        y_s, stacks_s, r_s = _layer(y_s, caches, l, depth, p, tabs_s, stacks_s)
        rows_p.append(r_p)
        rows_s.append(r_s)

    def assemble(stacks, rows, nb, t):
        fk, fv, sk, sv, ckv = stacks
        heads = lambda a: a.reshape(depth, nb, t, N_HEADS, HEAD_DIM)
        return (heads(fk), heads(fv), jnp.stack([r[0] for r in rows], axis=0), heads(sk), heads(sv),
                ckv.reshape(depth, nb, t, MLA_KV_RANK), jnp.stack([r[1] for r in rows], axis=0))

    return ((y_p, y_s) + assemble(stacks_p, rows_p, x_prompt.shape[0], t_p)
            + assemble(stacks_s, rows_s, nb_s, t_s))
```

```python
import functools
import math

import jax
import jax.numpy as jnp
from jax import lax
from jax.experimental import pallas as pl
from jax.experimental.pallas import tpu as pltpu

D_MODEL = 2048
CHUNK = 64
HEAD_DIM = 128
N_HEADS = 4
W_GROUP = N_HEADS * HEAD_DIM
MLA_Q_RANK = 512
MLA_KV_RANK = 256
MLA_NOPE_DIM = 128
MLA_ROPE_DIM = 64
MLA_QK_DIM = MLA_NOPE_DIM + MLA_ROPE_DIM
MLA_HEAD_PAD = 256
ROPE_THETA = 10000.0
D_FF = 5632
RMS_EPS = 1e-6
NEG_INF = -1e30
LOG2E = math.log2(math.e)
N_IN_PAD = 4096
LANES = 128
TRI = 256
ATTN_TILE = 512
SB_TILE = 256
INPROJ_ROWS = 256
PROJ_CHUNK = 1024
ROW_GROUP = 128
STABILISER_LIMIT = 48.0
UNDERFLOW_BITS = 160.0
VMEM_LIMIT = 56 * 1024 * 1024

C_QA, C_KA, C_VA, C_QB, C_KB, C_VB, C_CQ, C_CKV, C_KR, C_FA = (
    0, 512, 1024, 1536, 2048, 2560, 3072, 3584, 3840, 3968)

BF16 = jnp.bfloat16
F32 = jnp.float32


def _params(*sem):
    return pltpu.CompilerParams(dimension_semantics=sem, vmem_limit_bytes=VMEM_LIMIT)


def _pick_tile(n, pref):
    t = min(n, pref)
    while n % t:
        t //= 2
    return t


def _rms(x, g):
    return x * lax.rsqrt(jnp.mean(x * x, axis=-1, keepdims=True) + RMS_EPS) * g


def _dot(a, b):
    return jnp.dot(a, b, preferred_element_type=F32)


def _dot_nt(a, b):
    return lax.dot_general(a, b, (((1,), (1,)), ((), ())), preferred_element_type=F32)


def _split2(x):
    hi = x.astype(BF16)
    lo = (x - hi.astype(F32)).astype(BF16)
    return hi, lo


def _neg_abs(x):
    bits = lax.bitcast_convert_type(x, jnp.uint32) | jnp.uint32(0x80000000)
    return lax.bitcast_convert_type(bits, F32)


def _split3(x):
    a = x.astype(BF16)
    r = x - a.astype(F32)
    b = r.astype(BF16)
    c = (r - b.astype(F32)).astype(BF16)
    return a, b, c


def _log_sigmoid(x):
    return -(jnp.maximum(-x, 0.0) + jnp.log1p(jnp.exp(-jnp.abs(x))))


def _log2_keep(z2):
    return -(jnp.maximum(z2, 0.0) + jnp.log2(1.0 + jnp.exp2(_neg_abs(z2))))


def _w_in_prep_kernel(w_ref, o_ref):
    rows = w_ref.shape[0]
    fa0 = C_QB
    o_ref[:, :C_QB] = w_ref[:, :fa0].astype(BF16)
    o_ref[:, C_QB:C_KR] = w_ref[:, fa0 + N_HEADS:C_KR + N_HEADS].astype(BF16)
    kr = w_ref[:, C_KR + N_HEADS:C_KR + N_HEADS + MLA_ROPE_DIM]
    half = MLA_ROPE_DIM // 2
    o_ref[:, C_KR:C_FA] = jnp.concatenate([kr, kr[:, half:], kr[:, :half]], axis=1).astype(BF16)
    fa = w_ref[:, fa0:fa0 + N_HEADS]
    o_ref[:, C_FA:] = jnp.concatenate([fa, jnp.zeros((rows, LANES - N_HEADS), F32)], axis=1).astype(BF16)


def w_in_prep(w_in):
    depth, d, n = w_in.shape
    tr = 256
    return pl.pallas_call(
        _w_in_prep_kernel,
        grid=(depth, d // tr),
        in_specs=[pl.BlockSpec((None, tr, n), lambda l, i: (l, i, 0))],
        out_specs=pl.BlockSpec((None, tr, N_IN_PAD), lambda l, i: (l, i, 0)),
        out_shape=jax.ShapeDtypeStruct((depth, d, N_IN_PAD), BF16),
        compiler_params=_params("parallel", "parallel"),
        name="w_in_prep",
    )(w_in)


def _rope_chunk(y2, cos, sin):
    return y2 * cos + pltpu.roll(y2, 64, 1) * sin


def _rope_lane_mask():
    return lax.broadcasted_iota(jnp.int32, (1, LANES), 1) < MLA_ROPE_DIM


def _inproj_post_kernel(with_kt, n_alias, x_ref, g_ref, w_ref, fqg_ref, fkg_ref, fb_ref, qag_ref, wuq_ref, qg_ref,
                        kvg_ref, cos_ref, sin_ref, *refs):
    refs = refs[n_alias:]
    (ka_o, va_o, kb_o, vb_o, ckv_o, qa_o, vab_o, lf_o, qb_o, vbb_o, qc_o, kr_o, k2a_o, k2b_o,
     proj_ref) = refs
    hn = _rms(x_ref[...], g_ref[...]).astype(BF16)
    for c in range(N_IN_PAD // PROJ_CHUNK):
        cols = slice(c * PROJ_CHUNK, (c + 1) * PROJ_CHUNK)
        proj_ref[:, cols] = _dot(hn, w_ref[:, cols])
    scale = HEAD_DIM ** -0.5 * LOG2E
    va = proj_ref[:, C_VA:C_VA + W_GROUP]
    vb = proj_ref[:, C_VB:C_VB + W_GROUP]
    for h in range(N_HEADS):
        sl = slice(h * HEAD_DIM, (h + 1) * HEAD_DIM)
        qa = proj_ref[:, C_QA + h * HEAD_DIM:C_QA + (h + 1) * HEAD_DIM]
        qa_o[:, sl] = (_rms(qa, fqg_ref[...]) * scale).astype(BF16)
        ka = _rms(proj_ref[:, C_KA + h * HEAD_DIM:C_KA + (h + 1) * HEAD_DIM], fkg_ref[...])
        kb = proj_ref[:, C_KB + h * HEAD_DIM:C_KB + (h + 1) * HEAD_DIM]
        ka_o[:, h, :] = ka
        kb_o[:, h, :] = kb
        va_o[:, h, :] = va[:, sl]
        vb_o[:, h, :] = vb[:, sl]
        if with_kt:
            k2a_o[h, 0] = ka.T.astype(BF16)
            k2b_o[h, 0] = kb.T.astype(BF16)
        else:
            k2a_o[:, sl] = ka.astype(BF16)
            k2b_o[:, sl] = kb.astype(BF16)
    vab_o[...] = va.astype(BF16)
    qb_o[...] = (proj_ref[:, C_QB:C_QB + W_GROUP] * scale).astype(BF16)
    vbb_o[...] = vb.astype(BF16)

    lane = lax.broadcasted_iota(jnp.int32, (1, LANES), 1)
    lf = _log_sigmoid(proj_ref[:, C_FA:C_FA + LANES] + fb_ref[...])
    lf_o[...] = jnp.where(lane < N_HEADS, lf, 0.0)

    cqn = _rms(proj_ref[:, C_CQ:C_CQ + MLA_Q_RANK], qag_ref[...]).astype(BF16)
    qc = _dot(cqn, wuq_ref[...])
    rmask = _rope_lane_mask()
    cos = cos_ref[...]
    sin = sin_ref[...]
    for h in range(N_HEADS):
        c1 = qc[:, h * MLA_HEAD_PAD:h * MLA_HEAD_PAD + LANES]
        c2 = qc[:, h * MLA_HEAD_PAD + LANES:(h + 1) * MLA_HEAD_PAD]
        ss = (jnp.sum(c1 * c1, axis=-1, keepdims=True)
              + jnp.sum(jnp.where(rmask, c2 * c2, 0.0), axis=-1, keepdims=True))
        rs = lax.rsqrt(ss * (1.0 / MLA_QK_DIM) + RMS_EPS)
        y1 = c1 * rs * qg_ref[:, :LANES]
        y2 = c2 * rs * qg_ref[:, LANES:]
        qc_o[:, h * MLA_HEAD_PAD:h * MLA_HEAD_PAD + LANES] = y1.astype(BF16)
        qc_o[:, h * MLA_HEAD_PAD + LANES:(h + 1) * MLA_HEAD_PAD] = _rope_chunk(y2, cos, sin).astype(BF16)

    ckv_o[...] = _rms(proj_ref[:, C_CKV:C_CKV + MLA_KV_RANK], kvg_ref[...])
    kr_o[...] = proj_ref[:, C_KR:C_KR + LANES]


def inproj_post(x, p, cos, sin, with_kt, l, depth, stacks):
    m = x.shape[0]
    tm = _pick_tile(m, INPROJ_ROWS)
    row =lambda w: pl.BlockSpec((tm, w), lambda i: (i, 0))
    full = lambda a: pl.BlockSpec(a.shape, lambda i: (0,) * a.ndim)
    consts = (p["fox_q_norm"], p["fox_k_norm"], p["fox_f_bias"], p["mla_qa_norm"], p["mla_w_uq"],
              p["mla_q_gain"], p["mla_kva_norm"])
    w_spec = pl.BlockSpec((None, D_MODEL, N_IN_PAD), lambda i: (l, 0, 0), pipeline_mode=pl.Buffered(1))
    heads_spec = pl.BlockSpec((None, tm, N_HEADS, HEAD_DIM), lambda i: (l, i, 0, 0))
    heads_shape = jax.ShapeDtypeStruct((depth, m, N_HEADS, HEAD_DIM), F32)
    out_specs = [heads_spec] * 4 + [pl.BlockSpec((None, tm, MLA_KV_RANK), lambda i: (l, i, 0))]
    out_shape = [heads_shape] * 4 + [jax.ShapeDtypeStruct((depth, m, MLA_KV_RANK), F32)]
    outs = [(W_GROUP, BF16), (W_GROUP, BF16), (LANES, F32), (W_GROUP, BF16), (W_GROUP, BF16),
            (N_HEADS * MLA_HEAD_PAD, BF16), (LANES, F32)]
    out_specs += [row(w) for w, _ in outs]
    out_shape += [jax.ShapeDtypeStruct((m, w), dt) for w, dt in outs]
    for kt_pref in (ATTN_TILE, SB_TILE):
        if with_kt:
            kt_tile = _pick_tile(m, kt_pref)
            sub = kt_tile // tm
            out_specs.append(pl.BlockSpec((N_HEADS, 1, HEAD_DIM, tm),
                                          lambda i, sub=sub: (0, i // sub, 0, i % sub)))
            out_shape.append(jax.ShapeDtypeStruct((N_HEADS, m // kt_tile, HEAD_DIM, kt_tile), BF16))
        else:
            out_specs.append(row(W_GROUP))
            out_shape.append(jax.ShapeDtypeStruct((m, W_GROUP), BF16))
    n_in = 3 + len(consts) + 2
    stacks = () if stacks is None else tuple(stacks)
    return pl.pallas_call(
        functools.partial(_inproj_post_kernel, with_kt, len(stacks)),
        grid=(m // tm,),
        in_specs=([row(D_MODEL), full(p["norm_mix"]), w_spec] + [full(a) for a in consts]
                  + [row(LANES), row(LANES)] + [pl.BlockSpec(memory_space=pl.ANY)] * len(stacks)),
        out_specs=out_specs,
        out_shape=out_shape,
        scratch_shapes=[pltpu.VMEM((tm, N_IN_PAD), F32)],
        input_output_aliases={n_in + k: k for k in range(len(stacks))},
        compiler_params=_params("parallel"),
        name="inproj_post",
    )(x, p["norm_mix"], p["w_in"], *consts, cos, sin, *stacks)


def _kexp_kernel(transposed, ckv_ref, kr_ref, w_ref, kg_ref, cos_ref, sin_ref, kc_o, vc_o):
    kv = _dot(ckv_ref[...].astype(BF16), w_ref[...])
    kr = kr_ref[...]
    krsq = jnp.sum(jnp.where(_rope_lane_mask(), kr * kr, 0.0), axis=-1, keepdims=True)
    cos = cos_ref[...]
    sin = sin_ref[...]
    for h in range(N_HEADS):
        n = kv[:, h * LANES:(h + 1) * LANES]
        ss = jnp.sum(n * n, axis=-1, keepdims=True) + krsq
        rs = lax.rsqrt(ss * (1.0 / MLA_QK_DIM) + RMS_EPS)
        k1 = n * rs * kg_ref[:, :LANES]
        k2 = _rope_chunk(kr * rs * kg_ref[:, LANES:], cos, sin)
        if transposed:
            kc_o[h, 0, :LANES, :] = k1.T.astype(BF16)
            kc_o[h, 0, LANES:, :] = k2.T.astype(BF16)
        else:
            kc_o[:, h * MLA_HEAD_PAD:h * MLA_HEAD_PAD + LANES] = k1.astype(BF16)
            kc_o[:, h * MLA_HEAD_PAD + LANES:(h + 1) * MLA_HEAD_PAD] = k2.astype(BF16)
    vc_o[...] = kv[:, W_GROUP:].astype(BF16)


def kexp(ckv, krdup, p, cos, sin, transposed, layer=None):
    m = krdup.shape[0]
    tm = _pick_tile(m, ATTN_TILE)
    row = lambda w: pl.BlockSpec((tm, w), lambda i: (i, 0))
    full = lambda a: pl.BlockSpec(a.shape, lambda i: (0,) * a.ndim)
    ckv_spec = row(MLA_KV_RANK) if layer is None else pl.BlockSpec(
        (None, tm, MLA_KV_RANK), lambda i: (layer, i, 0))
    if transposed:
        kc_spec = pl.BlockSpec((N_HEADS, 1, MLA_HEAD_PAD, tm), lambda i: (0, i, 0, 0))
        kc_shape = jax.ShapeDtypeStruct((N_HEADS, m // tm, MLA_HEAD_PAD, tm), BF16)
    else:
        kc_spec = row(N_HEADS * MLA_HEAD_PAD)
        kc_shape = jax.ShapeDtypeStruct((m, N_HEADS * MLA_HEAD_PAD), BF16)
    return pl.pallas_call(
        functools.partial(_kexp_kernel, transposed),
        grid=(m // tm,),
        in_specs=[ckv_spec, row(LANES), full(p["mla_w_ukv"]), full(p["mla_k_gain"]), row(LANES), row(LANES)],
        out_specs=[kc_spec, row(W_GROUP)],
        out_shape=[kc_shape, jax.ShapeDtypeStruct((m, W_GROUP), BF16)],
        compiler_params=_params("parallel"),
        name="kexp",
    )(ckv, krdup, p["mla_w_ukv"], p["mla_k_gain"], cos, sin)


def _cumsum_kernel(x_ref, o_ref, carry_ref):
    @pl.when(pl.program_id(1) == 0)
    def _():
        carry_ref[...] = jnp.zeros_like(carry_ref)

    t = x_ref.shape[1]
    xt = x_ref[0].T[:8, :]
    ri = lax.broadcasted_iota(jnp.int32, (t, t), 0)
    ci = lax.broadcasted_iota(jnp.int32, (t, t), 1)
    upper = jnp.where(ri <= ci, 1.0, 0.0).astype(BF16)
    a, b, c = _split3(xt)
    f = _dot(a, upper) + _dot(b, upper) + _dot(c, upper) + carry_ref[:, :1]
    o_ref[0] = f
    carry_ref[...] = jnp.broadcast_to(f[:, t - 1:t], carry_ref.shape)


def cumsum_rows(x):
    b, n, _ = x.shape
    t = n if n <= 1280 else _pick_tile(n, 512)
    return pl.pallas_call(
        _cumsum_kernel,
        grid=(b, n // t),
        in_specs=[pl.BlockSpec((1, t, LANES), lambda i, j: (i, j, 0))],
        out_specs=pl.BlockSpec((1, 8, t), lambda i, j: (i, 0, j)),
        out_shape=jax.ShapeDtypeStruct((b, 8, n), F32),
        scratch_shapes=[pltpu.VMEM((8, LANES), F32)],
        compiler_params=_params("parallel", "arbitrary"),
        name="cumsum_rows",
    )(x)


def _tile_mask(kind, qpos, kpos):
    if kind == "fox":
        return kpos <= qpos
    if kind == "sb":
        return kpos < qpos
    shift = CHUNK.bit_length() - 1
    return lax.shift_right_logical(kpos, shift) <= lax.shift_right_logical(qpos, shift)


def _tri_lower(n):
    ri = lax.broadcasted_iota(jnp.int32, (n, n), 0)
    ci = lax.broadcasted_iota(jnp.int32, (n, n), 1)
    return jnp.where(ri >= ci, 1.0, 0.0).astype(BF16)


def _sb_weights(z, carry, mask, tri):
    lk = _log2_keep(z)
    if mask is not None:
        lk = jnp.where(mask, lk, 0.0)
    blk = tri.shape[0]
    nblk = z.shape[1] // blk
    parts = [None] * nblk
    for c in reversed(range(nblk)):
        lkc = lk[:, c * blk:(c + 1) * blk]
        hi, lo = _split2(lkc)
        intra = _dot(hi, tri) + _dot(lo, tri)
        if carry.shape[1] == 1 or carry.shape[1] == blk:
            parts[c] = intra + carry
        else:
            parts[c] = intra + jnp.concatenate([carry] * (blk // carry.shape[1]), axis=1)
        carry = carry + jnp.sum(lkc, axis=-1, keepdims=True)
    r = parts[0] if nblk == 1 else jnp.concatenate(parts, axis=-1)
    a = jnp.exp2(z + r)
    if mask is not None:
        a = jnp.where(mask, a, 0.0)
    return a, carry


def _attn_prompt_kernel(kind, tq, *refs):
    f_ref = al_ref = kmax_ref = None
    if kind == "fox":
        q_ref, kt_ref, v_ref, f_ref, o_ref, m_ref, acc_ref, s_ref, p_ref, al_ref, kmax_ref = refs
    elif kind == "mla":
        q_ref, kt_ref, v_ref, o_ref, m_ref, acc_ref, s_ref, p_ref, al_ref, kmax_ref = refs
    else:
        q_ref, kt_ref, v_ref, o_ref, m_ref, acc_ref, s_ref, p_ref, kmax_ref = refs
    qb = pl.program_id(1)
    rg = min(TRI if kind == "sb" else ROW_GROUP, tq)
    n_rg = tq // rg
    q = q_ref[...]
    ones = jnp.ones((tq, LANES), BF16)
    tri2 = jnp.concatenate([_tri_lower(rg)] * 2, axis=0) if kind == "sb" else None
    fref = f_ref[0, qb][:, :1] if kind == "fox" else None

    def key_block(j):
        return jnp.clip(qb - j, 0, qb)

    def stage_a(j, slot):
        s_ref[slot] = _dot(q, kt_ref[0, key_block(j)])

    def stage_b(j, slot, diagonal, fixed=False):
        brow = (fref - f_ref[0, key_block(j)]) * LOG2E if kind == "fox" else None
        ms, als, prs = [], [], []
        for r in range(n_rg):
            rows = slice(r * rg, (r + 1) * rg)
            kw = (r + 1) * rg if diagonal else tq
            s = s_ref[slot, rows, :kw]
            mask = None
            if diagonal:
                qpos = r * rg + lax.broadcasted_iota(jnp.int32, (rg, kw), 0)
                kpos = lax.broadcasted_iota(jnp.int32, (rg, kw), 1)
                mask = _tile_mask(kind, qpos, kpos)
            m_prev = m_ref[rows, :]
            if kind == "sb":
                plk = jnp.maximum(s, 0.0) + jnp.log2(1.0 + jnp.exp2(_neg_abs(s)))
                if mask is not None:
                    plk = jnp.where(mask, plk, 0.0)
                carry = m_prev
                parts = [None] * (kw // rg)
                for c in reversed(range(kw // rg)):
                    pc = plk[:, c * rg:(c + 1) * rg]
                    hi, lo = _split2(pc)
                    later = _dot(jnp.concatenate([hi, lo], axis=1), tri2)
                    parts[c] = s[:, c * rg:(c + 1) * rg] - later - jnp.concatenate([carry] * (rg // LANES), 1)
                    carry = carry + jnp.sum(pc, axis=-1, keepdims=True)
                a = jnp.exp2(parts[0] if len(parts) == 1 else jnp.concatenate(parts, axis=1))
                if mask is not None:
                    a = jnp.where(mask, a, 0.0)
                pr = a.astype(BF16)
                ms.append(carry)
            else:
                if kind == "fox":
                    s = s + brow[:, :kw]
                if mask is not None:
                    s = jnp.where(mask, s, NEG_INF)
                chunks = [s[:, c * LANES:(c + 1) * LANES] for c in range(kw // LANES)]
                if fixed:
                    m_new = zb[rows, :]
                else:
                    m_cur = jnp.max(functools.reduce(jnp.maximum, chunks), axis=-1, keepdims=True)
                    m_new = jnp.maximum(m_prev, m_cur)
                    als.append(jnp.exp2(m_prev - m_new))
                    ms.append(m_new)
                pr = jnp.concatenate([jnp.exp2(c - m_new).astype(BF16) for c in chunks], axis=1)
            if kw < tq:
                pr = jnp.concatenate([pr, jnp.zeros((rg, tq - kw), BF16)], axis=1)
            prs.append(pr)
        p_ref[slot] = jnp.concatenate(prs, axis=0)
        if not fixed:
            m_ref[...] = jnp.concatenate(ms, axis=0)
            if kind != "sb":
                al_ref[slot] = jnp.concatenate(als, axis=0)

    def stage_c(j, slot, fixed=False):
        start = pl.multiple_of(key_block(j) * tq, tq)
        v = v_ref[pl.ds(start, tq), :]
        if kind == "sb":
            acc_ref[...] += _dot(p_ref[slot], v)
        else:
            pv = _dot(p_ref[slot], jnp.concatenate([v, ones], axis=1))
            if fixed:
                acc_ref[...] += pv
            else:
                al = al_ref[slot]
                acc_ref[...] = jnp.concatenate([al, al], axis=1) * acc_ref[...] + pv

    def step(j, slot, fixed=False):
        stage_b(j, slot, False, fixed)
        stage_a(j + 1, 1 - slot)
        stage_c(j - 1, 1 - slot, fixed)

    @pl.when(qb == 0)
    def _():
        def norm_body(i, mx):
            kt = kt_ref[0, i].astype(F32)
            return jnp.maximum(mx, jnp.max(jnp.sum(kt * kt, axis=0, keepdims=True), axis=1, keepdims=True))

        mx = lax.fori_loop(0, kt_ref.shape[1], norm_body, jnp.zeros((1, 1), F32))
        kmax_ref[...] = jnp.broadcast_to(jnp.sqrt(mx), kmax_ref.shape)

    qf = q.astype(F32)
    zb = jnp.sqrt(jnp.sum(qf * qf, axis=1, keepdims=True)) * kmax_ref[:1, :]

    if kind == "mla":
        def dense(fixed, unroll):
            m_ref[...] = jnp.full(m_ref.shape, NEG_INF, F32)
            acc_ref[...] = jnp.zeros_like(acc_ref)
            stage_a(0, 0)
            stage_b(0, 0, True, fixed)
            stage_a(1, 1)

            def body(t, c):
                for u in range(unroll):
                    step(unroll * t + 1 + u, (1 + u) % 2, fixed)
                return c

            trips = qb // unroll
            lax.fori_loop(0, trips, body, 0)
            for u in range(unroll - 1):
                @pl.when(qb - unroll * trips > u)
                def _():
                    step(unroll * trips + 1 + u, (1 + u) % 2, fixed)

            for slot in range(2):
                @pl.when(qb % 2 == slot)
                def _():
                    stage_c(qb, slot, fixed)

            o_ref[...] = acc_ref[:, :HEAD_DIM] / acc_ref[:, HEAD_DIM:]

        small = jnp.max(zb) < STABILISER_LIMIT

        @pl.when(small)
        def _():
            dense(True, 4)

        @pl.when(jnp.logical_not(small))
        def _():
            dense(False, 2)

        return

    def exhausted(j):
        if kind == "sb":
            return jnp.min(m_ref[...] - zb) > UNDERFLOW_BITS
        brow_max = jnp.max((fref - f_ref[0, key_block(j + 1)]) * LOG2E)
        return jnp.min(m_ref[...] - zb) - brow_max > UNDERFLOW_BITS

    m_ref[...] = jnp.full(m_ref.shape, 0.0 if kind == "sb" else NEG_INF, F32)
    acc_ref[...] = jnp.zeros_like(acc_ref)
    stage_a(0, 0)
    stage_b(0, 0, True)
    stage_a(1, 1)

    if kind == "sb":
        def cond1(c):
            return jnp.logical_and(c[0] <= qb, jnp.logical_not(c[1]))

        def single(c):
            j = c[0]
            step(j, j & 1)
            return j + 1, exhausted(j)

        nxt, _ = lax.while_loop(cond1, single, (jnp.int32(1), exhausted(0)))
        stage_c(nxt - 1, (nxt - 1) & 1)
        o_ref[...] = acc_ref[...]
        return

    def cond(c):
        return jnp.logical_and(c[0] < qb // 2, jnp.logical_not(c[1]))

    def pair(c):
        t = c[0]
        step(1 + 2 * t, 1)
        step(2 + 2 * t, 0)
        return t + 1, exhausted(2 + 2 * t)

    pairs, done = lax.while_loop(cond, pair, (jnp.int32(0), exhausted(0)))
    last = 2 * pairs
    tail = jnp.logical_and(jnp.logical_not(done), last != qb)

    @pl.when(tail)
    def _():
        step(qb, 1)
        stage_c(qb, 1)

    @pl.when(jnp.logical_not(tail))
    def _():
        stage_c(last, 0)

    o_ref[...] = acc_ref[:, :HEAD_DIM] / acc_ref[:, HEAD_DIM:]


def attn_prompt(kind, q, kt, v, f=None):
    t = q.shape[0]
    dq = q.shape[1] // N_HEADS
    tq = kt.shape[3]
    nq = t // tq
    in_specs = [pl.BlockSpec((tq, dq), lambda h, i: (i, h)),
                pl.BlockSpec((1, nq, dq, tq), lambda h, i: (h, 0, 0, 0)),
                pl.BlockSpec((t, HEAD_DIM), lambda h, i: (0, h))]
    args = [q, kt, v]
    if kind == "fox":
        in_specs.append(pl.BlockSpec((1, nq, 1, tq), lambda h, i: (h, 0, 0, 0)))
        args.append(f.reshape(N_HEADS, nq, 1, tq))
    acc_w = HEAD_DIM if kind == "sb" else 2 * HEAD_DIM
    scratch = [pltpu.VMEM((tq, LANES), F32), pltpu.VMEM((tq, acc_w), F32),
               pltpu.VMEM((2, tq, tq), F32), pltpu.VMEM((2, tq, tq), BF16)]
    if kind != "sb":
        scratch.append(pltpu.VMEM((2, tq, LANES), F32))
    scratch.append(pltpu.VMEM((8, LANES), F32))
    return pl.pallas_call(
        functools.partial(_attn_prompt_kernel, kind, tq),
        grid=(N_HEADS, nq),
        in_specs=in_specs,
        out_specs=pl.BlockSpec((tq, HEAD_DIM), lambda h, i: (i, h)),
        out_shape=jax.ShapeDtypeStruct((t, W_GROUP), F32),
        scratch_shapes=scratch,
        compiler_params=_params("parallel", "arbitrary"),
        name="attn_prompt_" + kind,
    )(*args)


def _softmax_tile(s, v, m_prev, l_prev, acc_prev):
    m_new = jnp.maximum(m_prev, jnp.max(s, axis=-1, keepdims=True))
    alpha = jnp.exp2(m_prev - m_new)
    pr = jnp.exp2(s - m_new)
    l_new = alpha * l_prev + jnp.sum(pr, axis=-1, keepdims=True)
    acc_new = alpha * acc_prev + _dot(pr.astype(BF16), v)
    return m_new, l_new, acc_new


def _attn_decode_kernel(kind, native, *refs):
    if kind == "fox":
        q_ref, kn_ref, vn_ref, kc_ref, vc_ref, fn_ref, fc_ref, o_ref = refs
    else:
        q_ref, kn_ref, vn_ref, kc_ref, vc_ref, o_ref = refs
    tq = q_ref.shape[0]
    dq = q_ref.shape[1] // N_HEADS
    past = kc_ref.shape[0] // N_HEADS if native else kc_ref.shape[0]
    qpos = past + lax.broadcasted_iota(jnp.int32, (tq, tq), 0)
    kpos = past + lax.broadcasted_iota(jnp.int32, (tq, tq), 1)
    mask = _tile_mask(kind, qpos, kpos)
    for h in range(N_HEADS):
        q = q_ref[:, h * dq:(h + 1) * dq]
        kn = kn_ref[:, h * dq:(h + 1) * dq]
        vn = vn_ref[:, h * HEAD_DIM:(h + 1) * HEAD_DIM]
        if native:
            kc = kc_ref[pl.ds(h, past, stride=N_HEADS), :].astype(BF16)
            vc = vc_ref[pl.ds(h, past, stride=N_HEADS), :].astype(BF16)
        else:
            kc = kc_ref[:, h * dq:(h + 1) * dq]
            vc = vc_ref[:, h * HEAD_DIM:(h + 1) * HEAD_DIM]
        s_n = _dot_nt(q, kn)
        s_c = _dot_nt(q, kc)
        if kind == "sb":
            a_n, carry = _sb_weights(s_n, jnp.zeros((tq, 1), F32), mask, _tri_lower(tq))
            a_c, _ = _sb_weights(s_c, carry, None, _tri_lower(min(TRI, past)))
            out = _dot(a_n.astype(BF16), vn) + _dot(a_c.astype(BF16), vc)
        else:
            if kind == "fox":
                fref = fn_ref[h][:, :1]
                s_n = s_n + (fref - fn_ref[h]) * LOG2E
                s_c = s_c + (fref - fc_ref[h]) * LOG2E
            s_n = jnp.where(mask, s_n, NEG_INF)
            m0 = jnp.full((tq, 1), NEG_INF, F32)
            z0 = jnp.zeros((tq, 1), F32)
            m, l, acc = _softmax_tile(s_n, vn, m0, z0, jnp.zeros((tq, HEAD_DIM), F32))
            m, l, acc = _softmax_tile(s_c, vc, m, l, acc)
            out = acc / l
        o_ref[:, h * HEAD_DIM:(h + 1) * HEAD_DIM] = out


def attn_decode(kind, q, kn, vn, kc, vc, layer=None, fn=None, fc=None):
    native = layer is not None
    nb, past = (kc.shape[1], kc.shape[2] // N_HEADS) if native else (kc.shape[0], kc.shape[1])
    tq = q.shape[0] // nb
    new = lambda a: pl.BlockSpec((tq, a.shape[1]), lambda b: (b, 0))
    if native:
        old = lambda a: pl.BlockSpec((None, None, past * N_HEADS, HEAD_DIM), lambda b: (layer, b, 0, 0))
    else:
        old = lambda a: pl.BlockSpec((None, past, a.shape[2]), lambda b: (b, 0, 0))
    in_specs = [new(q), new(kn), new(vn), old(kc), old(vc)]
    args = [q, kn, vn, kc, vc]
    if kind == "fox":
        in_specs += [pl.BlockSpec((None, N_HEADS, 1, tq), lambda b: (b, 0, 0, 0)),
                     pl.BlockSpec((None, N_HEADS, 1, past), lambda b: (b, 0, 0, 0))]
        args += [fn, fc]
    return pl.pallas_call(
        functools.partial(_attn_decode_kernel, kind, native),
        grid=(nb,),
        in_specs=in_specs,
        out_specs=pl.BlockSpec((tq, W_GROUP), lambda b: (b, 0)),
        out_shape=jax.ShapeDtypeStruct((nb * tq, W_GROUP), F32),
        compiler_params=_params("parallel"),
        name="attn_decode_" + kind,
    )(*args)


def _out_proj_kernel(oa_ref, ob_ref, oc_ref, ga_ref, gb_ref, gc_ref, w_ref, x_ref, o_ref):
    acc = x_ref[...]
    for g, (o, gn) in enumerate(((oa_ref, ga_ref), (ob_ref, gb_ref), (oc_ref, gc_ref))):
        y = _rms(o[...], gn[...]).astype(BF16)
        acc = acc + _dot(y, w_ref[g * W_GROUP:(g + 1) * W_GROUP, :])
    o_ref[...] = acc


def out_proj(oa, ob, oc, p, x, l):
    m = x.shape[0]
    tm = _pick_tile(m, 512)
    row = lambda w: pl.BlockSpec((tm, w), lambda i: (i, 0))
    full = lambda a: pl.BlockSpec(a.shape, lambda i: (0,) * a.ndim)
    consts = (p["out_norm_a"], p["out_norm_b"], p["out_norm_c"], p["w_out"])
    w_spec = pl.BlockSpec((None,) + p["w_out"].shape[1:], lambda i: (l, 0, 0))
    return pl.pallas_call(
        _out_proj_kernel,
        grid=(m // tm,),
        in_specs=[row(W_GROUP)] * 3 + [full(a) for a in consts[:3]] + [w_spec, row(D_MODEL)],
        out_specs=row(D_MODEL),
        out_shape=jax.ShapeDtypeStruct((m, D_MODEL), F32),
        compiler_params=_params("parallel"),
        name="out_proj",
    )(oa, ob, oc, *consts, x)


def _ffn_kernel(x_ref, g_ref, wg_ref, wu_ref, wd_ref, o_ref, h_ref):
    @pl.when(pl.program_id(1) == 0)
    def _():
        x = x_ref[...]
        h_ref[...] = _rms(x, g_ref[...]).astype(BF16)
        o_ref[...] = x

    h = h_ref[...]
    gate = _dot(h, wg_ref[...])
    up = _dot(h, wu_ref[...])
    act = (gate * jax.nn.sigmoid(gate) * up).astype(BF16)
    o_ref[...] += _dot(act, wd_ref[...])


def ffn(x, g, w_gu, w_down, l):
    m = x.shape[0]
    tm = _pick_tile(m, 1024)
    tf = 512
    nf = D_FF // tf
    return pl.pallas_call(
        _ffn_kernel,
        grid=(m // tm, nf),
        in_specs=[pl.BlockSpec((tm, D_MODEL), lambda i, j: (i, 0)),
                  pl.BlockSpec((1, D_MODEL), lambda i, j: (0, 0)),
                  pl.BlockSpec((None, D_MODEL, tf), lambda i, j: (l, 0, j)),
                  pl.BlockSpec((None, D_MODEL, tf), lambda i, j: (l, 0, j + nf)),
                  pl.BlockSpec((None, tf, D_MODEL), lambda i, j: (l, j, 0))],
        out_specs=pl.BlockSpec((tm, D_MODEL), lambda i, j: (i, 0)),
        out_shape=jax.ShapeDtypeStruct((m, D_MODEL), F32),
        scratch_shapes=[pltpu.VMEM((tm, D_MODEL), BF16)],
        compiler_params=_params("parallel", "arbitrary"),
        name="ffn",
    )(x, g, w_gu, w_gu, w_down)


def _dup_rope(r):
    half = MLA_ROPE_DIM // 2
    return jnp.concatenate([r, r[..., half:], r[..., :half]], axis=-1)


def _prep_layer(l, fox_f_bias, fox_q_norm, fox_k_norm, mla_qa_norm, mla_w_uq, mla_kva_norm, mla_w_ukv,
                mla_q_norm, mla_k_norm, out_norm_a, out_norm_b, out_norm_c, w_in, w_out, w_gu, w_down,
                norm_mix, norm_ffn):
    uq = mla_w_uq[l].reshape(MLA_Q_RANK, N_HEADS, MLA_QK_DIM)
    uq = jnp.concatenate([uq[..., :MLA_NOPE_DIM], _dup_rope(uq[..., MLA_NOPE_DIM:])], axis=-1)
    ukv = mla_w_ukv[l].reshape(MLA_KV_RANK, N_HEADS, 2 * LANES)
    ukv = jnp.concatenate([ukv[..., :LANES].reshape(MLA_KV_RANK, W_GROUP),
                           ukv[..., LANES:].reshape(MLA_KV_RANK, W_GROUP)], axis=1)

    def gain256(g):
        return jnp.concatenate([g[:MLA_NOPE_DIM], _dup_rope(g[MLA_NOPE_DIM:])])[None, :]

    row = lambda a: a[l][None, :]
    return {
        "norm_mix": row(norm_mix), "w_in": w_in,
        "fox_q_norm": row(fox_q_norm), "fox_k_norm": row(fox_k_norm),
        "fox_f_bias": jnp.pad(fox_f_bias[l], (0, LANES - N_HEADS))[None, :],
        "mla_qa_norm": row(mla_qa_norm),
        "mla_w_uq": uq.reshape(MLA_Q_RANK, N_HEADS * MLA_HEAD_PAD).astype(BF16),
        "mla_q_gain": gain256(mla_q_norm[l]) * (MLA_QK_DIM ** -0.5 * LOG2E),
        "mla_kva_norm": row(mla_kva_norm),
        "mla_w_ukv": ukv.astype(BF16),
        "mla_k_gain": gain256(mla_k_norm[l]),
        "out_norm_a": row(out_norm_a), "out_norm_b": row(out_norm_b), "out_norm_c": row(out_norm_c),
        "w_out": w_out,
        "norm_ffn": row(norm_ffn), "w_gu": w_gu, "w_down": w_down,
    }


def _rope_tables(pos):
    half = MLA_ROPE_DIM // 2
    inv_freq = ROPE_THETA ** (-(jnp.arange(half, dtype=F32) / half))
    ang = pos.astype(F32)[:, None] * inv_freq[None, :]
    cos, sin = jnp.cos(ang), jnp.sin(ang)
    zero = jnp.zeros_like(cos)
    return (jnp.concatenate([cos, cos, zero, zero], axis=1),
            jnp.concatenate([-sin, sin, zero, zero], axis=1))


def _layer(x, caches, l, depth, p, tabs, stacks):
    nb, t, _ = x.shape
    m = nb * t
    x2 = x.reshape(m, D_MODEL)
    prompt = caches is None
    outs = inproj_post(x2, p, *tabs["q"], with_kt=prompt, l=l, depth=depth, stacks=stacks)
    stacks = outs[:5]
    qa, vab, lf, qb, vbb, qc, krd, k2a, k2b = outs[5:]
    if prompt:
        kct, vc_new = kexp(stacks[4], krd, p, *tabs["q"], transposed=True, layer=l)
        f = cumsum_rows(lf.reshape(nb, t, LANES))
        oa = attn_prompt("fox", qa, k2a, vab, f[0, :N_HEADS])
        ob = attn_prompt("sb", qb, k2b, vbb)
        oc = attn_prompt("mla", qc, kct, vc_new)
    else:
        c_fk, c_fv, c_lf, c_sk, c_sv, c_ckv, c_kr = caches
        pl_ = c_fk.shape[2]
        kc_new, vc_new = kexp(stacks[4], krd, p, *tabs["q"], transposed=False, layer=l)
        lf_all = jnp.concatenate(
            [jnp.pad(c_lf[l], ((0, 0), (0, 0), (0, LANES - N_HEADS))), lf.reshape(nb, t, LANES)], axis=1)
        n_pad = -(-(pl_ + t) // LANES) * LANES
        lf_all = jnp.pad(lf_all, ((0, 0), (0, n_pad - pl_ - t), (0, 0)))
        f = cumsum_rows(lf_all)[:, :N_HEADS, None, :]
        kc_old, vc_old = kexp(c_ckv.reshape(c_ckv.shape[0], nb * pl_, MLA_KV_RANK),
                              _dup_rope(c_kr[l]).reshape(nb * pl_, LANES), p, *tabs["kc"],
                              transposed=False, layer=l)
        rows4 = lambda a: a.reshape(a.shape[0], nb, pl_ * N_HEADS, HEAD_DIM)
        oa = attn_decode("fox", qa, k2a, vab, rows4(c_fk), rows4(c_fv), layer=l,
                         fn=f[..., pl_:pl_ + t], fc=f[..., :pl_])
        ob = attn_decode("sb", qb, k2b, vbb, rows4(c_sk), rows4(c_sv), layer=l)
        oc = attn_decode("mla", qc, kc_new, vc_new, kc_old.reshape(nb, pl_, -1), vc_old.reshape(nb, pl_, -1))
    x2 = out_proj(oa, ob, oc, p, x2, l)
    x2 = ffn(x2, p["norm_ffn"], p["w_gu"], p["w_down"], l)
    small = (lf[:, :N_HEADS].reshape(nb, t, N_HEADS), krd[:, :MLA_ROPE_DIM].reshape(nb, t, MLA_ROPE_DIM))
    return x2.reshape(nb, t, D_MODEL), stacks, small


def kernel(x_prompt, x_sample, cache_fox_k, cache_fox_v, cache_fox_logf, cache_sb_k, cache_sb_v, cache_mla_ckv, cache_mla_krope, norm_mix, w_in, fox_f_bias, fox_q_norm, fox_k_norm, mla_qa_norm, mla_w_uq, mla_kva_norm, mla_w_ukv, mla_q_norm, mla_k_norm, out_norm_a, out_norm_b, out_norm_c, w_out, norm_ffn, w_gu, w_down):
    depth = w_in.shape[0]
    t_p = x_prompt.shape[1]
    nb_s, t_s = x_sample.shape[0], x_sample.shape[1]
    past_len = cache_fox_k.shape[2]
    tabs_p = {"q": _rope_tables(jnp.arange(t_p, dtype=jnp.int32))}
    tabs_s = {"q": _rope_tables(jnp.tile(past_len + jnp.arange(t_s, dtype=jnp.int32), nb_s)),
              "kc": _rope_tables(jnp.tile(jnp.arange(past_len, dtype=jnp.int32), nb_s))}
    caches = (cache_fox_k, cache_fox_v, cache_fox_logf, cache_sb_k, cache_sb_v, cache_mla_ckv, cache_mla_krope)
    y_p, y_s = x_prompt, x_sample
    rows_p, rows_s = [], []
    def new_stacks(m):
        heads = tuple(jnp.zeros((depth, m, N_HEADS, HEAD_DIM), F32) for _ in range(4))
        return heads + (jnp.zeros((depth, m, MLA_KV_RANK), F32),)

    stacks_p = new_stacks(x_prompt.shape[0] * t_p)
    stacks_s = new_stacks(nb_s * t_s)
    w_in_b = w_in_prep(w_in)
    w_out_b, w_gu_b, w_down_b = w_out.astype(BF16), w_gu.astype(BF16), w_down.astype(BF16)
    for l in range(depth):
        p = _prep_layer(l, fox_f_bias, fox_q_norm, fox_k_norm, mla_qa_norm, mla_w_uq, mla_kva_norm, mla_w_ukv,
                        mla_q_norm, mla_k_norm, out_norm_a, out_norm_b, out_norm_c, w_in_b, w_out_b, w_gu_b,
                        w_down_b, norm_mix, norm_ffn)
        y_p, stacks_p, r_p = _layer(y_p, None, l, depth, p, tabs_p, stacks_p)
        y_s, stacks_s, r_s = _layer(y_s, caches, l, depth, p, tabs_s, stacks_s)
        rows_p.append(r_p)
        rows_s.append(r_s)

    def assemble(stacks, rows, nb, t):
        fk, fv, sk, sv, ckv = stacks
        heads = lambda a: a.reshape(depth, nb, t, N_HEADS, HEAD_DIM)
        return (heads(fk), heads(fv), jnp.stack([r[0] for r in rows], axis=0), heads(sk), heads(sv),
                ckv.reshape(depth, nb, t, MLA_KV_RANK), jnp.stack([r[1] for r in rows], axis=0))

    return ((y_p, y_s) + assemble(stacks_p, rows_p, x_prompt.shape[0], t_p)
            + assemble(stacks_s, rows_s, nb_s, t_s))
```

```python
import functools
import math

import jax
import jax.numpy as jnp
from jax import lax
from jax.experimental import pallas as pl
from jax.experimental.pallas import tpu as pltpu

D_MODEL = 2048
CHUNK = 64
HEAD_DIM = 128
N_HEADS = 4
W_GROUP = N_HEADS * HEAD_DIM
MLA_Q_RANK = 512
MLA_KV_RANK = 256
MLA_NOPE_DIM = 128
MLA_ROPE_DIM = 64
MLA_QK_DIM = MLA_NOPE_DIM + MLA_ROPE_DIM
MLA_HEAD_PAD = 256
ROPE_THETA = 10000.0
D_FF = 5632
RMS_EPS = 1e-6
NEG_INF = -1e30
LOG2E = math.log2(math.e)
N_IN_PAD = 4096
LANES = 128
TRI = 256
ATTN_TILE = 512
SB_TILE = 512
INPROJ_ROWS = 256
PROJ_CHUNK = 1024
ROW_GROUP = 128
STABILISER_LIMIT = 48.0
UNDERFLOW_BITS = 160.0
VMEM_LIMIT = 56 * 1024 * 1024

C_QA, C_KA, C_VA, C_QB, C_KB, C_VB, C_CQ, C_CKV, C_KR, C_FA = (
    0, 512, 1024, 1536, 2048, 2560, 3072, 3584, 3840, 3968)

BF16 = jnp.bfloat16
F32 = jnp.float32


def _params(*sem):
    return pltpu.CompilerParams(dimension_semantics=sem, vmem_limit_bytes=VMEM_LIMIT)


def _pick_tile(n, pref):
    t = min(n, pref)
    while n % t:
        t //= 2
    return t


def _rms(x, g):
    return x * lax.rsqrt(jnp.mean(x * x, axis=-1, keepdims=True) + RMS_EPS) * g


def _dot(a, b):
    return jnp.dot(a, b, preferred_element_type=F32)


def _dot_nt(a, b):
    return lax.dot_general(a, b, (((1,), (1,)), ((), ())), preferred_element_type=F32)


def _split2(x):
    hi = x.astype(BF16)
    lo = (x - hi.astype(F32)).astype(BF16)
    return hi, lo


def _neg_abs(x):
    bits = lax.bitcast_convert_type(x, jnp.uint32) | jnp.uint32(0x80000000)
    return lax.bitcast_convert_type(bits, F32)


def _split3(x):
    a = x.astype(BF16)
    r = x - a.astype(F32)
    b = r.astype(BF16)
    c = (r - b.astype(F32)).astype(BF16)
    return a, b, c


def _log_sigmoid(x):
    return -(jnp.maximum(-x, 0.0) + jnp.log1p(jnp.exp(-jnp.abs(x))))


def _log2_keep(z2):
    return -(jnp.maximum(z2, 0.0) + jnp.log2(1.0 + jnp.exp2(_neg_abs(z2))))


def _w_in_prep_kernel(w_ref, o_ref):
    rows = w_ref.shape[0]
    fa0 = C_QB
    o_ref[:, :C_QB] = w_ref[:, :fa0].astype(BF16)
    o_ref[:, C_QB:C_KR] = w_ref[:, fa0 + N_HEADS:C_KR + N_HEADS].astype(BF16)
    kr = w_ref[:, C_KR + N_HEADS:C_KR + N_HEADS + MLA_ROPE_DIM]
    half = MLA_ROPE_DIM // 2
    o_ref[:, C_KR:C_FA] = jnp.concatenate([kr, kr[:, half:], kr[:, :half]], axis=1).astype(BF16)
    fa = w_ref[:, fa0:fa0 + N_HEADS]
    o_ref[:, C_FA:] = jnp.concatenate([fa, jnp.zeros((rows, LANES - N_HEADS), F32)], axis=1).astype(BF16)


def w_in_prep(w_in):
    depth, d, n = w_in.shape
    tr = 256
    return pl.pallas_call(
        _w_in_prep_kernel,
        grid=(depth, d // tr),
        in_specs=[pl.BlockSpec((None, tr, n), lambda l, i: (l, i, 0))],
        out_specs=pl.BlockSpec((None, tr, N_IN_PAD), lambda l, i: (l, i, 0)),
        out_shape=jax.ShapeDtypeStruct((depth, d, N_IN_PAD), BF16),
        compiler_params=_params("parallel", "parallel"),
        name="w_in_prep",
    )(w_in)


def _rope_chunk(y2, cos, sin):
    return y2 * cos + pltpu.roll(y2, 64, 1) * sin


def _rope_lane_mask():
    return lax.broadcasted_iota(jnp.int32, (1, LANES), 1) < MLA_ROPE_DIM


def _inproj_post_kernel(with_kt, n_alias, x_ref, g_ref, w_ref, fqg_ref, fkg_ref, fb_ref, qag_ref, wuq_ref, qg_ref,
                        kvg_ref, cos_ref, sin_ref, *refs):
    refs = refs[n_alias:]
    (ka_o, va_o, kb_o, vb_o, ckv_o, qa_o, vab_o, lf_o, qb_o, vbb_o, qc_o, kr_o, k2a_o, k2b_o,
     proj_ref) = refs
    hn = _rms(x_ref[...], g_ref[...]).astype(BF16)
    for c in range(N_IN_PAD // PROJ_CHUNK):
        cols = slice(c * PROJ_CHUNK, (c + 1) * PROJ_CHUNK)
        proj_ref[:, cols] = _dot(hn, w_ref[:, cols])
    scale = HEAD_DIM ** -0.5 * LOG2E
    va = proj_ref[:, C_VA:C_VA + W_GROUP]
    vb = proj_ref[:, C_VB:C_VB + W_GROUP]
    for h in range(N_HEADS):
        sl = slice(h * HEAD_DIM, (h + 1) * HEAD_DIM)
        qa = proj_ref[:, C_QA + h * HEAD_DIM:C_QA + (h + 1) * HEAD_DIM]
        qa_o[:, sl] = (_rms(qa, fqg_ref[...]) * scale).astype(BF16)
        ka = _rms(proj_ref[:, C_KA + h * HEAD_DIM:C_KA + (h + 1) * HEAD_DIM], fkg_ref[...])
        kb = proj_ref[:, C_KB + h * HEAD_DIM:C_KB + (h + 1) * HEAD_DIM]
        ka_o[:, h, :] = ka
        kb_o[:, h, :] = kb
        va_o[:, h, :] = va[:, sl]
        vb_o[:, h, :] = vb[:, sl]
        if with_kt:
            k2a_o[h, 0] = ka.T.astype(BF16)
            k2b_o[h, 0] = kb.T.astype(BF16)
        else:
            k2a_o[:, sl] = ka.astype(BF16)
            k2b_o[:, sl] = kb.astype(BF16)
    vab_o[...] = va.astype(BF16)
    qb_o[...] = (proj_ref[:, C_QB:C_QB + W_GROUP] * scale).astype(BF16)
    vbb_o[...] = vb.astype(BF16)

    lane = lax.broadcasted_iota(jnp.int32, (1, LANES), 1)
    lf = _log_sigmoid(proj_ref[:, C_FA:C_FA + LANES] + fb_ref[...])
    lf_o[...] = jnp.where(lane < N_HEADS, lf, 0.0)

    cqn = _rms(proj_ref[:, C_CQ:C_CQ + MLA_Q_RANK], qag_ref[...]).astype(BF16)
    qc = _dot(cqn, wuq_ref[...])
    rmask = _rope_lane_mask()
    cos = cos_ref[...]
    sin = sin_ref[...]
    for h in range(N_HEADS):
        c1 = qc[:, h * MLA_HEAD_PAD:h * MLA_HEAD_PAD + LANES]
        c2 = qc[:, h * MLA_HEAD_PAD + LANES:(h + 1) * MLA_HEAD_PAD]
        ss = (jnp.sum(c1 * c1, axis=-1, keepdims=True)
              + jnp.sum(jnp.where(rmask, c2 * c2, 0.0), axis=-1, keepdims=True))
        rs = lax.rsqrt(ss * (1.0 / MLA_QK_DIM) + RMS_EPS)
        y1 = c1 * rs * qg_ref[:, :LANES]
        y2 = c2 * rs * qg_ref[:, LANES:]
        qc_o[:, h * MLA_HEAD_PAD:h * MLA_HEAD_PAD + LANES] = y1.astype(BF16)
        qc_o[:, h * MLA_HEAD_PAD + LANES:(h + 1) * MLA_HEAD_PAD] = _rope_chunk(y2, cos, sin).astype(BF16)

    ckv_o[...] = _rms(proj_ref[:, C_CKV:C_CKV + MLA_KV_RANK], kvg_ref[...])
    kr_o[...] = proj_ref[:, C_KR:C_KR + LANES]


def inproj_post(x, p, cos, sin, with_kt, l, depth, stacks):
    m = x.shape[0]
    tm = _pick_tile(m, INPROJ_ROWS)
    row =lambda w: pl.BlockSpec((tm, w), lambda i: (i, 0))
    full = lambda a: pl.BlockSpec(a.shape, lambda i: (0,) * a.ndim)
    consts = (p["fox_q_norm"], p["fox_k_norm"], p["fox_f_bias"], p["mla_qa_norm"], p["mla_w_uq"],
              p["mla_q_gain"], p["mla_kva_norm"])
    w_spec = pl.BlockSpec((None, D_MODEL, N_IN_PAD), lambda i: (l, 0, 0), pipeline_mode=pl.Buffered(1))
    heads_spec = pl.BlockSpec((None, tm, N_HEADS, HEAD_DIM), lambda i: (l, i, 0, 0))
    heads_shape = jax.ShapeDtypeStruct((depth, m, N_HEADS, HEAD_DIM), F32)
    out_specs = [heads_spec] * 4 + [pl.BlockSpec((None, tm, MLA_KV_RANK), lambda i: (l, i, 0))]
    out_shape = [heads_shape] * 4 + [jax.ShapeDtypeStruct((depth, m, MLA_KV_RANK), F32)]
    outs = [(W_GROUP, BF16), (W_GROUP, BF16), (LANES, F32), (W_GROUP, BF16), (W_GROUP, BF16),
            (N_HEADS * MLA_HEAD_PAD, BF16), (LANES, F32)]
    out_specs += [row(w) for w, _ in outs]
    out_shape += [jax.ShapeDtypeStruct((m, w), dt) for w, dt in outs]
    for kt_pref in (ATTN_TILE, SB_TILE):
        if with_kt:
            kt_tile = _pick_tile(m, kt_pref)
            sub = kt_tile // tm
            out_specs.append(pl.BlockSpec((N_HEADS, 1, HEAD_DIM, tm),
                                          lambda i, sub=sub: (0, i // sub, 0, i % sub)))
            out_shape.append(jax.ShapeDtypeStruct((N_HEADS, m // kt_tile, HEAD_DIM, kt_tile), BF16))
        else:
            out_specs.append(row(W_GROUP))
            out_shape.append(jax.ShapeDtypeStruct((m, W_GROUP), BF16))
    n_in = 3 + len(consts) + 2
    stacks = () if stacks is None else tuple(stacks)
    return pl.pallas_call(
        functools.partial(_inproj_post_kernel, with_kt, len(stacks)),
        grid=(m // tm,),
        in_specs=([row(D_MODEL), full(p["norm_mix"]), w_spec] + [full(a) for a in consts]
                  + [row(LANES), row(LANES)] + [pl.BlockSpec(memory_space=pl.ANY)] * len(stacks)),
        out_specs=out_specs,
        out_shape=out_shape,
        scratch_shapes=[pltpu.VMEM((tm, N_IN_PAD), F32)],
        input_output_aliases={n_in + k: k for k in range(len(stacks))},
        compiler_params=_params("parallel"),
        name="inproj_post",
    )(x, p["norm_mix"], p["w_in"], *consts, cos, sin, *stacks)


def _kexp_kernel(transposed, ckv_ref, kr_ref, w_ref, kg_ref, cos_ref, sin_ref, kc_o, vc_o):
    kv = _dot(ckv_ref[...].astype(BF16), w_ref[...])
    kr = kr_ref[...]
    krsq = jnp.sum(jnp.where(_rope_lane_mask(), kr * kr, 0.0), axis=-1, keepdims=True)
    cos = cos_ref[...]
    sin = sin_ref[...]
    for h in range(N_HEADS):
        n = kv[:, h * LANES:(h + 1) * LANES]
        ss = jnp.sum(n * n, axis=-1, keepdims=True) + krsq
        rs = lax.rsqrt(ss * (1.0 / MLA_QK_DIM) + RMS_EPS)
        k1 = n * rs * kg_ref[:, :LANES]
        k2 = _rope_chunk(kr * rs * kg_ref[:, LANES:], cos, sin)
        if transposed:
            kc_o[h, 0, :LANES, :] = k1.T.astype(BF16)
            kc_o[h, 0, LANES:, :] = k2.T.astype(BF16)
        else:
            kc_o[:, h * MLA_HEAD_PAD:h * MLA_HEAD_PAD + LANES] = k1.astype(BF16)
            kc_o[:, h * MLA_HEAD_PAD + LANES:(h + 1) * MLA_HEAD_PAD] = k2.astype(BF16)
    vc_o[...] = kv[:, W_GROUP:].astype(BF16)


def kexp(ckv, krdup, p, cos, sin, transposed, layer=None):
    m = krdup.shape[0]
    tm = _pick_tile(m, ATTN_TILE)
    row = lambda w: pl.BlockSpec((tm, w), lambda i: (i, 0))
    full = lambda a: pl.BlockSpec(a.shape, lambda i: (0,) * a.ndim)
    ckv_spec = row(MLA_KV_RANK) if layer is None else pl.BlockSpec(
        (None, tm, MLA_KV_RANK), lambda i: (layer, i, 0))
    if transposed:
        kc_spec = pl.BlockSpec((N_HEADS, 1, MLA_HEAD_PAD, tm), lambda i: (0, i, 0, 0))
        kc_shape = jax.ShapeDtypeStruct((N_HEADS, m // tm, MLA_HEAD_PAD, tm), BF16)
    else:
        kc_spec = row(N_HEADS * MLA_HEAD_PAD)
        kc_shape = jax.ShapeDtypeStruct((m, N_HEADS * MLA_HEAD_PAD), BF16)
    return pl.pallas_call(
        functools.partial(_kexp_kernel, transposed),
        grid=(m // tm,),
        in_specs=[ckv_spec, row(LANES), full(p["mla_w_ukv"]), full(p["mla_k_gain"]), row(LANES), row(LANES)],
        out_specs=[kc_spec, row(W_GROUP)],
        out_shape=[kc_shape, jax.ShapeDtypeStruct((m, W_GROUP), BF16)],
        compiler_params=_params("parallel"),
        name="kexp",
    )(ckv, krdup, p["mla_w_ukv"], p["mla_k_gain"], cos, sin)


def _cumsum_kernel(x_ref, o_ref, carry_ref):
    @pl.when(pl.program_id(1) == 0)
    def _():
        carry_ref[...] = jnp.zeros_like(carry_ref)

    t = x_ref.shape[1]
    xt = x_ref[0].T[:8, :]
    ri = lax.broadcasted_iota(jnp.int32, (t, t), 0)
    ci = lax.broadcasted_iota(jnp.int32, (t, t), 1)
    upper = jnp.where(ri <= ci, 1.0, 0.0).astype(BF16)
    a, b, c = _split3(xt)
    f = _dot(a, upper) + _dot(b, upper) + _dot(c, upper) + carry_ref[:, :1]
    o_ref[0] = f
    carry_ref[...] = jnp.broadcast_to(f[:, t - 1:t], carry_ref.shape)


def cumsum_rows(x):
    b, n, _ = x.shape
    t = n if n <= 1280 else _pick_tile(n, 512)
    return pl.pallas_call(
        _cumsum_kernel,
        grid=(b, n // t),
        in_specs=[pl.BlockSpec((1, t, LANES), lambda i, j: (i, j, 0))],
        out_specs=pl.BlockSpec((1, 8, t), lambda i, j: (i, 0, j)),
        out_shape=jax.ShapeDtypeStruct((b, 8, n), F32),
        scratch_shapes=[pltpu.VMEM((8, LANES), F32)],
        compiler_params=_params("parallel", "arbitrary"),
        name="cumsum_rows",
    )(x)


def _tile_mask(kind, qpos, kpos):
    if kind == "fox":
        return kpos <= qpos
    if kind == "sb":
        return kpos < qpos
    shift = CHUNK.bit_length() - 1
    return lax.shift_right_logical(kpos, shift) <= lax.shift_right_logical(qpos, shift)


def _tri_lower(n):
    ri = lax.broadcasted_iota(jnp.int32, (n, n), 0)
    ci = lax.broadcasted_iota(jnp.int32, (n, n), 1)
    return jnp.where(ri >= ci, 1.0, 0.0).astype(BF16)


def _sb_weights(z, carry, mask, tri):
    lk = _log2_keep(z)
    if mask is not None:
        lk = jnp.where(mask, lk, 0.0)
    blk = tri.shape[0]
    nblk = z.shape[1] // blk
    parts = [None] * nblk
    for c in reversed(range(nblk)):
        lkc = lk[:, c * blk:(c + 1) * blk]
        hi, lo = _split2(lkc)
        intra = _dot(hi, tri) + _dot(lo, tri)
        if carry.shape[1] == 1 or carry.shape[1] == blk:
            parts[c] = intra + carry
        else:
            parts[c] = intra + jnp.concatenate([carry] * (blk // carry.shape[1]), axis=1)
        carry = carry + jnp.sum(lkc, axis=-1, keepdims=True)
    r = parts[0] if nblk == 1 else jnp.concatenate(parts, axis=-1)
    a = jnp.exp2(z + r)
    if mask is not None:
        a = jnp.where(mask, a, 0.0)
    return a, carry


def _attn_prompt_kernel(kind, tq, *refs):
    f_ref = al_ref = kmax_ref = None
    if kind == "fox":
        q_ref, kt_ref, v_ref, f_ref, o_ref, m_ref, acc_ref, s_ref, p_ref, al_ref, kmax_ref = refs
    elif kind == "mla":
        q_ref, kt_ref, v_ref, o_ref, m_ref, acc_ref, s_ref, p_ref, al_ref, kmax_ref = refs
    else:
        q_ref, kt_ref, v_ref, o_ref, m_ref, acc_ref, s_ref, p_ref, kmax_ref = refs
    qb = pl.program_id(1)
    rg = min(TRI if kind == "sb" else ROW_GROUP, tq)
    n_rg = tq // rg
    q = q_ref[...]
    ones = jnp.ones((tq, LANES), BF16)
    tri2 = jnp.concatenate([_tri_lower(rg)] * 2, axis=0) if kind == "sb" else None
    fref = f_ref[0, qb][:, :1] if kind == "fox" else None

    def key_block(j):
        return jnp.clip(qb - j, 0, qb)

    def stage_a(j, slot):
        s_ref[slot] = _dot(q, kt_ref[0, key_block(j)])

    def stage_b(j, slot, diagonal, fixed=False):
        brow = (fref - f_ref[0, key_block(j)]) * LOG2E if kind == "fox" else None
        ms, als, prs = [], [], []
        for r in range(n_rg):
            rows = slice(r * rg, (r + 1) * rg)
            kw = (r + 1) * rg if diagonal else tq
            s = s_ref[slot, rows, :kw]
            mask = None
            if diagonal:
                qpos = r * rg + lax.broadcasted_iota(jnp.int32, (rg, kw), 0)
                kpos = lax.broadcasted_iota(jnp.int32, (rg, kw), 1)
                mask = _tile_mask(kind, qpos, kpos)
            m_prev = m_ref[rows, :]
            if kind == "sb":
                plk = jnp.maximum(s, 0.0) + jnp.log2(1.0 + jnp.exp2(_neg_abs(s)))
                if mask is not None:
                    plk = jnp.where(mask, plk, 0.0)
                carry = m_prev
                parts = [None] * (kw // rg)
                for c in reversed(range(kw // rg)):
                    pc = plk[:, c * rg:(c + 1) * rg]
                    hi, lo = _split2(pc)
                    later = _dot(jnp.concatenate([hi, lo], axis=1), tri2)
                    parts[c] = s[:, c * rg:(c + 1) * rg] - later - jnp.concatenate([carry] * (rg // LANES), 1)
                    carry = carry + jnp.sum(pc, axis=-1, keepdims=True)
                a = jnp.exp2(parts[0] if len(parts) == 1 else jnp.concatenate(parts, axis=1))
                if mask is not None:
                    a = jnp.where(mask, a, 0.0)
                pr = a.astype(BF16)
                ms.append(carry)
            else:
                if kind == "fox":
                    s = s + brow[:, :kw]
                if mask is not None:
                    s = jnp.where(mask, s, NEG_INF)
                chunks = [s[:, c * LANES:(c + 1) * LANES] for c in range(kw // LANES)]
                if fixed:
                    m_new = zb[rows, :]
                else:
                    m_cur = jnp.max(functools.reduce(jnp.maximum, chunks), axis=-1, keepdims=True)
                    m_new = jnp.maximum(m_prev, m_cur)
                    als.append(jnp.exp2(m_prev - m_new))
                    ms.append(m_new)
                pr = jnp.concatenate([jnp.exp2(c - m_new).astype(BF16) for c in chunks], axis=1)
            if kw < tq:
                pr = jnp.concatenate([pr, jnp.zeros((rg, tq - kw), BF16)], axis=1)
            prs.append(pr)
        p_ref[slot] = jnp.concatenate(prs, axis=0)
        if not fixed:
            m_ref[...] = jnp.concatenate(ms, axis=0)
            if kind != "sb":
                al_ref[slot] = jnp.concatenate(als, axis=0)

    def stage_c(j, slot, fixed=False):
        start = pl.multiple_of(key_block(j) * tq, tq)
        v = v_ref[pl.ds(start, tq), :]
        if kind == "sb":
            acc_ref[...] += _dot(p_ref[slot], v)
        else:
            pv = _dot(p_ref[slot], jnp.concatenate([v, ones], axis=1))
            if fixed:
                acc_ref[...] += pv
            else:
                al = al_ref[slot]
                acc_ref[...] = jnp.concatenate([al, al], axis=1) * acc_ref[...] + pv

    def step(j, slot, fixed=False):
        stage_b(j, slot, False, fixed)
        stage_a(j + 1, 1 - slot)
        stage_c(j - 1, 1 - slot, fixed)

    @pl.when(qb == 0)
    def _():
        def norm_body(i, mx):
            kt = kt_ref[0, i].astype(F32)
            return jnp.maximum(mx, jnp.max(jnp.sum(kt * kt, axis=0, keepdims=True), axis=1, keepdims=True))

        mx = lax.fori_loop(0, kt_ref.shape[1], norm_body, jnp.zeros((1, 1), F32))
        kmax_ref[...] = jnp.broadcast_to(jnp.sqrt(mx), kmax_ref.shape)

    qf = q.astype(F32)
    zb = jnp.sqrt(jnp.sum(qf * qf, axis=1, keepdims=True)) * kmax_ref[:1, :]

    if kind == "mla":
        def dense(fixed, unroll):
            m_ref[...] = jnp.full(m_ref.shape, NEG_INF, F32)
            acc_ref[...] = jnp.zeros_like(acc_ref)
            stage_a(0, 0)
            stage_b(0, 0, True, fixed)
            stage_a(1, 1)

            def body(t, c):
                for u in range(unroll):
                    step(unroll * t + 1 + u, (1 + u) % 2, fixed)
                return c

            trips = qb // unroll
            lax.fori_loop(0, trips, body, 0)
            for u in range(unroll - 1):
                @pl.when(qb - unroll * trips > u)
                def _():
                    step(unroll * trips + 1 + u, (1 + u) % 2, fixed)

            for slot in range(2):
                @pl.when(qb % 2 == slot)
                def _():
                    stage_c(qb, slot, fixed)

            o_ref[...] = acc_ref[:, :HEAD_DIM] / acc_ref[:, HEAD_DIM:]

        small = jnp.max(zb) < STABILISER_LIMIT

        @pl.when(small)
        def _():
            dense(True, 4)

        @pl.when(jnp.logical_not(small))
        def _():
            dense(False, 2)

        return

    def exhausted(j):
        if kind == "sb":
            return jnp.min(m_ref[...] - zb) > UNDERFLOW_BITS
        brow_max = jnp.max((fref - f_ref[0, key_block(j + 1)]) * LOG2E)
        return jnp.min(m_ref[...] - zb) - brow_max > UNDERFLOW_BITS

    m_ref[...] = jnp.full(m_ref.shape, 0.0 if kind == "sb" else NEG_INF, F32)
    acc_ref[...] = jnp.zeros_like(acc_ref)
    stage_a(0, 0)
    stage_b(0, 0, True)
    stage_a(1, 1)

    if kind == "sb":
        def cond1(c):
            return jnp.logical_and(c[0] <= qb, jnp.logical_not(c[1]))

        def single(c):
            j = c[0]
            step(j, j & 1)
            return j + 1, exhausted(j)

        nxt, _ = lax.while_loop(cond1, single, (jnp.int32(1), exhausted(0)))
        stage_c(nxt - 1, (nxt - 1) & 1)
        o_ref[...] = acc_ref[...]
        return

    def cond(c):
        return jnp.logical_and(c[0] < qb // 2, jnp.logical_not(c[1]))

    def pair(c):
        t = c[0]
        step(1 + 2 * t, 1)
        step(2 + 2 * t, 0)
        return t + 1, exhausted(2 + 2 * t)

    pairs, done = lax.while_loop(cond, pair, (jnp.int32(0), exhausted(0)))
    last = 2 * pairs
    tail = jnp.logical_and(jnp.logical_not(done), last != qb)

    @pl.when(tail)
    def _():
        step(qb, 1)
        stage_c(qb, 1)

    @pl.when(jnp.logical_not(tail))
    def _():
        stage_c(last, 0)

    o_ref[...] = acc_ref[:, :HEAD_DIM] / acc_ref[:, HEAD_DIM:]


def attn_prompt(kind, q, kt, v, f=None):
    t = q.shape[0]
    dq = q.shape[1] // N_HEADS
    tq = kt.shape[3]
    nq = t // tq
    in_specs = [pl.BlockSpec((tq, dq), lambda h, i: (i, h)),
                pl.BlockSpec((1, nq, dq, tq), lambda h, i: (h, 0, 0, 0)),
                pl.BlockSpec((t, HEAD_DIM), lambda h, i: (0, h))]
    args = [q, kt, v]
    if kind == "fox":
        in_specs.append(pl.BlockSpec((1, nq, 1, tq), lambda h, i: (h, 0, 0, 0)))
        args.append(f.reshape(N_HEADS, nq, 1, tq))
    acc_w = HEAD_DIM if kind == "sb" else 2 * HEAD_DIM
    scratch = [pltpu.VMEM((tq, LANES), F32), pltpu.VMEM((tq, acc_w), F32),
               pltpu.VMEM((2, tq, tq), F32), pltpu.VMEM((2, tq, tq), BF16)]
    if kind != "sb":
        scratch.append(pltpu.VMEM((2, tq, LANES), F32))
    scratch.append(pltpu.VMEM((8, LANES), F32))
    return pl.pallas_call(
        functools.partial(_attn_prompt_kernel, kind, tq),
        grid=(N_HEADS, nq),
        in_specs=in_specs,
        out_specs=pl.BlockSpec((tq, HEAD_DIM), lambda h, i: (i, h)),
        out_shape=jax.ShapeDtypeStruct((t, W_GROUP), F32),
        scratch_shapes=scratch,
        compiler_params=_params("parallel", "arbitrary"),
        name="attn_prompt_" + kind,
    )(*args)


def _softmax_tile(s, v, m_prev, l_prev, acc_prev):
    m_new = jnp.maximum(m_prev, jnp.max(s, axis=-1, keepdims=True))
    alpha = jnp.exp2(m_prev - m_new)
    pr = jnp.exp2(s - m_new)
    l_new = alpha * l_prev + jnp.sum(pr, axis=-1, keepdims=True)
    acc_new = alpha * acc_prev + _dot(pr.astype(BF16), v)
    return m_new, l_new, acc_new


def _attn_decode_kernel(kind, native, *refs):
    if kind == "fox":
        q_ref, kn_ref, vn_ref, kc_ref, vc_ref, fn_ref, fc_ref, o_ref = refs
    else:
        q_ref, kn_ref, vn_ref, kc_ref, vc_ref, o_ref = refs
    tq = q_ref.shape[0]
    dq = q_ref.shape[1] // N_HEADS
    past = kc_ref.shape[0] // N_HEADS if native else kc_ref.shape[0]
    qpos = past + lax.broadcasted_iota(jnp.int32, (tq, tq), 0)
    kpos = past + lax.broadcasted_iota(jnp.int32, (tq, tq), 1)
    mask = _tile_mask(kind, qpos, kpos)
    for h in range(N_HEADS):
        q = q_ref[:, h * dq:(h + 1) * dq]
        kn = kn_ref[:, h * dq:(h + 1) * dq]
        vn = vn_ref[:, h * HEAD_DIM:(h + 1) * HEAD_DIM]
        if native:
            kc = kc_ref[pl.ds(h, past, stride=N_HEADS), :].astype(BF16)
            vc = vc_ref[pl.ds(h, past, stride=N_HEADS), :].astype(BF16)
        else:
            kc = kc_ref[:, h * dq:(h + 1) * dq]
            vc = vc_ref[:, h * HEAD_DIM:(h + 1) * HEAD_DIM]
        s_n = _dot_nt(q, kn)
        s_c = _dot_nt(q, kc)
        if kind == "sb":
            a_n, carry = _sb_weights(s_n, jnp.zeros((tq, 1), F32), mask, _tri_lower(tq))
            a_c, _ = _sb_weights(s_c, carry, None, _tri_lower(min(TRI, past)))
            out = _dot(a_n.astype(BF16), vn) + _dot(a_c.astype(BF16), vc)
        else:
            if kind == "fox":
                fref = fn_ref[h][:, :1]
                s_n = s_n + (fref - fn_ref[h]) * LOG2E
                s_c = s_c + (fref - fc_ref[h]) * LOG2E
            s_n = jnp.where(mask, s_n, NEG_INF)
            m0 = jnp.full((tq, 1), NEG_INF, F32)
            z0 = jnp.zeros((tq, 1), F32)
            m, l, acc = _softmax_tile(s_n, vn, m0, z0, jnp.zeros((tq, HEAD_DIM), F32))
            m, l, acc = _softmax_tile(s_c, vc, m, l, acc)
            out = acc / l
        o_ref[:, h * HEAD_DIM:(h + 1) * HEAD_DIM] = out


def attn_decode(kind, q, kn, vn, kc, vc, layer=None, fn=None, fc=None):
    native = layer is not None
    nb, past = (kc.shape[1], kc.shape[2] // N_HEADS) if native else (kc.shape[0], kc.shape[1])
    tq = q.shape[0] // nb
    new = lambda a: pl.BlockSpec((tq, a.shape[1]), lambda b: (b, 0))
    if native:
        old = lambda a: pl.BlockSpec((None, None, past * N_HEADS, HEAD_DIM), lambda b: (layer, b, 0, 0))
    else:
        old = lambda a: pl.BlockSpec((None, past, a.shape[2]), lambda b: (b, 0, 0))
    in_specs = [new(q), new(kn), new(vn), old(kc), old(vc)]
    args = [q, kn, vn, kc, vc]
    if kind == "fox":
        in_specs += [pl.BlockSpec((None, N_HEADS, 1, tq), lambda b: (b, 0, 0, 0)),
                     pl.BlockSpec((None, N_HEADS, 1, past), lambda b: (b, 0, 0, 0))]
        args += [fn, fc]
    return pl.pallas_call(
        functools.partial(_attn_decode_kernel, kind, native),
        grid=(nb,),
        in_specs=in_specs,
        out_specs=pl.BlockSpec((tq, W_GROUP), lambda b: (b, 0)),
        out_shape=jax.ShapeDtypeStruct((nb * tq, W_GROUP), F32),
        compiler_params=_params("parallel"),
        name="attn_decode_" + kind,
    )(*args)


def _out_proj_kernel(oa_ref, ob_ref, oc_ref, ga_ref, gb_ref, gc_ref, w_ref, x_ref, o_ref):
    acc = x_ref[...]
    for g, (o, gn) in enumerate(((oa_ref, ga_ref), (ob_ref, gb_ref), (oc_ref, gc_ref))):
        y = _rms(o[...], gn[...]).astype(BF16)
        acc = acc + _dot(y, w_ref[g * W_GROUP:(g + 1) * W_GROUP, :])
    o_ref[...] = acc


def out_proj(oa, ob, oc, p, x, l):
    m = x.shape[0]
    tm = _pick_tile(m, 512)
    row = lambda w: pl.BlockSpec((tm, w), lambda i: (i, 0))
    full = lambda a: pl.BlockSpec(a.shape, lambda i: (0,) * a.ndim)
    consts = (p["out_norm_a"], p["out_norm_b"], p["out_norm_c"], p["w_out"])
    w_spec = pl.BlockSpec((None,) + p["w_out"].shape[1:], lambda i: (l, 0, 0))
    return pl.pallas_call(
        _out_proj_kernel,
        grid=(m // tm,),
        in_specs=[row(W_GROUP)] * 3 + [full(a) for a in consts[:3]] + [w_spec, row(D_MODEL)],
        out_specs=row(D_MODEL),
        out_shape=jax.ShapeDtypeStruct((m, D_MODEL), F32),
        compiler_params=_params("parallel"),
        name="out_proj",
    )(oa, ob, oc, *consts, x)


def _ffn_kernel(x_ref, g_ref, wg_ref, wu_ref, wd_ref, o_ref, h_ref):
    @pl.when(pl.program_id(1) == 0)
    def _():
        x = x_ref[...]
        h_ref[...] = _rms(x, g_ref[...]).astype(BF16)
        o_ref[...] = x

    h = h_ref[...]
    gate = _dot(h, wg_ref[...])
    up = _dot(h, wu_ref[...])
    act = (gate * jax.nn.sigmoid(gate) * up).astype(BF16)
    o_ref[...] += _dot(act, wd_ref[...])


def ffn(x, g, w_gu, w_down, l):
    m = x.shape[0]
    tm = _pick_tile(m, 1024)
    tf = 512
    nf = D_FF // tf
    return pl.pallas_call(
        _ffn_kernel,
        grid=(m // tm, nf),
        in_specs=[pl.BlockSpec((tm, D_MODEL), lambda i, j: (i, 0)),
                  pl.BlockSpec((1, D_MODEL), lambda i, j: (0, 0)),
                  pl.BlockSpec((None, D_MODEL, tf), lambda i, j: (l, 0, j)),
                  pl.BlockSpec((None, D_MODEL, tf), lambda i, j: (l, 0, j + nf)),
                  pl.BlockSpec((None, tf, D_MODEL), lambda i, j: (l, j, 0))],
        out_specs=pl.BlockSpec((tm, D_MODEL), lambda i, j: (i, 0)),
        out_shape=jax.ShapeDtypeStruct((m, D_MODEL), F32),
        scratch_shapes=[pltpu.VMEM((tm, D_MODEL), BF16)],
        compiler_params=_params("parallel", "arbitrary"),
        name="ffn",
    )(x, g, w_gu, w_gu, w_down)


def _dup_rope(r):
    half = MLA_ROPE_DIM // 2
    return jnp.concatenate([r, r[..., half:], r[..., :half]], axis=-1)


def _prep_layer(l, fox_f_bias, fox_q_norm, fox_k_norm, mla_qa_norm, mla_w_uq, mla_kva_norm, mla_w_ukv,
                mla_q_norm, mla_k_norm, out_norm_a, out_norm_b, out_norm_c, w_in, w_out, w_gu, w_down,
                norm_mix, norm_ffn):
    uq = mla_w_uq[l].reshape(MLA_Q_RANK, N_HEADS, MLA_QK_DIM)
    uq = jnp.concatenate([uq[..., :MLA_NOPE_DIM], _dup_rope(uq[..., MLA_NOPE_DIM:])], axis=-1)
    ukv = mla_w_ukv[l].reshape(MLA_KV_RANK, N_HEADS, 2 * LANES)
    ukv = jnp.concatenate([ukv[..., :LANES].reshape(MLA_KV_RANK, W_GROUP),
                           ukv[..., LANES:].reshape(MLA_KV_RANK, W_GROUP)], axis=1)

    def gain256(g):
        return jnp.concatenate([g[:MLA_NOPE_DIM], _dup_rope(g[MLA_NOPE_DIM:])])[None, :]

    row = lambda a: a[l][None, :]
    return {
        "norm_mix": row(norm_mix), "w_in": w_in,
        "fox_q_norm": row(fox_q_norm), "fox_k_norm": row(fox_k_norm),
        "fox_f_bias": jnp.pad(fox_f_bias[l], (0, LANES - N_HEADS))[None, :],
        "mla_qa_norm": row(mla_qa_norm),
        "mla_w_uq": uq.reshape(MLA_Q_RANK, N_HEADS * MLA_HEAD_PAD).astype(BF16),
        "mla_q_gain": gain256(mla_q_norm[l]) * (MLA_QK_DIM ** -0.5 * LOG2E),
        "mla_kva_norm": row(mla_kva_norm),
        "mla_w_ukv": ukv.astype(BF16),
        "mla_k_gain": gain256(mla_k_norm[l]),
        "out_norm_a": row(out_norm_a), "out_norm_b": row(out_norm_b), "out_norm_c": row(out_norm_c),
        "w_out": w_out,
        "norm_ffn": row(norm_ffn), "w_gu": w_gu, "w_down": w_down,
    }


def _rope_tables(pos):
    half = MLA_ROPE_DIM // 2
    inv_freq = ROPE_THETA ** (-(jnp.arange(half, dtype=F32) / half))
    ang = pos.astype(F32)[:, None] * inv_freq[None, :]
    cos, sin = jnp.cos(ang), jnp.sin(ang)
    zero = jnp.zeros_like(cos)
    return (jnp.concatenate([cos, cos, zero, zero], axis=1),
            jnp.concatenate([-sin, sin, zero, zero], axis=1))


def _layer(x, caches, l, depth, p, tabs, stacks):
    nb, t, _ = x.shape
    m = nb * t
    x2 = x.reshape(m, D_MODEL)
    prompt = caches is None
    outs = inproj_post(x2, p, *tabs["q"], with_kt=prompt, l=l, depth=depth, stacks=stacks)
    stacks = outs[:5]
    qa, vab, lf, qb, vbb, qc, krd, k2a, k2b = outs[5:]
    if prompt:
        kct, vc_new = kexp(stacks[4], krd, p, *tabs["q"], transposed=True, layer=l)
        f = cumsum_rows(lf.reshape(nb, t, LANES))
        oa = attn_prompt("fox", qa, k2a, vab, f[0, :N_HEADS])
        ob = attn_prompt("sb", qb, k2b, vbb)
        oc = attn_prompt("mla", qc, kct, vc_new)
    else:
        c_fk, c_fv, c_lf, c_sk, c_sv, c_ckv, c_kr = caches
        pl_ = c_fk.shape[2]
        kc_new, vc_new = kexp(stacks[4], krd, p, *tabs["q"], transposed=False, layer=l)
        lf_all = jnp.concatenate(
            [jnp.pad(c_lf[l], ((0, 0), (0, 0), (0, LANES - N_HEADS))), lf.reshape(nb, t, LANES)], axis=1)
        n_pad = -(-(pl_ + t) // LANES) * LANES
        lf_all = jnp.pad(lf_all, ((0, 0), (0, n_pad - pl_ - t), (0, 0)))
        f = cumsum_rows(lf_all)[:, :N_HEADS, None, :]
        kc_old, vc_old = kexp(c_ckv.reshape(c_ckv.shape[0], nb * pl_, MLA_KV_RANK),
                              _dup_rope(c_kr[l]).reshape(nb * pl_, LANES), p, *tabs["kc"],
                              transposed=False, layer=l)
        rows4 = lambda a: a.reshape(a.shape[0], nb, pl_ * N_HEADS, HEAD_DIM)
        oa = attn_decode("fox", qa, k2a, vab, rows4(c_fk), rows4(c_fv), layer=l,
                         fn=f[..., pl_:pl_ + t], fc=f[..., :pl_])
        ob = attn_decode("sb", qb, k2b, vbb, rows4(c_sk), rows4(c_sv), layer=l)
        oc = attn_decode("mla", qc, kc_new, vc_new, kc_old.reshape(nb, pl_, -1), vc_old.reshape(nb, pl_, -1))
    x2 = out_proj(oa, ob, oc, p, x2, l)
    x2 = ffn(x2, p["norm_ffn"], p["w_gu"], p["w_down"], l)
    small = (lf[:, :N_HEADS].reshape(nb, t, N_HEADS), krd[:, :MLA_ROPE_DIM].reshape(nb, t, MLA_ROPE_DIM))
    return x2.reshape(nb, t, D_MODEL), stacks, small


def kernel(x_prompt, x_sample, cache_fox_k, cache_fox_v, cache_fox_logf, cache_sb_k, cache_sb_v, cache_mla_ckv, cache_mla_krope, norm_mix, w_in, fox_f_bias, fox_q_norm, fox_k_norm, mla_qa_norm, mla_w_uq, mla_kva_norm, mla_w_ukv, mla_q_norm, mla_k_norm, out_norm_a, out_norm_b, out_norm_c, w_out, norm_ffn, w_gu, w_down):
    depth = w_in.shape[0]
    t_p = x_prompt.shape[1]
    nb_s, t_s = x_sample.shape[0], x_sample.shape[1]
    past_len = cache_fox_k.shape[2]
    tabs_p = {"q": _rope_tables(jnp.arange(t_p, dtype=jnp.int32))}
    tabs_s = {"q": _rope_tables(jnp.tile(past_len + jnp.arange(t_s, dtype=jnp.int32), nb_s)),
              "kc": _rope_tables(jnp.tile(jnp.arange(past_len, dtype=jnp.int32), nb_s))}
    caches = (cache_fox_k, cache_fox_v, cache_fox_logf, cache_sb_k, cache_sb_v, cache_mla_ckv, cache_mla_krope)
    y_p, y_s = x_prompt, x_sample
    rows_p, rows_s = [], []
    def new_stacks(m):
        heads = tuple(jnp.zeros((depth, m, N_HEADS, HEAD_DIM), F32) for _ in range(4))
        return heads + (jnp.zeros((depth, m, MLA_KV_RANK), F32),)

    stacks_p = new_stacks(x_prompt.shape[0] * t_p)
    stacks_s = new_stacks(nb_s * t_s)
    w_in_b = w_in_prep(w_in)
    w_out_b, w_gu_b, w_down_b = w_out.astype(BF16), w_gu.astype(BF16), w_down.astype(BF16)
    for l in range(depth):
        p = _prep_layer(l, fox_f_bias, fox_q_norm, fox_k_norm, mla_qa_norm, mla_w_uq, mla_kva_norm, mla_w_ukv,
                        mla_q_norm, mla_k_norm, out_norm_a, out_norm_b, out_norm_c, w_in_b, w_out_b, w_gu_b,
                        w_down_b, norm_mix, norm_ffn)
        y_p, stacks_p, r_p = _layer(y_p, None, l, depth, p, tabs_p, stacks_p)
        y_s, stacks_s, r_s = _layer(y_s, caches, l, depth, p, tabs_s, stacks_s)
        rows_p.append(r_p)
        rows_s.append(r_s)

    def assemble(stacks, rows, nb, t):
        fk, fv, sk, sv, ckv = stacks
        heads = lambda a: a.reshape(depth, nb, t, N_HEADS, HEAD_DIM)
        return (heads(fk), heads(fv), jnp.stack([r[0] for r in rows], axis=0), heads(sk), heads(sv),
                ckv.reshape(depth, nb, t, MLA_KV_RANK), jnp.stack([r[1] for r in rows], axis=0))

    return ((y_p, y_s) + assemble(stacks_p, rows_p, x_prompt.shape[0], t_p)
            + assemble(stacks_s, rows_s, nb_s, t_s))
```

```python
import functools
import math

import jax
import jax.numpy as jnp
from jax import lax
from jax.experimental import pallas as pl
from jax.experimental.pallas import tpu as pltpu

D_MODEL = 2048
CHUNK = 64
HEAD_DIM = 128
N_HEADS = 4
W_GROUP = N_HEADS * HEAD_DIM
MLA_Q_RANK = 512
MLA_KV_RANK = 256
MLA_NOPE_DIM = 128
MLA_ROPE_DIM = 64
MLA_QK_DIM = MLA_NOPE_DIM + MLA_ROPE_DIM
MLA_HEAD_PAD = 256
ROPE_THETA = 10000.0
D_FF = 5632
RMS_EPS = 1e-6
NEG_INF = -1e30
LOG2E = math.log2(math.e)
N_IN_PAD = 4096
LANES = 128
TRI = 256
ATTN_TILE = 512
SB_TILE = 512
INPROJ_ROWS = 256
PROJ_CHUNK = 1024
ROW_GROUP = 128
STABILISER_LIMIT = 48.0
UNDERFLOW_BITS = 160.0
VMEM_LIMIT = 56 * 1024 * 1024

C_QA, C_KA, C_VA, C_QB, C_KB, C_VB, C_CQ, C_CKV, C_KR, C_FA = (
    0, 512, 1024, 1536, 2048, 2560, 3072, 3584, 3840, 3968)

BF16 = jnp.bfloat16
F32 = jnp.float32


def _params(*sem):
    return pltpu.CompilerParams(dimension_semantics=sem, vmem_limit_bytes=VMEM_LIMIT)


def _pick_tile(n, pref):
    t = min(n, pref)
    while n % t:
        t //= 2
    return t


def _rms(x, g):
    return x * lax.rsqrt(jnp.mean(x * x, axis=-1, keepdims=True) + RMS_EPS) * g


def _dot(a, b):
    return jnp.dot(a, b, preferred_element_type=F32)


def _dot_nt(a, b):
    return lax.dot_general(a, b, (((1,), (1,)), ((), ())), preferred_element_type=F32)


def _split2(x):
    hi = x.astype(BF16)
    lo = (x - hi.astype(F32)).astype(BF16)
    return hi, lo


def _neg_abs(x):
    bits = lax.bitcast_convert_type(x, jnp.uint32) | jnp.uint32(0x80000000)
    return lax.bitcast_convert_type(bits, F32)


def _split3(x):
    a = x.astype(BF16)
    r = x - a.astype(F32)
    b = r.astype(BF16)
    c = (r - b.astype(F32)).astype(BF16)
    return a, b, c


def _log_sigmoid(x):
    return -(jnp.maximum(-x, 0.0) + jnp.log1p(jnp.exp(-jnp.abs(x))))


def _log2_keep(z2):
    return -(jnp.maximum(z2, 0.0) + jnp.log2(1.0 + jnp.exp2(_neg_abs(z2))))


def _w_in_prep_kernel(w_ref, o_ref):
    rows = w_ref.shape[0]
    fa0 = C_QB
    o_ref[:, :C_QB] = w_ref[:, :fa0].astype(BF16)
    o_ref[:, C_QB:C_KR] = w_ref[:, fa0 + N_HEADS:C_KR + N_HEADS].astype(BF16)
    kr = w_ref[:, C_KR + N_HEADS:C_KR + N_HEADS + MLA_ROPE_DIM]
    half = MLA_ROPE_DIM // 2
    o_ref[:, C_KR:C_FA] = jnp.concatenate([kr, kr[:, half:], kr[:, :half]], axis=1).astype(BF16)
    fa = w_ref[:, fa0:fa0 + N_HEADS]
    o_ref[:, C_FA:] = jnp.concatenate([fa, jnp.zeros((rows, LANES - N_HEADS), F32)], axis=1).astype(BF16)


def w_in_prep(w_in):
    depth, d, n = w_in.shape
    tr = 256
    return pl.pallas_call(
        _w_in_prep_kernel,
        grid=(depth, d // tr),
        in_specs=[pl.BlockSpec((None, tr, n), lambda l, i: (l, i, 0))],
        out_specs=pl.BlockSpec((None, tr, N_IN_PAD), lambda l, i: (l, i, 0)),
        out_shape=jax.ShapeDtypeStruct((depth, d, N_IN_PAD), BF16),
        compiler_params=_params("parallel", "parallel"),
        name="w_in_prep",
    )(w_in)


def _rope_chunk(y2, cos, sin):
    return y2 * cos + pltpu.roll(y2, 64, 1) * sin


def _rope_lane_mask():
    return lax.broadcasted_iota(jnp.int32, (1, LANES), 1) < MLA_ROPE_DIM


def _inproj_post_kernel(with_kt, n_alias, x_ref, g_ref, w_ref, fqg_ref, fkg_ref, fb_ref, qag_ref, wuq_ref, qg_ref,
                        kvg_ref, cos_ref, sin_ref, *refs):
    refs = refs[n_alias:]
    (ka_o, va_o, kb_o, vb_o, ckv_o, qa_o, vab_o, lf_o, qb_o, vbb_o, qc_o, kr_o, k2a_o, k2b_o,
     proj_ref) = refs
    hn = _rms(x_ref[...], g_ref[...]).astype(BF16)
    for c in range(N_IN_PAD // PROJ_CHUNK):
        cols = slice(c * PROJ_CHUNK, (c + 1) * PROJ_CHUNK)
        proj_ref[:, cols] = _dot(hn, w_ref[:, cols])
    scale = HEAD_DIM ** -0.5 * LOG2E
    va = proj_ref[:, C_VA:C_VA + W_GROUP]
    vb = proj_ref[:, C_VB:C_VB + W_GROUP]
    for h in range(N_HEADS):
        sl = slice(h * HEAD_DIM, (h + 1) * HEAD_DIM)
        qa = proj_ref[:, C_QA + h * HEAD_DIM:C_QA + (h + 1) * HEAD_DIM]
        qa_o[:, sl] = (_rms(qa, fqg_ref[...]) * scale).astype(BF16)
        ka = _rms(proj_ref[:, C_KA + h * HEAD_DIM:C_KA + (h + 1) * HEAD_DIM], fkg_ref[...])
        kb = proj_ref[:, C_KB + h * HEAD_DIM:C_KB + (h + 1) * HEAD_DIM]
        ka_o[:, h, :] = ka
        kb_o[:, h, :] = kb
        va_o[:, h, :] = va[:, sl]
        vb_o[:, h, :] = vb[:, sl]
        if with_kt:
            k2a_o[h, 0] = ka.T.astype(BF16)
            k2b_o[h, 0] = kb.T.astype(BF16)
        else:
            k2a_o[:, sl] = ka.astype(BF16)
            k2b_o[:, sl] = kb.astype(BF16)
    vab_o[...] = va.astype(BF16)
    qb_o[...] = (proj_ref[:, C_QB:C_QB + W_GROUP] * scale).astype(BF16)
    vbb_o[...] = vb.astype(BF16)

    lane = lax.broadcasted_iota(jnp.int32, (1, LANES), 1)
    lf = _log_sigmoid(proj_ref[:, C_FA:C_FA + LANES] + fb_ref[...])
    lf_o[...] = jnp.where(lane < N_HEADS, lf, 0.0)

    cqn = _rms(proj_ref[:, C_CQ:C_CQ + MLA_Q_RANK], qag_ref[...]).astype(BF16)
    qc = _dot(cqn, wuq_ref[...])
    rmask = _rope_lane_mask()
    cos = cos_ref[...]
    sin = sin_ref[...]
    for h in range(N_HEADS):
        c1 = qc[:, h * MLA_HEAD_PAD:h * MLA_HEAD_PAD + LANES]
        c2 = qc[:, h * MLA_HEAD_PAD + LANES:(h + 1) * MLA_HEAD_PAD]
        ss = (jnp.sum(c1 * c1, axis=-1, keepdims=True)
              + jnp.sum(jnp.where(rmask, c2 * c2, 0.0), axis=-1, keepdims=True))
        rs = lax.rsqrt(ss * (1.0 / MLA_QK_DIM) + RMS_EPS)
        y1 = c1 * rs * qg_ref[:, :LANES]
        y2 = c2 * rs * qg_ref[:, LANES:]
        qc_o[:, h * MLA_HEAD_PAD:h * MLA_HEAD_PAD + LANES] = y1.astype(BF16)
        qc_o[:, h * MLA_HEAD_PAD + LANES:(h + 1) * MLA_HEAD_PAD] = _rope_chunk(y2, cos, sin).astype(BF16)

    ckv_o[...] = _rms(proj_ref[:, C_CKV:C_CKV + MLA_KV_RANK], kvg_ref[...])
    kr_o[...] = proj_ref[:, C_KR:C_KR + LANES]


def inproj_post(x, p, cos, sin, with_kt, l, depth, stacks):
    m = x.shape[0]
    tm = _pick_tile(m, INPROJ_ROWS)
    row =lambda w: pl.BlockSpec((tm, w), lambda i: (i, 0))
    full = lambda a: pl.BlockSpec(a.shape, lambda i: (0,) * a.ndim)
    consts = (p["fox_q_norm"], p["fox_k_norm"], p["fox_f_bias"], p["mla_qa_norm"], p["mla_w_uq"],
              p["mla_q_gain"], p["mla_kva_norm"])
    w_spec = pl.BlockSpec((None, D_MODEL, N_IN_PAD), lambda i: (l, 0, 0), pipeline_mode=pl.Buffered(1))
    heads_spec = pl.BlockSpec((None, tm, N_HEADS, HEAD_DIM), lambda i: (l, i, 0, 0))
    heads_shape = jax.ShapeDtypeStruct((depth, m, N_HEADS, HEAD_DIM), F32)
    out_specs = [heads_spec] * 4 + [pl.BlockSpec((None, tm, MLA_KV_RANK), lambda i: (l, i, 0))]
    out_shape = [heads_shape] * 4 + [jax.ShapeDtypeStruct((depth, m, MLA_KV_RANK), F32)]
    outs = [(W_GROUP, BF16), (W_GROUP, BF16), (LANES, F32), (W_GROUP, BF16), (W_GROUP, BF16),
            (N_HEADS * MLA_HEAD_PAD, BF16), (LANES, F32)]
    out_specs += [row(w) for w, _ in outs]
    out_shape += [jax.ShapeDtypeStruct((m, w), dt) for w, dt in outs]
    for kt_pref in (ATTN_TILE, SB_TILE):
        if with_kt:
            kt_tile = _pick_tile(m, kt_pref)
            sub = kt_tile // tm
            out_specs.append(pl.BlockSpec((N_HEADS, 1, HEAD_DIM, tm),
                                          lambda i, sub=sub: (0, i // sub, 0, i % sub)))
            out_shape.append(jax.ShapeDtypeStruct((N_HEADS, m // kt_tile, HEAD_DIM, kt_tile), BF16))
        else:
            out_specs.append(row(W_GROUP))
            out_shape.append(jax.ShapeDtypeStruct((m, W_GROUP), BF16))
    n_in = 3 + len(consts) + 2
    stacks = () if stacks is None else tuple(stacks)
    return pl.pallas_call(
        functools.partial(_inproj_post_kernel, with_kt, len(stacks)),
        grid=(m // tm,),
        in_specs=([row(D_MODEL), full(p["norm_mix"]), w_spec] + [full(a) for a in consts]
                  + [row(LANES), row(LANES)] + [pl.BlockSpec(memory_space=pl.ANY)] * len(stacks)),
        out_specs=out_specs,
        out_shape=out_shape,
        scratch_shapes=[pltpu.VMEM((tm, N_IN_PAD), F32)],
        input_output_aliases={n_in + k: k for k in range(len(stacks))},
        compiler_params=_params("parallel"),
        name="inproj_post",
    )(x, p["norm_mix"], p["w_in"], *consts, cos, sin, *stacks)


def _kexp_kernel(transposed, ckv_ref, kr_ref, w_ref, kg_ref, cos_ref, sin_ref, kc_o, vc_o):
    kv = _dot(ckv_ref[...].astype(BF16), w_ref[...])
    kr = kr_ref[...]
    krsq = jnp.sum(jnp.where(_rope_lane_mask(), kr * kr, 0.0), axis=-1, keepdims=True)
    cos = cos_ref[...]
    sin = sin_ref[...]
    for h in range(N_HEADS):
        n = kv[:, h * LANES:(h + 1) * LANES]
        ss = jnp.sum(n * n, axis=-1, keepdims=True) + krsq
        rs = lax.rsqrt(ss * (1.0 / MLA_QK_DIM) + RMS_EPS)
        k1 = n * rs * kg_ref[:, :LANES]
        k2 = _rope_chunk(kr * rs * kg_ref[:, LANES:], cos, sin)
        if transposed:
            kc_o[h, 0, :LANES, :] = k1.T.astype(BF16)
            kc_o[h, 0, LANES:, :] = k2.T.astype(BF16)
        else:
            kc_o[:, h * MLA_HEAD_PAD:h * MLA_HEAD_PAD + LANES] = k1.astype(BF16)
            kc_o[:, h * MLA_HEAD_PAD + LANES:(h + 1) * MLA_HEAD_PAD] = k2.astype(BF16)
    vc_o[...] = kv[:, W_GROUP:].astype(BF16)


def kexp(ckv, krdup, p, cos, sin, transposed, layer=None):
    m = krdup.shape[0]
    tm = _pick_tile(m, ATTN_TILE)
    row = lambda w: pl.BlockSpec((tm, w), lambda i: (i, 0))
    full = lambda a: pl.BlockSpec(a.shape, lambda i: (0,) * a.ndim)
    ckv_spec = row(MLA_KV_RANK) if layer is None else pl.BlockSpec(
        (None, tm, MLA_KV_RANK), lambda i: (layer, i, 0))
    if transposed:
        kc_spec = pl.BlockSpec((N_HEADS, 1, MLA_HEAD_PAD, tm), lambda i: (0, i, 0, 0))
        kc_shape = jax.ShapeDtypeStruct((N_HEADS, m // tm, MLA_HEAD_PAD, tm), BF16)
    else:
        kc_spec = row(N_HEADS * MLA_HEAD_PAD)
        kc_shape = jax.ShapeDtypeStruct((m, N_HEADS * MLA_HEAD_PAD), BF16)
    return pl.pallas_call(
        functools.partial(_kexp_kernel, transposed),
        grid=(m // tm,),
        in_specs=[ckv_spec, row(LANES), full(p["mla_w_ukv"]), full(p["mla_k_gain"]), row(LANES), row(LANES)],
        out_specs=[kc_spec, row(W_GROUP)],
        out_shape=[kc_shape, jax.ShapeDtypeStruct((m, W_GROUP), BF16)],
        compiler_params=_params("parallel"),
        name="kexp",
    )(ckv, krdup, p["mla_w_ukv"], p["mla_k_gain"], cos, sin)


def _cumsum_kernel(x_ref, o_ref, carry_ref):
    @pl.when(pl.program_id(1) == 0)
    def _():
        carry_ref[...] = jnp.zeros_like(carry_ref)

    t = x_ref.shape[1]
    xt = x_ref[0].T[:8, :]
    ri = lax.broadcasted_iota(jnp.int32, (t, t), 0)
    ci = lax.broadcasted_iota(jnp.int32, (t, t), 1)
    upper = jnp.where(ri <= ci, 1.0, 0.0).astype(BF16)
    a, b, c = _split3(xt)
    f = _dot(a, upper) + _dot(b, upper) + _dot(c, upper) + carry_ref[:, :1]
    o_ref[0] = f
    carry_ref[...] = jnp.broadcast_to(f[:, t - 1:t], carry_ref.shape)


def cumsum_rows(x):
    b, n, _ = x.shape
    t = n if n <= 1280 else _pick_tile(n, 512)
    return pl.pallas_call(
        _cumsum_kernel,
        grid=(b, n // t),
        in_specs=[pl.BlockSpec((1, t, LANES), lambda i, j: (i, j, 0))],
        out_specs=pl.BlockSpec((1, 8, t), lambda i, j: (i, 0, j)),
        out_shape=jax.ShapeDtypeStruct((b, 8, n), F32),
        scratch_shapes=[pltpu.VMEM((8, LANES), F32)],
        compiler_params=_params("parallel", "arbitrary"),
        name="cumsum_rows",
    )(x)


def _tile_mask(kind, qpos, kpos):
    if kind == "fox":
        return kpos <= qpos
    if kind == "sb":
        return kpos < qpos
    shift = CHUNK.bit_length() - 1
    return lax.shift_right_logical(kpos, shift) <= lax.shift_right_logical(qpos, shift)


def _tri_lower(n):
    ri = lax.broadcasted_iota(jnp.int32, (n, n), 0)
    ci = lax.broadcasted_iota(jnp.int32, (n, n), 1)
    return jnp.where(ri >= ci, 1.0, 0.0).astype(BF16)


def _sb_weights(z, carry, mask, tri):
    lk = _log2_keep(z)
    if mask is not None:
        lk = jnp.where(mask, lk, 0.0)
    blk = tri.shape[0]
    nblk = z.shape[1] // blk
    parts = [None] * nblk
    for c in reversed(range(nblk)):
        lkc = lk[:, c * blk:(c + 1) * blk]
        hi, lo = _split2(lkc)
        intra = _dot(hi, tri) + _dot(lo, tri)
        if carry.shape[1] == 1 or carry.shape[1] == blk:
            parts[c] = intra + carry
        else:
            parts[c] = intra + jnp.concatenate([carry] * (blk // carry.shape[1]), axis=1)
        carry = carry + jnp.sum(lkc, axis=-1, keepdims=True)
    r = parts[0] if nblk == 1 else jnp.concatenate(parts, axis=-1)
    a = jnp.exp2(z + r)
    if mask is not None:
        a = jnp.where(mask, a, 0.0)
    return a, carry


def _attn_prompt_kernel(kind, tq, *refs):
    f_ref = al_ref = kmax_ref = None
    if kind == "fox":
        q_ref, kt_ref, v_ref, f_ref, o_ref, m_ref, acc_ref, s_ref, p_ref, al_ref, kmax_ref = refs
    elif kind == "mla":
        q_ref, kt_ref, v_ref, o_ref, m_ref, acc_ref, s_ref, p_ref, al_ref, kmax_ref = refs
    else:
        q_ref, kt_ref, v_ref, o_ref, m_ref, acc_ref, s_ref, p_ref, kmax_ref = refs
    qb = pl.program_id(1)
    rg = min(TRI if kind == "sb" else ROW_GROUP, tq)
    n_rg = tq // rg
    q = q_ref[...]
    ones = jnp.ones((tq, LANES), BF16)
    tri2 = jnp.concatenate([_tri_lower(rg)] * 2, axis=0) if kind == "sb" else None
    fref = f_ref[0, qb][:, :1] if kind == "fox" else None

    def key_block(j):
        return jnp.clip(qb - j, 0, qb)

    def stage_a(j, slot):
        s_ref[slot] = _dot(q, kt_ref[0, key_block(j)])

    def stage_b(j, slot, diagonal, fixed=False):
        brow = (fref - f_ref[0, key_block(j)]) * LOG2E if kind == "fox" else None
        ms, als, prs = [], [], []
        for r in range(n_rg):
            rows = slice(r * rg, (r + 1) * rg)
            kw = (r + 1) * rg if diagonal else tq
            s = s_ref[slot, rows, :kw]
            mask = None
            if diagonal:
                qpos = r * rg + lax.broadcasted_iota(jnp.int32, (rg, kw), 0)
                kpos = lax.broadcasted_iota(jnp.int32, (rg, kw), 1)
                mask = _tile_mask(kind, qpos, kpos)
            m_prev = m_ref[rows, :]
            if kind == "sb":
                plk = jnp.maximum(s, 0.0) + jnp.log2(1.0 + jnp.exp2(_neg_abs(s)))
                if mask is not None:
                    plk = jnp.where(mask, plk, 0.0)
                carry = m_prev
                parts = [None] * (kw // rg)
                for c in reversed(range(kw // rg)):
                    pc = plk[:, c * rg:(c + 1) * rg]
                    hi, lo = _split2(pc)
                    later = _dot(jnp.concatenate([hi, lo], axis=1), tri2)
                    parts[c] = s[:, c * rg:(c + 1) * rg] - later - jnp.concatenate([carry] * (rg // LANES), 1)
                    carry = carry + jnp.sum(pc, axis=-1, keepdims=True)
                a = jnp.exp2(parts[0] if len(parts) == 1 else jnp.concatenate(parts, axis=1))
                if mask is not None:
                    a = jnp.where(mask, a, 0.0)
                pr = a.astype(BF16)
                ms.append(carry)
            else:
                if kind == "fox":
                    s = s + brow[:, :kw]
                if mask is not None:
                    s = jnp.where(mask, s, NEG_INF)
                chunks = [s[:, c * LANES:(c + 1) * LANES] for c in range(kw // LANES)]
                if fixed:
                    m_new = zb[rows, :]
                else:
                    m_cur = jnp.max(functools.reduce(jnp.maximum, chunks), axis=-1, keepdims=True)
                    m_new = jnp.maximum(m_prev, m_cur)
                    als.append(jnp.exp2(m_prev - m_new))
                    ms.append(m_new)
                pr = jnp.concatenate([jnp.exp2(c - m_new).astype(BF16) for c in chunks], axis=1)
            if kw < tq:
                pr = jnp.concatenate([pr, jnp.zeros((rg, tq - kw), BF16)], axis=1)
            prs.append(pr)
        p_ref[slot] = jnp.concatenate(prs, axis=0)
        if not fixed:
            m_ref[...] = jnp.concatenate(ms, axis=0)
            if kind != "sb":
                al_ref[slot] = jnp.concatenate(als, axis=0)

    def stage_c(j, slot, fixed=False):
        start = pl.multiple_of(key_block(j) * tq, tq)
        v = v_ref[pl.ds(start, tq), :]
        if kind == "sb":
            acc_ref[...] += _dot(p_ref[slot], v)
        else:
            pv = _dot(p_ref[slot], jnp.concatenate([v, ones], axis=1))
            if fixed:
                acc_ref[...] += pv
            else:
                al = al_ref[slot]
                acc_ref[...] = jnp.concatenate([al, al], axis=1) * acc_ref[...] + pv

    def step(j, slot, fixed=False):
        stage_b(j, slot, False, fixed)
        stage_a(j + 1, 1 - slot)
        stage_c(j - 1, 1 - slot, fixed)

    @pl.when(qb == 0)
    def _():
        def norm_body(i, mx):
            kt = kt_ref[0, i].astype(F32)
            return jnp.maximum(mx, jnp.max(jnp.sum(kt * kt, axis=0, keepdims=True), axis=1, keepdims=True))

        mx = lax.fori_loop(0, kt_ref.shape[1], norm_body, jnp.zeros((1, 1), F32))
        kmax_ref[...] = jnp.broadcast_to(jnp.sqrt(mx), kmax_ref.shape)

    qf = q.astype(F32)
    zb = jnp.sqrt(jnp.sum(qf * qf, axis=1, keepdims=True)) * kmax_ref[:1, :]

    if kind == "mla":
        def dense(fixed, unroll):
            m_ref[...] = jnp.full(m_ref.shape, NEG_INF, F32)
            acc_ref[...] = jnp.zeros_like(acc_ref)
            stage_a(0, 0)
            stage_b(0, 0, True, fixed)
            stage_a(1, 1)

            def body(t, c):
                for u in range(unroll):
                    step(unroll * t + 1 + u, (1 + u) % 2, fixed)
                return c

            trips = qb // unroll
            lax.fori_loop(0, trips, body, 0)
            for u in range(unroll - 1):
                @pl.when(qb - unroll * trips > u)
                def _():
                    step(unroll * trips + 1 + u, (1 + u) % 2, fixed)

            for slot in range(2):
                @pl.when(qb % 2 == slot)
                def _():
                    stage_c(qb, slot, fixed)

            o_ref[...] = acc_ref[:, :HEAD_DIM] / acc_ref[:, HEAD_DIM:]

        small = jnp.max(zb) < STABILISER_LIMIT

        @pl.when(small)
        def _():
            dense(True, 4)

        @pl.when(jnp.logical_not(small))
        def _():
            dense(False, 2)

        return

    def exhausted(j):
        if kind == "sb":
            return jnp.min(m_ref[...] - zb) > UNDERFLOW_BITS
        brow_max = jnp.max((fref - f_ref[0, key_block(j + 1)]) * LOG2E)
        return jnp.min(m_ref[...] - zb) - brow_max > UNDERFLOW_BITS

    m_ref[...] = jnp.full(m_ref.shape, 0.0 if kind == "sb" else NEG_INF, F32)
    acc_ref[...] = jnp.zeros_like(acc_ref)
    stage_a(0, 0)
    stage_b(0, 0, True)
    stage_a(1, 1)

    if kind == "sb":
        def cond1(c):
            return jnp.logical_and(c[0] <= qb, jnp.logical_not(c[1]))

        def single(c):
            j = c[0]
            step(j, j & 1)
            return j + 1, exhausted(j)

        nxt, _ = lax.while_loop(cond1, single, (jnp.int32(1), jnp.bool_(False)))
        stage_c(nxt - 1, (nxt - 1) & 1)
        o_ref[...] = acc_ref[...]
        return

    def cond(c):
        return jnp.logical_and(c[0] < qb // 2, jnp.logical_not(c[1]))

    def pair(c):
        t = c[0]
        step(1 + 2 * t, 1)
        step(2 + 2 * t, 0)
        return t + 1, exhausted(2 + 2 * t)

    pairs, done = lax.while_loop(cond, pair, (jnp.int32(0), jnp.bool_(False)))
    last = 2 * pairs
    tail = jnp.logical_and(jnp.logical_not(done), last != qb)

    @pl.when(tail)
    def _():
        step(qb, 1)
        stage_c(qb, 1)

    @pl.when(jnp.logical_not(tail))
    def _():
        stage_c(last, 0)

    o_ref[...] = acc_ref[:, :HEAD_DIM] / acc_ref[:, HEAD_DIM:]


def attn_prompt(kind, q, kt, v, f=None):
    t = q.shape[0]
    dq = q.shape[1] // N_HEADS
    tq = kt.shape[3]
    nq = t // tq
    in_specs = [pl.BlockSpec((tq, dq), lambda h, i: (i, h)),
                pl.BlockSpec((1, nq, dq, tq), lambda h, i: (h, 0, 0, 0)),
                pl.BlockSpec((t, HEAD_DIM), lambda h, i: (0, h))]
    args = [q, kt, v]
    if kind == "fox":
        in_specs.append(pl.BlockSpec((1, nq, 1, tq), lambda h, i: (h, 0, 0, 0)))
        args.append(f.reshape(N_HEADS, nq, 1, tq))
    acc_w = HEAD_DIM if kind == "sb" else 2 * HEAD_DIM
    scratch = [pltpu.VMEM((tq, LANES), F32), pltpu.VMEM((tq, acc_w), F32),
               pltpu.VMEM((2, tq, tq), F32), pltpu.VMEM((2, tq, tq), BF16)]
    if kind != "sb":
        scratch.append(pltpu.VMEM((2, tq, LANES), F32))
    scratch.append(pltpu.VMEM((8, LANES), F32))
    return pl.pallas_call(
        functools.partial(_attn_prompt_kernel, kind, tq),
        grid=(N_HEADS, nq),
        in_specs=in_specs,
        out_specs=pl.BlockSpec((tq, HEAD_DIM), lambda h, i: (i, h)),
        out_shape=jax.ShapeDtypeStruct((t, W_GROUP), F32),
        scratch_shapes=scratch,
        compiler_params=_params("parallel", "arbitrary"),
        name="attn_prompt_" + kind,
    )(*args)


def _softmax_tile(s, v, m_prev, l_prev, acc_prev):
    m_new = jnp.maximum(m_prev, jnp.max(s, axis=-1, keepdims=True))
    alpha = jnp.exp2(m_prev - m_new)
    pr = jnp.exp2(s - m_new)
    l_new = alpha * l_prev + jnp.sum(pr, axis=-1, keepdims=True)
    acc_new = alpha * acc_prev + _dot(pr.astype(BF16), v)
    return m_new, l_new, acc_new


def _attn_decode_kernel(kind, native, *refs):
    if kind == "fox":
        q_ref, kn_ref, vn_ref, kc_ref, vc_ref, fn_ref, fc_ref, o_ref = refs
    else:
        q_ref, kn_ref, vn_ref, kc_ref, vc_ref, o_ref = refs
    tq = q_ref.shape[0]
    dq = q_ref.shape[1] // N_HEADS
    past = kc_ref.shape[0] // N_HEADS if native else kc_ref.shape[0]
    qpos = past + lax.broadcasted_iota(jnp.int32, (tq, tq), 0)
    kpos = past + lax.broadcasted_iota(jnp.int32, (tq, tq), 1)
    mask = _tile_mask(kind, qpos, kpos)
    for h in range(N_HEADS):
        q = q_ref[:, h * dq:(h + 1) * dq]
        kn = kn_ref[:, h * dq:(h + 1) * dq]
        vn = vn_ref[:, h * HEAD_DIM:(h + 1) * HEAD_DIM]
        if native:
            kc = kc_ref[pl.ds(h, past, stride=N_HEADS), :].astype(BF16)
            vc = vc_ref[pl.ds(h, past, stride=N_HEADS), :].astype(BF16)
        else:
            kc = kc_ref[:, h * dq:(h + 1) * dq]
            vc = vc_ref[:, h * HEAD_DIM:(h + 1) * HEAD_DIM]
        s_n = _dot_nt(q, kn)
        s_c = _dot_nt(q, kc)
        if kind == "sb":
            a_n, carry = _sb_weights(s_n, jnp.zeros((tq, 1), F32), mask, _tri_lower(tq))
            a_c, _ = _sb_weights(s_c, carry, None, _tri_lower(min(TRI, past)))
            out = _dot(a_n.astype(BF16), vn) + _dot(a_c.astype(BF16), vc)
        else:
            if kind == "fox":
                fref = fn_ref[h][:, :1]
                s_n = s_n + (fref - fn_ref[h]) * LOG2E
                s_c = s_c + (fref - fc_ref[h]) * LOG2E
            s_n = jnp.where(mask, s_n, NEG_INF)
            m0 = jnp.full((tq, 1), NEG_INF, F32)
            z0 = jnp.zeros((tq, 1), F32)
            m, l, acc = _softmax_tile(s_n, vn, m0, z0, jnp.zeros((tq, HEAD_DIM), F32))
            m, l, acc = _softmax_tile(s_c, vc, m, l, acc)
            out = acc / l
        o_ref[:, h * HEAD_DIM:(h + 1) * HEAD_DIM] = out


def attn_decode(kind, q, kn, vn, kc, vc, layer=None, fn=None, fc=None):
    native = layer is not None
    nb, past = (kc.shape[1], kc.shape[2] // N_HEADS) if native else (kc.shape[0], kc.shape[1])
    tq = q.shape[0] // nb
    new = lambda a: pl.BlockSpec((tq, a.shape[1]), lambda b: (b, 0))
    if native:
        old = lambda a: pl.BlockSpec((None, None, past * N_HEADS, HEAD_DIM), lambda b: (layer, b, 0, 0))
    else:
        old = lambda a: pl.BlockSpec((None, past, a.shape[2]), lambda b: (b, 0, 0))
    in_specs = [new(q), new(kn), new(vn), old(kc), old(vc)]
    args = [q, kn, vn, kc, vc]
    if kind == "fox":
        in_specs += [pl.BlockSpec((None, N_HEADS, 1, tq), lambda b: (b, 0, 0, 0)),
                     pl.BlockSpec((None, N_HEADS, 1, past), lambda b: (b, 0, 0, 0))]
        args += [fn, fc]
    return pl.pallas_call(
        functools.partial(_attn_decode_kernel, kind, native),
        grid=(nb,),
        in_specs=in_specs,
        out_specs=pl.BlockSpec((tq, W_GROUP), lambda b: (b, 0)),
        out_shape=jax.ShapeDtypeStruct((nb * tq, W_GROUP), F32),
        compiler_params=_params("parallel"),
        name="attn_decode_" + kind,
    )(*args)


def _out_proj_kernel(oa_ref, ob_ref, oc_ref, ga_ref, gb_ref, gc_ref, w_ref, x_ref, o_ref):
    acc = x_ref[...]
    for g, (o, gn) in enumerate(((oa_ref, ga_ref), (ob_ref, gb_ref), (oc_ref, gc_ref))):
        y = _rms(o[...], gn[...]).astype(BF16)
        acc = acc + _dot(y, w_ref[g * W_GROUP:(g + 1) * W_GROUP, :])
    o_ref[...] = acc


def out_proj(oa, ob, oc, p, x, l):
    m = x.shape[0]
    tm = _pick_tile(m, 512)
    row = lambda w: pl.BlockSpec((tm, w), lambda i: (i, 0))
    full = lambda a: pl.BlockSpec(a.shape, lambda i: (0,) * a.ndim)
    consts = (p["out_norm_a"], p["out_norm_b"], p["out_norm_c"], p["w_out"])
    w_spec = pl.BlockSpec((None,) + p["w_out"].shape[1:], lambda i: (l, 0, 0))
    return pl.pallas_call(
        _out_proj_kernel,
        grid=(m // tm,),
        in_specs=[row(W_GROUP)] * 3 + [full(a) for a in consts[:3]] + [w_spec, row(D_MODEL)],
        out_specs=row(D_MODEL),
        out_shape=jax.ShapeDtypeStruct((m, D_MODEL), F32),
        compiler_params=_params("parallel"),
        name="out_proj",
    )(oa, ob, oc, *consts, x)


def _ffn_kernel(x_ref, g_ref, wg_ref, wu_ref, wd_ref, o_ref, h_ref):
    @pl.when(pl.program_id(1) == 0)
    def _():
        x = x_ref[...]
        h_ref[...] = _rms(x, g_ref[...]).astype(BF16)
        o_ref[...] = x

    h = h_ref[...]
    gate = _dot(h, wg_ref[...])
    up = _dot(h, wu_ref[...])
    act = (gate * jax.nn.sigmoid(gate) * up).astype(BF16)
    o_ref[...] += _dot(act, wd_ref[...])


def ffn(x, g, w_gu, w_down, l):
    m = x.shape[0]
    tm = _pick_tile(m, 1024)
    tf = 512
    nf = D_FF // tf
    return pl.pallas_call(
        _ffn_kernel,
        grid=(m // tm, nf),
        in_specs=[pl.BlockSpec((tm, D_MODEL), lambda i, j: (i, 0)),
                  pl.BlockSpec((1, D_MODEL), lambda i, j: (0, 0)),
                  pl.BlockSpec((None, D_MODEL, tf), lambda i, j: (l, 0, j)),
                  pl.BlockSpec((None, D_MODEL, tf), lambda i, j: (l, 0, j + nf)),
                  pl.BlockSpec((None, tf, D_MODEL), lambda i, j: (l, j, 0))],
        out_specs=pl.BlockSpec((tm, D_MODEL), lambda i, j: (i, 0)),
        out_shape=jax.ShapeDtypeStruct((m, D_MODEL), F32),
        scratch_shapes=[pltpu.VMEM((tm, D_MODEL), BF16)],
        compiler_params=_params("parallel", "arbitrary"),
        name="ffn",
    )(x, g, w_gu, w_gu, w_down)


def _dup_rope(r):
    half = MLA_ROPE_DIM // 2
    return jnp.concatenate([r, r[..., half:], r[..., :half]], axis=-1)


def _prep_layer(l, fox_f_bias, fox_q_norm, fox_k_norm, mla_qa_norm, mla_w_uq, mla_kva_norm, mla_w_ukv,
                mla_q_norm, mla_k_norm, out_norm_a, out_norm_b, out_norm_c, w_in, w_out, w_gu, w_down,
                norm_mix, norm_ffn):
    uq = mla_w_uq[l].reshape(MLA_Q_RANK, N_HEADS, MLA_QK_DIM)
    uq = jnp.concatenate([uq[..., :MLA_NOPE_DIM], _dup_rope(uq[..., MLA_NOPE_DIM:])], axis=-1)
    ukv = mla_w_ukv[l].reshape(MLA_KV_RANK, N_HEADS, 2 * LANES)
    ukv = jnp.concatenate([ukv[..., :LANES].reshape(MLA_KV_RANK, W_GROUP),
                           ukv[..., LANES:].reshape(MLA_KV_RANK, W_GROUP)], axis=1)

    def gain256(g):
        return jnp.concatenate([g[:MLA_NOPE_DIM], _dup_rope(g[MLA_NOPE_DIM:])])[None, :]

    row = lambda a: a[l][None, :]
    return {
        "norm_mix": row(norm_mix), "w_in": w_in,
        "fox_q_norm": row(fox_q_norm), "fox_k_norm": row(fox_k_norm),
        "fox_f_bias": jnp.pad(fox_f_bias[l], (0, LANES - N_HEADS))[None, :],
        "mla_qa_norm": row(mla_qa_norm),
        "mla_w_uq": uq.reshape(MLA_Q_RANK, N_HEADS * MLA_HEAD_PAD).astype(BF16),
        "mla_q_gain": gain256(mla_q_norm[l]) * (MLA_QK_DIM ** -0.5 * LOG2E),
        "mla_kva_norm": row(mla_kva_norm),
        "mla_w_ukv": ukv.astype(BF16),
        "mla_k_gain": gain256(mla_k_norm[l]),
        "out_norm_a": row(out_norm_a), "out_norm_b": row(out_norm_b), "out_norm_c": row(out_norm_c),
        "w_out": w_out,
        "norm_ffn": row(norm_ffn), "w_gu": w_gu, "w_down": w_down,
    }


def _rope_tables(pos):
    half = MLA_ROPE_DIM // 2
    inv_freq = ROPE_THETA ** (-(jnp.arange(half, dtype=F32) / half))
    ang = pos.astype(F32)[:, None] * inv_freq[None, :]
    cos, sin = jnp.cos(ang), jnp.sin(ang)
    zero = jnp.zeros_like(cos)
    return (jnp.concatenate([cos, cos, zero, zero], axis=1),
            jnp.concatenate([-sin, sin, zero, zero], axis=1))


def _layer(x, caches, l, depth, p, tabs, stacks):
    nb, t, _ = x.shape
    m = nb * t
    x2 = x.reshape(m, D_MODEL)
    prompt = caches is None
    outs = inproj_post(x2, p, *tabs["q"], with_kt=prompt, l=l, depth=depth, stacks=stacks)
    stacks = outs[:5]
    qa, vab, lf, qb, vbb, qc, krd, k2a, k2b = outs[5:]
    if prompt:
        kct, vc_new = kexp(stacks[4], krd, p, *tabs["q"], transposed=True, layer=l)
        f = cumsum_rows(lf.reshape(nb, t, LANES))
        oa = attn_prompt("fox", qa, k2a, vab, f[0, :N_HEADS])
        ob = attn_prompt("sb", qb, k2b, vbb)
        oc = attn_prompt("mla", qc, kct, vc_new)
    else:
        c_fk, c_fv, c_lf, c_sk, c_sv, c_ckv, c_kr = caches
        pl_ = c_fk.shape[2]
        kc_new, vc_new = kexp(stacks[4], krd, p, *tabs["q"], transposed=False, layer=l)
        lf_all = jnp.concatenate(
            [jnp.pad(c_lf[l], ((0, 0), (0, 0), (0, LANES - N_HEADS))), lf.reshape(nb, t, LANES)], axis=1)
        n_pad = -(-(pl_ + t) // LANES) * LANES
        lf_all = jnp.pad(lf_all, ((0, 0), (0, n_pad - pl_ - t), (0, 0)))
        f = cumsum_rows(lf_all)[:, :N_HEADS, None, :]
        kc_old, vc_old = kexp(c_ckv.reshape(c_ckv.shape[0], nb * pl_, MLA_KV_RANK),
                              _dup_rope(c_kr[l]).reshape(nb * pl_, LANES), p, *tabs["kc"],
                              transposed=False, layer=l)
        rows4 = lambda a: a.reshape(a.shape[0], nb, pl_ * N_HEADS, HEAD_DIM)
        oa = attn_decode("fox", qa, k2a, vab, rows4(c_fk), rows4(c_fv), layer=l,
                         fn=f[..., pl_:pl_ + t], fc=f[..., :pl_])
        ob = attn_decode("sb", qb, k2b, vbb, rows4(c_sk), rows4(c_sv), layer=l)
        oc = attn_decode("mla", qc, kc_new, vc_new, kc_old.reshape(nb, pl_, -1), vc_old.reshape(nb, pl_, -1))
    x2 = out_proj(oa, ob, oc, p, x2, l)
    x2 = ffn(x2, p["norm_ffn"], p["w_gu"], p["w_down"], l)
    small = (lf[:, :N_HEADS].reshape(nb, t, N_HEADS), krd[:, :MLA_ROPE_DIM].reshape(nb, t, MLA_ROPE_DIM))
    return x2.reshape(nb, t, D_MODEL), stacks, small


def kernel(x_prompt, x_sample, cache_fox_k, cache_fox_v, cache_fox_logf, cache_sb_k, cache_sb_v, cache_mla_ckv, cache_mla_krope, norm_mix, w_in, fox_f_bias, fox_q_norm, fox_k_norm, mla_qa_norm, mla_w_uq, mla_kva_norm, mla_w_ukv, mla_q_norm, mla_k_norm, out_norm_a, out_norm_b, out_norm_c, w_out, norm_ffn, w_gu, w_down):
    depth = w_in.shape[0]
    t_p = x_prompt.shape[1]
    nb_s, t_s = x_sample.shape[0], x_sample.shape[1]
    past_len = cache_fox_k.shape[2]
    tabs_p = {"q": _rope_tables(jnp.arange(t_p, dtype=jnp.int32))}
    tabs_s = {"q": _rope_tables(jnp.tile(past_len + jnp.arange(t_s, dtype=jnp.int32), nb_s)),
              "kc": _rope_tables(jnp.tile(jnp.arange(past_len, dtype=jnp.int32), nb_s))}
    caches = (cache_fox_k, cache_fox_v, cache_fox_logf, cache_sb_k, cache_sb_v, cache_mla_ckv, cache_mla_krope)
    y_p, y_s = x_prompt, x_sample
    rows_p, rows_s = [], []
    def new_stacks(m):
        heads = tuple(jnp.zeros((depth, m, N_HEADS, HEAD_DIM), F32) for _ in range(4))
        return heads + (jnp.zeros((depth, m, MLA_KV_RANK), F32),)

    stacks_p = new_stacks(x_prompt.shape[0] * t_p)
    stacks_s = new_stacks(nb_s * t_s)
    w_in_b = w_in_prep(w_in)
    w_out_b, w_gu_b, w_down_b = w_out.astype(BF16), w_gu.astype(BF16), w_down.astype(BF16)
    for l in range(depth):
        p = _prep_layer(l, fox_f_bias, fox_q_norm, fox_k_norm, mla_qa_norm, mla_w_uq, mla_kva_norm, mla_w_ukv,
                        mla_q_norm, mla_k_norm, out_norm_a, out_norm_b, out_norm_c, w_in_b, w_out_b, w_gu_b,
                        w_down_b, norm_mix, norm_ffn)
        y_p, stacks_p, r_p = _layer(y_p, None, l, depth, p, tabs_p, stacks_p)
        y_s, stacks_s, r_s = _layer(y_s, caches, l, depth, p, tabs_s, stacks_s)
        rows_p.append(r_p)
        rows_s.append(r_s)

    def assemble(stacks, rows, nb, t):
        fk, fv, sk, sv, ckv = stacks
        heads = lambda a: a.reshape(depth, nb, t, N_HEADS, HEAD_DIM)
        return (heads(fk), heads(fv), jnp.stack([r[0] for r in rows], axis=0), heads(sk), heads(sv),
                ckv.reshape(depth, nb, t, MLA_KV_RANK), jnp.stack([r[1] for r in rows], axis=0))

    return ((y_p, y_s) + assemble(stacks_p, rows_p, x_prompt.shape[0], t_p)
            + assemble(stacks_s, rows_s, nb_s, t_s))
```

```python
import functools
import math

import jax
import jax.numpy as jnp
from jax import lax
from jax.experimental import pallas as pl
from jax.experimental.pallas import tpu as pltpu

D_MODEL = 2048
CHUNK = 64
HEAD_DIM = 128
N_HEADS = 4
W_GROUP = N_HEADS * HEAD_DIM
MLA_Q_RANK = 512
MLA_KV_RANK = 256
MLA_NOPE_DIM = 128
MLA_ROPE_DIM = 64
MLA_QK_DIM = MLA_NOPE_DIM + MLA_ROPE_DIM
MLA_HEAD_PAD = 256
ROPE_THETA = 10000.0
D_FF = 5632
RMS_EPS = 1e-6
NEG_INF = -1e30
LOG2E = math.log2(math.e)
N_IN_PAD = 4096
LANES = 128
TRI = 256
ATTN_TILE = 512
SB_TILE = 512
INPROJ_ROWS = 256
PROJ_CHUNK = 1024
ROW_GROUP = 128
STABILISER_LIMIT = 48.0
UNDERFLOW_BITS = 160.0
VMEM_LIMIT = 56 * 1024 * 1024

C_QA, C_KA, C_VA, C_QB, C_KB, C_VB, C_CQ, C_CKV, C_KR, C_FA = (
    0, 512, 1024, 1536, 2048, 2560, 3072, 3584, 3840, 3968)

BF16 = jnp.bfloat16
F32 = jnp.float32


def _params(*sem):
    return pltpu.CompilerParams(dimension_semantics=sem, vmem_limit_bytes=VMEM_LIMIT)


def _pick_tile(n, pref):
    t = min(n, pref)
    while n % t:
        t //= 2
    return t


def _rms(x, g):
    return x * lax.rsqrt(jnp.mean(x * x, axis=-1, keepdims=True) + RMS_EPS) * g


def _dot(a, b):
    return jnp.dot(a, b, preferred_element_type=F32)


def _dot_nt(a, b):
    return lax.dot_general(a, b, (((1,), (1,)), ((), ())), preferred_element_type=F32)


def _split2(x):
    hi = x.astype(BF16)
    lo = (x - hi.astype(F32)).astype(BF16)
    return hi, lo


def _neg_abs(x):
    bits = lax.bitcast_convert_type(x, jnp.uint32) | jnp.uint32(0x80000000)
    return lax.bitcast_convert_type(bits, F32)


def _split3(x):
    a = x.astype(BF16)
    r = x - a.astype(F32)
    b = r.astype(BF16)
    c = (r - b.astype(F32)).astype(BF16)
    return a, b, c


def _log_sigmoid(x):
    return -(jnp.maximum(-x, 0.0) + jnp.log1p(jnp.exp(-jnp.abs(x))))


def _log2_keep(z2):
    return -(jnp.maximum(z2, 0.0) + jnp.log2(1.0 + jnp.exp2(_neg_abs(z2))))


def _w_in_prep_kernel(w_ref, o_ref):
    rows = w_ref.shape[0]
    fa0 = C_QB
    o_ref[:, :C_QB] = w_ref[:, :fa0].astype(BF16)
    o_ref[:, C_QB:C_KR] = w_ref[:, fa0 + N_HEADS:C_KR + N_HEADS].astype(BF16)
    kr = w_ref[:, C_KR + N_HEADS:C_KR + N_HEADS + MLA_ROPE_DIM]
    half = MLA_ROPE_DIM // 2
    o_ref[:, C_KR:C_FA] = jnp.concatenate([kr, kr[:, half:], kr[:, :half]], axis=1).astype(BF16)
    fa = w_ref[:, fa0:fa0 + N_HEADS]
    o_ref[:, C_FA:] = jnp.concatenate([fa, jnp.zeros((rows, LANES - N_HEADS), F32)], axis=1).astype(BF16)


def w_in_prep(w_in):
    depth, d, n = w_in.shape
    tr = 256
    return pl.pallas_call(
        _w_in_prep_kernel,
        grid=(depth, d // tr),
        in_specs=[pl.BlockSpec((None, tr, n), lambda l, i: (l, i, 0))],
        out_specs=pl.BlockSpec((None, tr, N_IN_PAD), lambda l, i: (l, i, 0)),
        out_shape=jax.ShapeDtypeStruct((depth, d, N_IN_PAD), BF16),
        compiler_params=_params("parallel", "parallel"),
        name="w_in_prep",
    )(w_in)


def _rope_chunk(y2, cos, sin):
    return y2 * cos + pltpu.roll(y2, 64, 1) * sin


def _rope_lane_mask():
    return lax.broadcasted_iota(jnp.int32, (1, LANES), 1) < MLA_ROPE_DIM


def _inproj_post_kernel(with_kt, n_alias, x_ref, g_ref, w_ref, fqg_ref, fkg_ref, fb_ref, qag_ref, wuq_ref, qg_ref,
                        kvg_ref, cos_ref, sin_ref, *refs):
    refs = refs[n_alias:]
    (ka_o, va_o, kb_o, vb_o, ckv_o, qa_o, vab_o, lf_o, qb_o, vbb_o, qc_o, kr_o, k2a_o, k2b_o,
     proj_ref) = refs
    hn = _rms(x_ref[...], g_ref[...]).astype(BF16)
    for c in range(N_IN_PAD // PROJ_CHUNK):
        cols = slice(c * PROJ_CHUNK, (c + 1) * PROJ_CHUNK)
        proj_ref[:, cols] = _dot(hn, w_ref[:, cols])
    scale = HEAD_DIM ** -0.5 * LOG2E
    va = proj_ref[:, C_VA:C_VA + W_GROUP]
    vb = proj_ref[:, C_VB:C_VB + W_GROUP]
    for h in range(N_HEADS):
        sl = slice(h * HEAD_DIM, (h + 1) * HEAD_DIM)
        qa = proj_ref[:, C_QA + h * HEAD_DIM:C_QA + (h + 1) * HEAD_DIM]
        qa_o[:, sl] = (_rms(qa, fqg_ref[...]) * scale).astype(BF16)
        ka = _rms(proj_ref[:, C_KA + h * HEAD_DIM:C_KA + (h + 1) * HEAD_DIM], fkg_ref[...])
        kb = proj_ref[:, C_KB + h * HEAD_DIM:C_KB + (h + 1) * HEAD_DIM]
        ka_o[:, h, :] = ka
        kb_o[:, h, :] = kb
        va_o[:, h, :] = va[:, sl]
        vb_o[:, h, :] = vb[:, sl]
        if with_kt:
            k2a_o[h, 0] = ka.T.astype(BF16)
            k2b_o[h, 0] = kb.T.astype(BF16)
        else:
            k2a_o[:, sl] = ka.astype(BF16)
            k2b_o[:, sl] = kb.astype(BF16)
    vab_o[...] = va.astype(BF16)
    qb_o[...] = (proj_ref[:, C_QB:C_QB + W_GROUP] * scale).astype(BF16)
    vbb_o[...] = vb.astype(BF16)

    lane = lax.broadcasted_iota(jnp.int32, (1, LANES), 1)
    lf = _log_sigmoid(proj_ref[:, C_FA:C_FA + LANES] + fb_ref[...])
    lf_o[...] = jnp.where(lane < N_HEADS, lf, 0.0)

    cqn = _rms(proj_ref[:, C_CQ:C_CQ + MLA_Q_RANK], qag_ref[...]).astype(BF16)
    qc = _dot(cqn, wuq_ref[...])
    rmask = _rope_lane_mask()
    cos = cos_ref[...]
    sin = sin_ref[...]
    for h in range(N_HEADS):
        c1 = qc[:, h * MLA_HEAD_PAD:h * MLA_HEAD_PAD + LANES]
        c2 = qc[:, h * MLA_HEAD_PAD + LANES:(h + 1) * MLA_HEAD_PAD]
        ss = (jnp.sum(c1 * c1, axis=-1, keepdims=True)
              + jnp.sum(jnp.where(rmask, c2 * c2, 0.0), axis=-1, keepdims=True))
        rs = lax.rsqrt(ss * (1.0 / MLA_QK_DIM) + RMS_EPS)
        y1 = c1 * rs * qg_ref[:, :LANES]
        y2 = c2 * rs * qg_ref[:, LANES:]
        qc_o[:, h * MLA_HEAD_PAD:h * MLA_HEAD_PAD + LANES] = y1.astype(BF16)
        qc_o[:, h * MLA_HEAD_PAD + LANES:(h + 1) * MLA_HEAD_PAD] = _rope_chunk(y2, cos, sin).astype(BF16)

    ckv_o[...] = _rms(proj_ref[:, C_CKV:C_CKV + MLA_KV_RANK], kvg_ref[...])
    kr_o[...] = proj_ref[:, C_KR:C_KR + LANES]


def inproj_post(x, p, cos, sin, with_kt, l, depth, stacks):
    m = x.shape[0]
    tm = _pick_tile(m, INPROJ_ROWS)
    row =lambda w: pl.BlockSpec((tm, w), lambda i: (i, 0))
    full = lambda a: pl.BlockSpec(a.shape, lambda i: (0,) * a.ndim)
    consts = (p["fox_q_norm"], p["fox_k_norm"], p["fox_f_bias"], p["mla_qa_norm"], p["mla_w_uq"],
              p["mla_q_gain"], p["mla_kva_norm"])
    w_spec = pl.BlockSpec((None, D_MODEL, N_IN_PAD), lambda i: (l, 0, 0), pipeline_mode=pl.Buffered(1))
    heads_spec = pl.BlockSpec((None, tm, N_HEADS, HEAD_DIM), lambda i: (l, i, 0, 0))
    heads_shape = jax.ShapeDtypeStruct((depth, m, N_HEADS, HEAD_DIM), F32)
    out_specs = [heads_spec] * 4 + [pl.BlockSpec((None, tm, MLA_KV_RANK), lambda i: (l, i, 0))]
    out_shape = [heads_shape] * 4 + [jax.ShapeDtypeStruct((depth, m, MLA_KV_RANK), F32)]
    outs = [(W_GROUP, BF16), (W_GROUP, BF16), (LANES, F32), (W_GROUP, BF16), (W_GROUP, BF16),
            (N_HEADS * MLA_HEAD_PAD, BF16), (LANES, F32)]
    out_specs += [row(w) for w, _ in outs]
    out_shape += [jax.ShapeDtypeStruct((m, w), dt) for w, dt in outs]
    for kt_pref in (ATTN_TILE, SB_TILE):
        if with_kt:
            kt_tile = _pick_tile(m, kt_pref)
            sub = kt_tile // tm
            out_specs.append(pl.BlockSpec((N_HEADS, 1, HEAD_DIM, tm),
                                          lambda i, sub=sub: (0, i // sub, 0, i % sub)))
            out_shape.append(jax.ShapeDtypeStruct((N_HEADS, m // kt_tile, HEAD_DIM, kt_tile), BF16))
        else:
            out_specs.append(row(W_GROUP))
            out_shape.append(jax.ShapeDtypeStruct((m, W_GROUP), BF16))
    n_in = 3 + len(consts) + 2
    stacks = () if stacks is None else tuple(stacks)
    return pl.pallas_call(
        functools.partial(_inproj_post_kernel, with_kt, len(stacks)),
        grid=(m // tm,),
        in_specs=([row(D_MODEL), full(p["norm_mix"]), w_spec] + [full(a) for a in consts]
                  + [row(LANES), row(LANES)] + [pl.BlockSpec(memory_space=pl.ANY)] * len(stacks)),
        out_specs=out_specs,
        out_shape=out_shape,
        scratch_shapes=[pltpu.VMEM((tm, N_IN_PAD), F32)],
        input_output_aliases={n_in + k: k for k in range(len(stacks))},
        compiler_params=_params("parallel"),
        name="inproj_post",
    )(x, p["norm_mix"], p["w_in"], *consts, cos, sin, *stacks)


def _kexp_kernel(transposed, ckv_ref, kr_ref, w_ref, kg_ref, cos_ref, sin_ref, kc_o, vc_o):
    kv = _dot(ckv_ref[...].astype(BF16), w_ref[...])
    kr = kr_ref[...]
    krsq = jnp.sum(jnp.where(_rope_lane_mask(), kr * kr, 0.0), axis=-1, keepdims=True)
    cos = cos_ref[...]
    sin = sin_ref[...]
    for h in range(N_HEADS):
        n = kv[:, h * LANES:(h + 1) * LANES]
        ss = jnp.sum(n * n, axis=-1, keepdims=True) + krsq
        rs = lax.rsqrt(ss * (1.0 / MLA_QK_DIM) + RMS_EPS)
        k1 = n * rs * kg_ref[:, :LANES]
        k2 = _rope_chunk(kr * rs * kg_ref[:, LANES:], cos, sin)
        if transposed:
            kc_o[h, 0, :LANES, :] = k1.T.astype(BF16)
            kc_o[h, 0, LANES:, :] = k2.T.astype(BF16)
        else:
            kc_o[:, h * MLA_HEAD_PAD:h * MLA_HEAD_PAD + LANES] = k1.astype(BF16)
            kc_o[:, h * MLA_HEAD_PAD + LANES:(h + 1) * MLA_HEAD_PAD] = k2.astype(BF16)
    vc_o[...] = kv[:, W_GROUP:].astype(BF16)


def kexp(ckv, krdup, p, cos, sin, transposed, layer=None):
    m = krdup.shape[0]
    tm = _pick_tile(m, ATTN_TILE)
    row = lambda w: pl.BlockSpec((tm, w), lambda i: (i, 0))
    full = lambda a: pl.BlockSpec(a.shape, lambda i: (0,) * a.ndim)
    ckv_spec = row(MLA_KV_RANK) if layer is None else pl.BlockSpec(
        (None, tm, MLA_KV_RANK), lambda i: (layer, i, 0))
    if transposed:
        kc_spec = pl.BlockSpec((N_HEADS, 1, MLA_HEAD_PAD, tm), lambda i: (0, i, 0, 0))
        kc_shape = jax.ShapeDtypeStruct((N_HEADS, m // tm, MLA_HEAD_PAD, tm), BF16)
    else:
        kc_spec = row(N_HEADS * MLA_HEAD_PAD)
        kc_shape = jax.ShapeDtypeStruct((m, N_HEADS * MLA_HEAD_PAD), BF16)
    return pl.pallas_call(
        functools.partial(_kexp_kernel, transposed),
        grid=(m // tm,),
        in_specs=[ckv_spec, row(LANES), full(p["mla_w_ukv"]), full(p["mla_k_gain"]), row(LANES), row(LANES)],
        out_specs=[kc_spec, row(W_GROUP)],
        out_shape=[kc_shape, jax.ShapeDtypeStruct((m, W_GROUP), BF16)],
        compiler_params=_params("parallel"),
        name="kexp",
    )(ckv, krdup, p["mla_w_ukv"], p["mla_k_gain"], cos, sin)


def _cumsum_kernel(x_ref, o_ref, carry_ref):
    @pl.when(pl.program_id(1) == 0)
    def _():
        carry_ref[...] = jnp.zeros_like(carry_ref)

    t = x_ref.shape[1]
    xt = x_ref[0].T[:8, :]
    ri = lax.broadcasted_iota(jnp.int32, (t, t), 0)
    ci = lax.broadcasted_iota(jnp.int32, (t, t), 1)
    upper = jnp.where(ri <= ci, 1.0, 0.0).astype(BF16)
    a, b, c = _split3(xt)
    f = _dot(a, upper) + _dot(b, upper) + _dot(c, upper) + carry_ref[:, :1]
    o_ref[0] = f
    carry_ref[...] = jnp.broadcast_to(f[:, t - 1:t], carry_ref.shape)


def cumsum_rows(x):
    b, n, _ = x.shape
    t = n if n <= 1280 else _pick_tile(n, 512)
    return pl.pallas_call(
        _cumsum_kernel,
        grid=(b, n // t),
        in_specs=[pl.BlockSpec((1, t, LANES), lambda i, j: (i, j, 0))],
        out_specs=pl.BlockSpec((1, 8, t), lambda i, j: (i, 0, j)),
        out_shape=jax.ShapeDtypeStruct((b, 8, n), F32),
        scratch_shapes=[pltpu.VMEM((8, LANES), F32)],
        compiler_params=_params("parallel", "arbitrary"),
        name="cumsum_rows",
    )(x)


def _tile_mask(kind, qpos, kpos):
    if kind == "fox":
        return kpos <= qpos
    if kind == "sb":
        return kpos < qpos
    shift = CHUNK.bit_length() - 1
    return lax.shift_right_logical(kpos, shift) <= lax.shift_right_logical(qpos, shift)


def _tri_lower(n):
    ri = lax.broadcasted_iota(jnp.int32, (n, n), 0)
    ci = lax.broadcasted_iota(jnp.int32, (n, n), 1)
    return jnp.where(ri >= ci, 1.0, 0.0).astype(BF16)


def _sb_weights(z, carry, mask, tri):
    lk = _log2_keep(z)
    if mask is not None:
        lk = jnp.where(mask, lk, 0.0)
    blk = tri.shape[0]
    nblk = z.shape[1] // blk
    parts = [None] * nblk
    for c in reversed(range(nblk)):
        lkc = lk[:, c * blk:(c + 1) * blk]
        hi, lo = _split2(lkc)
        intra = _dot(hi, tri) + _dot(lo, tri)
        if carry.shape[1] == 1 or carry.shape[1] == blk:
            parts[c] = intra + carry
        else:
            parts[c] = intra + jnp.concatenate([carry] * (blk // carry.shape[1]), axis=1)
        carry = carry + jnp.sum(lkc, axis=-1, keepdims=True)
    r = parts[0] if nblk == 1 else jnp.concatenate(parts, axis=-1)
    a = jnp.exp2(z + r)
    if mask is not None:
        a = jnp.where(mask, a, 0.0)
    return a, carry


def _attn_prompt_kernel(kind, tq, *refs):
    f_ref = al_ref = kmax_ref = None
    if kind == "fox":
        q_ref, kt_ref, v_ref, f_ref, o_ref, m_ref, acc_ref, s_ref, p_ref, al_ref, kmax_ref = refs
    elif kind == "mla":
        q_ref, kt_ref, v_ref, o_ref, m_ref, acc_ref, s_ref, p_ref, al_ref, kmax_ref = refs
    else:
        q_ref, kt_ref, v_ref, o_ref, m_ref, acc_ref, s_ref, p_ref, kmax_ref = refs
    qb = pl.program_id(1)
    rg = min(TRI if kind == "sb" else ROW_GROUP, tq)
    n_rg = tq // rg
    q = q_ref[...]
    ones = jnp.ones((tq, LANES), BF16)
    tri2 = jnp.concatenate([_tri_lower(rg)] * 2, axis=0) if kind == "sb" else None
    fref = f_ref[0, qb][:, :1] if kind == "fox" else None

    def key_block(j):
        return jnp.clip(qb - j, 0, qb)

    def stage_a(j, slot):
        s_ref[slot] = _dot(q, kt_ref[0, key_block(j)])

    def stage_b(j, slot, diagonal, fixed=False):
        brow = (fref - f_ref[0, key_block(j)]) * LOG2E if kind == "fox" else None
        ms, als, prs = [], [], []
        for r in range(n_rg):
            rows = slice(r * rg, (r + 1) * rg)
            kw = (r + 1) * rg if diagonal else tq
            s = s_ref[slot, rows, :kw]
            mask = None
            if diagonal:
                qpos = r * rg + lax.broadcasted_iota(jnp.int32, (rg, kw), 0)
                kpos = lax.broadcasted_iota(jnp.int32, (rg, kw), 1)
                mask = _tile_mask(kind, qpos, kpos)
            m_prev = m_ref[rows, :]
            if kind == "sb":
                plk = jnp.maximum(s, 0.0) + jnp.log2(1.0 + jnp.exp2(_neg_abs(s)))
                if mask is not None:
                    plk = jnp.where(mask, plk, 0.0)
                carry = m_prev
                parts = [None] * (kw // rg)
                for c in reversed(range(kw // rg)):
                    pc = plk[:, c * rg:(c + 1) * rg]
                    hi, lo = _split2(pc)
                    later = _dot(jnp.concatenate([hi, lo], axis=1), tri2)
                    parts[c] = s[:, c * rg:(c + 1) * rg] - later - jnp.concatenate([carry] * (rg // LANES), 1)
                    carry = carry + jnp.sum(pc, axis=-1, keepdims=True)
                a = jnp.exp2(parts[0] if len(parts) == 1 else jnp.concatenate(parts, axis=1))
                if mask is not None:
                    a = jnp.where(mask, a, 0.0)
                pr = a.astype(BF16)
                ms.append(carry)
            else:
                if kind == "fox":
                    s = s + brow[:, :kw]
                if mask is not None:
                    s = jnp.where(mask, s, NEG_INF)
                chunks = [s[:, c * LANES:(c + 1) * LANES] for c in range(kw // LANES)]
                if fixed:
                    m_new = zb[rows, :]
                else:
                    m_cur = jnp.max(functools.reduce(jnp.maximum, chunks), axis=-1, keepdims=True)
                    m_new = jnp.maximum(m_prev, m_cur)
                    als.append(jnp.exp2(m_prev - m_new))
                    ms.append(m_new)
                pr = jnp.concatenate([jnp.exp2(c - m_new).astype(BF16) for c in chunks], axis=1)
            if kw < tq:
                pr = jnp.concatenate([pr, jnp.zeros((rg, tq - kw), BF16)], axis=1)
            prs.append(pr)
        p_ref[slot] = jnp.concatenate(prs, axis=0)
        if not fixed:
            m_ref[...] = jnp.concatenate(ms, axis=0)
            if kind != "sb":
                al_ref[slot] = jnp.concatenate(als, axis=0)

    def stage_c(j, slot, fixed=False):
        start = pl.multiple_of(key_block(j) * tq, tq)
        v = v_ref[pl.ds(start, tq), :]
        if kind == "sb":
            acc_ref[...] += _dot(p_ref[slot], v)
        else:
            pv = _dot(p_ref[slot], jnp.concatenate([v, ones], axis=1))
            if fixed:
                acc_ref[...] += pv
            else:
                al = al_ref[slot]
                acc_ref[...] = jnp.concatenate([al, al], axis=1) * acc_ref[...] + pv

    def step(j, slot, fixed=False):
        stage_b(j, slot, False, fixed)
        stage_a(j + 1, 1 - slot)
        stage_c(j - 1, 1 - slot, fixed)

    @pl.when(qb == 0)
    def _():
        def norm_body(i, mx):
            kt = kt_ref[0, i].astype(F32)
            return jnp.maximum(mx, jnp.max(jnp.sum(kt * kt, axis=0, keepdims=True), axis=1, keepdims=True))

        mx = lax.fori_loop(0, kt_ref.shape[1], norm_body, jnp.zeros((1, 1), F32))
        kmax_ref[...] = jnp.broadcast_to(jnp.sqrt(mx), kmax_ref.shape)

    qf = q.astype(F32)
    zb = jnp.sqrt(jnp.sum(qf * qf, axis=1, keepdims=True)) * kmax_ref[:1, :]

    if kind == "mla":
        def dense(fixed, unroll):
            m_ref[...] = jnp.full(m_ref.shape, NEG_INF, F32)
            acc_ref[...] = jnp.zeros_like(acc_ref)
            stage_a(0, 0)
            stage_b(0, 0, True, fixed)
            stage_a(1, 1)

            def body(t, c):
                for u in range(unroll):
                    step(unroll * t + 1 + u, (1 + u) % 2, fixed)
                return c

            trips = qb // unroll
            lax.fori_loop(0, trips, body, 0)
            for u in range(unroll - 1):
                @pl.when(qb - unroll * trips > u)
                def _():
                    step(unroll * trips + 1 + u, (1 + u) % 2, fixed)

            for slot in range(2):
                @pl.when(qb % 2 == slot)
                def _():
                    stage_c(qb, slot, fixed)

            o_ref[...] = (acc_ref[:, :HEAD_DIM] / acc_ref[:, HEAD_DIM:]).astype(o_ref.dtype)

        small = jnp.max(zb) < STABILISER_LIMIT

        @pl.when(small)
        def _():
            dense(True, 4)

        @pl.when(jnp.logical_not(small))
        def _():
            dense(False, 2)

        return

    def exhausted(j):
        if kind == "sb":
            return jnp.min(m_ref[...] - zb) > UNDERFLOW_BITS
        brow_max = jnp.max((fref - f_ref[0, key_block(j + 1)]) * LOG2E)
        return jnp.min(m_ref[...] - zb) - brow_max > UNDERFLOW_BITS

    m_ref[...] = jnp.full(m_ref.shape, 0.0 if kind == "sb" else NEG_INF, F32)
    acc_ref[...] = jnp.zeros_like(acc_ref)
    stage_a(0, 0)
    stage_b(0, 0, True)
    stage_a(1, 1)

    if kind == "sb":
        def cond1(c):
            return jnp.logical_and(c[0] <= qb, jnp.logical_not(c[1]))

        def single(c):
            j = c[0]
            step(j, j & 1)
            return j + 1, exhausted(j)

        nxt, _ = lax.while_loop(cond1, single, (jnp.int32(1), jnp.bool_(False)))
        stage_c(nxt - 1, (nxt - 1) & 1)
        o_ref[...] = acc_ref[...].astype(o_ref.dtype)
        return

    def cond(c):
        return jnp.logical_and(c[0] < qb // 2, jnp.logical_not(c[1]))

    def pair(c):
        t = c[0]
        step(1 + 2 * t, 1)
        step(2 + 2 * t, 0)
        return t + 1, exhausted(2 + 2 * t)

    pairs, done = lax.while_loop(cond, pair, (jnp.int32(0), jnp.bool_(False)))
    last = 2 * pairs
    tail = jnp.logical_and(jnp.logical_not(done), last != qb)

    @pl.when(tail)
    def _():
        step(qb, 1)
        stage_c(qb, 1)

    @pl.when(jnp.logical_not(tail))
    def _():
        stage_c(last, 0)

    o_ref[...] = (acc_ref[:, :HEAD_DIM] / acc_ref[:, HEAD_DIM:]).astype(o_ref.dtype)


def attn_prompt(kind, q, kt, v, f=None):
    t = q.shape[0]
    dq = q.shape[1] // N_HEADS
    tq = kt.shape[3]
    nq = t // tq
    in_specs = [pl.BlockSpec((tq, dq), lambda h, i: (i, h)),
                pl.BlockSpec((1, nq, dq, tq), lambda h, i: (h, 0, 0, 0)),
                pl.BlockSpec((t, HEAD_DIM), lambda h, i: (0, h))]
    args = [q, kt, v]
    if kind == "fox":
        in_specs.append(pl.BlockSpec((1, nq, 1, tq), lambda h, i: (h, 0, 0, 0)))
        args.append(f.reshape(N_HEADS, nq, 1, tq))
    acc_w = HEAD_DIM if kind == "sb" else 2 * HEAD_DIM
    scratch = [pltpu.VMEM((tq, LANES), F32), pltpu.VMEM((tq, acc_w), F32),
               pltpu.VMEM((2, tq, tq), F32), pltpu.VMEM((2, tq, tq), BF16)]
    if kind != "sb":
        scratch.append(pltpu.VMEM((2, tq, LANES), F32))
    scratch.append(pltpu.VMEM((8, LANES), F32))
    return pl.pallas_call(
        functools.partial(_attn_prompt_kernel, kind, tq),
        grid=(N_HEADS, nq),
        in_specs=in_specs,
        out_specs=pl.BlockSpec((tq, HEAD_DIM), lambda h, i: (i, h)),
        out_shape=jax.ShapeDtypeStruct((t, W_GROUP), BF16),
        scratch_shapes=scratch,
        compiler_params=_params("parallel", "arbitrary"),
        name="attn_prompt_" + kind,
    )(*args)


def _softmax_tile(s, v, m_prev, l_prev, acc_prev):
    m_new = jnp.maximum(m_prev, jnp.max(s, axis=-1, keepdims=True))
    alpha = jnp.exp2(m_prev - m_new)
    pr = jnp.exp2(s - m_new)
    l_new = alpha * l_prev + jnp.sum(pr, axis=-1, keepdims=True)
    acc_new = alpha * acc_prev + _dot(pr.astype(BF16), v)
    return m_new, l_new, acc_new


def _attn_decode_kernel(kind, native, *refs):
    if kind == "fox":
        q_ref, kn_ref, vn_ref, kc_ref, vc_ref, fn_ref, fc_ref, o_ref = refs
    else:
        q_ref, kn_ref, vn_ref, kc_ref, vc_ref, o_ref = refs
    tq = q_ref.shape[0]
    dq = q_ref.shape[1] // N_HEADS
    past = kc_ref.shape[0] // N_HEADS if native else kc_ref.shape[0]
    qpos = past + lax.broadcasted_iota(jnp.int32, (tq, tq), 0)
    kpos = past + lax.broadcasted_iota(jnp.int32, (tq, tq), 1)
    mask = _tile_mask(kind, qpos, kpos)
    for h in range(N_HEADS):
        q = q_ref[:, h * dq:(h + 1) * dq]
        kn = kn_ref[:, h * dq:(h + 1) * dq]
        vn = vn_ref[:, h * HEAD_DIM:(h + 1) * HEAD_DIM]
        if native:
            kc = kc_ref[pl.ds(h, past, stride=N_HEADS), :].astype(BF16)
            vc = vc_ref[pl.ds(h, past, stride=N_HEADS), :].astype(BF16)
        else:
            kc = kc_ref[:, h * dq:(h + 1) * dq]
            vc = vc_ref[:, h * HEAD_DIM:(h + 1) * HEAD_DIM]
        s_n = _dot_nt(q, kn)
        s_c = _dot_nt(q, kc)
        if kind == "sb":
            a_n, carry = _sb_weights(s_n, jnp.zeros((tq, 1), F32), mask, _tri_lower(tq))
            a_c, _ = _sb_weights(s_c, carry, None, _tri_lower(min(TRI, past)))
            out = _dot(a_n.astype(BF16), vn) + _dot(a_c.astype(BF16), vc)
        else:
            if kind == "fox":
                fref = fn_ref[h][:, :1]
                s_n = s_n + (fref - fn_ref[h]) * LOG2E
                s_c = s_c + (fref - fc_ref[h]) * LOG2E
            s_n = jnp.where(mask, s_n, NEG_INF)
            m0 = jnp.full((tq, 1), NEG_INF, F32)
            z0 = jnp.zeros((tq, 1), F32)
            m, l, acc = _softmax_tile(s_n, vn, m0, z0, jnp.zeros((tq, HEAD_DIM), F32))
            m, l, acc = _softmax_tile(s_c, vc, m, l, acc)
            out = acc / l
        o_ref[:, h * HEAD_DIM:(h + 1) * HEAD_DIM] = out.astype(o_ref.dtype)


def attn_decode(kind, q, kn, vn, kc, vc, layer=None, fn=None, fc=None):
    native = layer is not None
    nb, past = (kc.shape[1], kc.shape[2] // N_HEADS) if native else (kc.shape[0], kc.shape[1])
    tq = q.shape[0] // nb
    new = lambda a: pl.BlockSpec((tq, a.shape[1]), lambda b: (b, 0))
    if native:
        old = lambda a: pl.BlockSpec((None, None, past * N_HEADS, HEAD_DIM), lambda b: (layer, b, 0, 0))
    else:
        old = lambda a: pl.BlockSpec((None, past, a.shape[2]), lambda b: (b, 0, 0))
    in_specs = [new(q), new(kn), new(vn), old(kc), old(vc)]
    args = [q, kn, vn, kc, vc]
    if kind == "fox":
        in_specs += [pl.BlockSpec((None, N_HEADS, 1, tq), lambda b: (b, 0, 0, 0)),
                     pl.BlockSpec((None, N_HEADS, 1, past), lambda b: (b, 0, 0, 0))]
        args += [fn, fc]
    return pl.pallas_call(
        functools.partial(_attn_decode_kernel, kind, native),
        grid=(nb,),
        in_specs=in_specs,
        out_specs=pl.BlockSpec((tq, W_GROUP), lambda b: (b, 0)),
        out_shape=jax.ShapeDtypeStruct((nb * tq, W_GROUP), BF16),
        compiler_params=_params("parallel"),
        name="attn_decode_" + kind,
    )(*args)


def _out_proj_kernel(oa_ref, ob_ref, oc_ref, ga_ref, gb_ref, gc_ref, w_ref, x_ref, o_ref):
    acc = x_ref[...]
    for g, (o, gn) in enumerate(((oa_ref, ga_ref), (ob_ref, gb_ref), (oc_ref, gc_ref))):
        y = _rms(o[...].astype(F32), gn[...]).astype(BF16)
        acc = acc + _dot(y, w_ref[g * W_GROUP:(g + 1) * W_GROUP, :])
    o_ref[...] = acc


def out_proj(oa, ob, oc, p, x, l):
    m = x.shape[0]
    tm = _pick_tile(m, 512)
    row = lambda w: pl.BlockSpec((tm, w), lambda i: (i, 0))
    full = lambda a: pl.BlockSpec(a.shape, lambda i: (0,) * a.ndim)
    consts = (p["out_norm_a"], p["out_norm_b"], p["out_norm_c"], p["w_out"])
    w_spec = pl.BlockSpec((None,) + p["w_out"].shape[1:], lambda i: (l, 0, 0))
    return pl.pallas_call(
        _out_proj_kernel,
        grid=(m // tm,),
        in_specs=[row(W_GROUP)] * 3 + [full(a) for a in consts[:3]] + [w_spec, row(D_MODEL)],
        out_specs=row(D_MODEL),
        out_shape=jax.ShapeDtypeStruct((m, D_MODEL), F32),
        compiler_params=_params("parallel"),
        name="out_proj",
    )(oa, ob, oc, *consts, x)


def _ffn_kernel(x_ref, g_ref, wg_ref, wu_ref, wd_ref, o_ref, h_ref):
    @pl.when(pl.program_id(1) == 0)
    def _():
        x = x_ref[...]
        h_ref[...] = _rms(x, g_ref[...]).astype(BF16)
        o_ref[...] = x

    h = h_ref[...]
    gate = _dot(h, wg_ref[...])
    up = _dot(h, wu_ref[...])
    act = (gate * jax.nn.sigmoid(gate) * up).astype(BF16)
    o_ref[...] += _dot(act, wd_ref[...])


def ffn(x, g, w_gu, w_down, l):
    m = x.shape[0]
    tm = _pick_tile(m, 1024)
    tf = 512
    nf = D_FF // tf
    return pl.pallas_call(
        _ffn_kernel,
        grid=(m // tm, nf),
        in_specs=[pl.BlockSpec((tm, D_MODEL), lambda i, j: (i, 0)),
                  pl.BlockSpec((1, D_MODEL), lambda i, j: (0, 0)),
                  pl.BlockSpec((None, D_MODEL, tf), lambda i, j: (l, 0, j)),
                  pl.BlockSpec((None, D_MODEL, tf), lambda i, j: (l, 0, j + nf)),
                  pl.BlockSpec((None, tf, D_MODEL), lambda i, j: (l, j, 0))],
        out_specs=pl.BlockSpec((tm, D_MODEL), lambda i, j: (i, 0)),
        out_shape=jax.ShapeDtypeStruct((m, D_MODEL), F32),
        scratch_shapes=[pltpu.VMEM((tm, D_MODEL), BF16)],
        compiler_params=_params("parallel", "arbitrary"),
        name="ffn",
    )(x, g, w_gu, w_gu, w_down)


def _dup_rope(r):
    half = MLA_ROPE_DIM // 2
    return jnp.concatenate([r, r[..., half:], r[..., :half]], axis=-1)


def _prep_layer(l, fox_f_bias, fox_q_norm, fox_k_norm, mla_qa_norm, mla_w_uq, mla_kva_norm, mla_w_ukv,
                mla_q_norm, mla_k_norm, out_norm_a, out_norm_b, out_norm_c, w_in, w_out, w_gu, w_down,
                norm_mix, norm_ffn):
    uq = mla_w_uq[l].reshape(MLA_Q_RANK, N_HEADS, MLA_QK_DIM)
    uq = jnp.concatenate([uq[..., :MLA_NOPE_DIM], _dup_rope(uq[..., MLA_NOPE_DIM:])], axis=-1)
    ukv = mla_w_ukv[l].reshape(MLA_KV_RANK, N_HEADS, 2 * LANES)
    ukv = jnp.concatenate([ukv[..., :LANES].reshape(MLA_KV_RANK, W_GROUP),
                           ukv[..., LANES:].reshape(MLA_KV_RANK, W_GROUP)], axis=1)

    def gain256(g):
        return jnp.concatenate([g[:MLA_NOPE_DIM], _dup_rope(g[MLA_NOPE_DIM:])])[None, :]

    row = lambda a: a[l][None, :]
    return {
        "norm_mix": row(norm_mix), "w_in": w_in,
        "fox_q_norm": row(fox_q_norm), "fox_k_norm": row(fox_k_norm),
        "fox_f_bias": jnp.pad(fox_f_bias[l], (0, LANES - N_HEADS))[None, :],
        "mla_qa_norm": row(mla_qa_norm),
        "mla_w_uq": uq.reshape(MLA_Q_RANK, N_HEADS * MLA_HEAD_PAD).astype(BF16),
        "mla_q_gain": gain256(mla_q_norm[l]) * (MLA_QK_DIM ** -0.5 * LOG2E),
        "mla_kva_norm": row(mla_kva_norm),
        "mla_w_ukv": ukv.astype(BF16),
        "mla_k_gain": gain256(mla_k_norm[l]),
        "out_norm_a": row(out_norm_a), "out_norm_b": row(out_norm_b), "out_norm_c": row(out_norm_c),
        "w_out": w_out,
        "norm_ffn": row(norm_ffn), "w_gu": w_gu, "w_down": w_down,
    }


def _rope_tables(pos):
    half = MLA_ROPE_DIM // 2
    inv_freq = ROPE_THETA ** (-(jnp.arange(half, dtype=F32) / half))
    ang = pos.astype(F32)[:, None] * inv_freq[None, :]
    cos, sin = jnp.cos(ang), jnp.sin(ang)
    zero = jnp.zeros_like(cos)
    return (jnp.concatenate([cos, cos, zero, zero], axis=1),
            jnp.concatenate([-sin, sin, zero, zero], axis=1))


def _layer(x, caches, l, depth, p, tabs, stacks):
    nb, t, _ = x.shape
    m = nb * t
    x2 = x.reshape(m, D_MODEL)
    prompt = caches is None
    outs = inproj_post(x2, p, *tabs["q"], with_kt=prompt, l=l, depth=depth, stacks=stacks)
    stacks = outs[:5]
    qa, vab, lf, qb, vbb, qc, krd, k2a, k2b = outs[5:]
    if prompt:
        kct, vc_new = kexp(stacks[4], krd, p, *tabs["q"], transposed=True, layer=l)
        f = cumsum_rows(lf.reshape(nb, t, LANES))
        oa = attn_prompt("fox", qa, k2a, vab, f[0, :N_HEADS])
        ob = attn_prompt("sb", qb, k2b, vbb)
        oc = attn_prompt("mla", qc, kct, vc_new)
    else:
        c_fk, c_fv, c_lf, c_sk, c_sv, c_ckv, c_kr = caches
        pl_ = c_fk.shape[2]
        kc_new, vc_new = kexp(stacks[4], krd, p, *tabs["q"], transposed=False, layer=l)
        lf_all = jnp.concatenate(
            [jnp.pad(c_lf[l], ((0, 0), (0, 0), (0, LANES - N_HEADS))), lf.reshape(nb, t, LANES)], axis=1)
        n_pad = -(-(pl_ + t) // LANES) * LANES
        lf_all = jnp.pad(lf_all, ((0, 0), (0, n_pad - pl_ - t), (0, 0)))
        f = cumsum_rows(lf_all)[:, :N_HEADS, None, :]
        kc_old, vc_old = kexp(c_ckv.reshape(c_ckv.shape[0], nb * pl_, MLA_KV_RANK),
                              _dup_rope(c_kr[l]).reshape(nb * pl_, LANES), p, *tabs["kc"],
                              transposed=False, layer=l)
        rows4 = lambda a: a.reshape(a.shape[0], nb, pl_ * N_HEADS, HEAD_DIM)
        oa = attn_decode("fox", qa, k2a, vab, rows4(c_fk), rows4(c_fv), layer=l,
                         fn=f[..., pl_:pl_ + t], fc=f[..., :pl_])
        ob = attn_decode("sb", qb, k2b, vbb, rows4(c_sk), rows4(c_sv), layer=l)
        oc = attn_decode("mla", qc, kc_new, vc_new, kc_old.reshape(nb, pl_, -1), vc_old.reshape(nb, pl_, -1))
    x2 = out_proj(oa, ob, oc, p, x2, l)
    x2 = ffn(x2, p["norm_ffn"], p["w_gu"], p["w_down"], l)
    small = (lf[:, :N_HEADS].reshape(nb, t, N_HEADS), krd[:, :MLA_ROPE_DIM].reshape(nb, t, MLA_ROPE_DIM))
    return x2.reshape(nb, t, D_MODEL), stacks, small


def kernel(x_prompt, x_sample, cache_fox_k, cache_fox_v, cache_fox_logf, cache_sb_k, cache_sb_v, cache_mla_ckv, cache_mla_krope, norm_mix, w_in, fox_f_bias, fox_q_norm, fox_k_norm, mla_qa_norm, mla_w_uq, mla_kva_norm, mla_w_ukv, mla_q_norm, mla_k_norm, out_norm_a, out_norm_b, out_norm_c, w_out, norm_ffn, w_gu, w_down):
    depth = w_in.shape[0]
    t_p = x_prompt.shape[1]
    nb_s, t_s = x_sample.shape[0], x_sample.shape[1]
    past_len = cache_fox_k.shape[2]
    tabs_p = {"q": _rope_tables(jnp.arange(t_p, dtype=jnp.int32))}
    tabs_s = {"q": _rope_tables(jnp.tile(past_len + jnp.arange(t_s, dtype=jnp.int32), nb_s)),
              "kc": _rope_tables(jnp.tile(jnp.arange(past_len, dtype=jnp.int32), nb_s))}
    caches = (cache_fox_k, cache_fox_v, cache_fox_logf, cache_sb_k, cache_sb_v, cache_mla_ckv, cache_mla_krope)
    y_p, y_s = x_prompt, x_sample
    rows_p, rows_s = [], []
    def new_stacks(m):
        heads = tuple(jnp.zeros((depth, m, N_HEADS, HEAD_DIM), F32) for _ in range(4))
        return heads + (jnp.zeros((depth, m, MLA_KV_RANK), F32),)

    stacks_p = new_stacks(x_prompt.shape[0] * t_p)
    stacks_s = new_stacks(nb_s * t_s)
    w_in_b = w_in_prep(w_in)
    w_out_b, w_gu_b, w_down_b = w_out.astype(BF16), w_gu.astype(BF16), w_down.astype(BF16)
    for l in range(depth):
        p = _prep_layer(l, fox_f_bias, fox_q_norm, fox_k_norm, mla_qa_norm, mla_w_uq, mla_kva_norm, mla_w_ukv,
                        mla_q_norm, mla_k_norm, out_norm_a, out_norm_b, out_norm_c, w_in_b, w_out_b, w_gu_b,
                        w_down_b, norm_mix, norm_ffn)
        y_p, stacks_p, r_p = _layer(y_p, None, l, depth, p, tabs_p, stacks_p)
        y_s, stacks_s, r_s = _layer(y_s, caches, l, depth, p, tabs_s, stacks_s)
        rows_p.append(r_p)
        rows_s.append(r_s)

    def assemble(stacks, rows, nb, t):
        fk, fv, sk, sv, ckv = stacks
        heads = lambda a: a.reshape(depth, nb, t, N_HEADS, HEAD_DIM)
        return (heads(fk), heads(fv), jnp.stack([r[0] for r in rows], axis=0), heads(sk), heads(sv),
                ckv.reshape(depth, nb, t, MLA_KV_RANK), jnp.stack([r[1] for r in rows], axis=0))

    return ((y_p, y_s) + assemble(stacks_p, rows_p, x_prompt.shape[0], t_p)
            + assemble(stacks_s, rows_s, nb_s, t_s))
```
